```python
import math
import jax, jax.numpy as jnp
from jax import lax
import numpy as np

D_MODEL = 1024
BATCH = 4
SEQ = 4096
DEPTH = 4
DEC_BATCH = 128
DEC_SEQ = 8
PAST_LEN = 8192
PAGE_SIZE = 128

HEAD_DIM = 64
ATT_WIDTH = D_MODEL // 2
RWKV_WIDTH = D_MODEL - ATT_WIDTH
N_Q_HEADS = ATT_WIDTH // HEAD_DIM
N_KV_HEADS = 2
Q_PER_KV = N_Q_HEADS // N_KV_HEADS
KV_WIDTH = N_KV_HEADS * HEAD_DIM
WINDOW = 128
ATT_BLOCK = WINDOW
ATT_SCALE = HEAD_DIM ** -0.5
N_RWKV_HEADS = RWKV_WIDTH // HEAD_DIM
D_DECAY_LORA = 64
D_AAA_LORA = 64
D_GATE_LORA = 128
D_MV_LORA = 32
RWKV_SIZES = (RWKV_WIDTH, D_DECAY_LORA, RWKV_WIDTH, RWKV_WIDTH, D_AAA_LORA, D_GATE_LORA)
N_ATT_COLS = ATT_WIDTH + 2 * KV_WIDTH
N_RWKV_COLS = sum(RWKV_SIZES)
N_IN_COLS = N_ATT_COLS + N_RWKV_COLS
DECAY_SCALE = math.exp(-0.5)
GN_EPS = 64e-5
LN_EPS = 1e-5
N_EXPERTS = 16
N_EXPERT_GROUPS = 4
EXPERTS_PER_GROUP = N_EXPERTS // N_EXPERT_GROUPS
TOP_K = 2
EXPERT_FF = D_MODEL // 2
ALPHA = (2 * DEPTH) ** 0.25
BETA = (8 * DEPTH) ** -0.25

kernel_name = "hymba_swa_sink_rwkv7_grouped_moe_step"

F32 = jnp.float32


def _split(t, sizes):
    out, off = [], 0
    for s in sizes:
        out.append(t[..., off:off + s])
        off += s
    return out


def layer_norm(x, g, b):
    xf = x.astype(F32)
    mu = xf.mean(-1, keepdims=True)
    var = jnp.mean(jnp.square(xf - mu), -1, keepdims=True)
    return ((xf - mu) * lax.rsqrt(var + LN_EPS) * g + b).astype(x.dtype)


def sink_softmax(s, mask, sink):
    s = jnp.where(mask, s, -jnp.inf)
    sink = sink.astype(F32)
    m = jnp.maximum(s.max(-1, keepdims=True), sink)
    e = jnp.exp(s - m)
    return e / (e.sum(-1, keepdims=True) + jnp.exp(sink - m))


def window_attention_banded(q, k, v, sinks):
    B, T, H, G, Dh = q.shape
    L = ATT_BLOCK
    nb = T // L
    qb = q.reshape(B, nb, L, H, G, Dh)
    kb = k.reshape(B, nb, L, H, Dh)
    vb = v.reshape(B, nb, L, H, Dh)
    pad = jnp.zeros_like(kb[:, :1])
    kw = jnp.concatenate([jnp.concatenate([pad, kb[:, :-1]], 1), kb], axis=2)
    vw = jnp.concatenate([jnp.concatenate([pad, vb[:, :-1]], 1), vb], axis=2)
    s = jnp.einsum("bnqhgd,bnkhd->bnhgqk", qb, kw).astype(F32) * ATT_SCALE
    qi = jnp.arange(L)[:, None]
    kj = jnp.arange(2 * L)[None, :]
    diff = qi + L - kj
    band = (diff >= 0) & (diff < WINDOW)
    exists = (jnp.arange(nb)[:, None, None] * L + kj[None] - L) >= 0
    mask = band[None] & exists
    p = sink_softmax(s, mask[None, :, None, None], sinks[:, :, None, None])
    o = jnp.einsum("bnhgqk,bnkhd->bnqhgd", p.astype(v.dtype), vw)
    return o.reshape(B, T, H * G * Dh)


def window_attention_cached(q, k, v, ck, cv, sinks):
    B, S, H, G, Dh = q.shape
    W = ck.shape[1]
    kall = jnp.concatenate([ck.astype(k.dtype), k], axis=1)
    vall = jnp.concatenate([cv.astype(v.dtype), v], axis=1)
    s = jnp.einsum("bqhgd,bkhd->bhgqk", q, kall).astype(F32) * ATT_SCALE
    kpos = jnp.concatenate([jnp.arange(W) - W, jnp.arange(S)])
    qpos = jnp.arange(S)
    diff = qpos[:, None] - kpos[None, :]
    mask = (diff >= 0) & (diff < WINDOW)
    p = sink_softmax(s, mask, sinks[:, :, None, None])
    o = jnp.einsum("bhgqk,bkhd->bqhgd", p.astype(v.dtype), vall)
    return o.reshape(B, S, H * G * Dh)


def wkv7_scan(state, r, w, k, v, a, b):
    def step(S, inp):
        r_t, w_t, k_t, v_t, a_t, b_t = inp
        sa = jnp.einsum("bhij,bhj->bhi", S, a_t)
        S = S * w_t[:, :, None, :] + sa[..., None] * b_t[:, :, None, :] + v_t[..., None] * k_t[:, :, None, :]
        return S, jnp.einsum("bhij,bhj->bhi", S, r_t)
    xs = tuple(jnp.moveaxis(t.astype(F32), 1, 0) for t in (r, w, k, v, a, b))
    S, ys = lax.scan(step, state.astype(F32), xs)
    return S, jnp.moveaxis(ys, 0, 1)


def rwkv7_time_mix(rw, state, v_first, wt, l):
    B, T, _ = rw.shape
    sizes = RWKV_SIZES + ((D_MV_LORA,) if l > 0 else ())
    parts = [p.astype(F32) for p in _split(rw, sizes)]
    r, wl, k, v, al, gl = parts[:6]
    decay = jnp.exp(-DECAY_SCALE * jax.nn.sigmoid(wt["decay_w0"][l] + jnp.tanh(wl) @ wt["decay_w2"][l]))
    a = jax.nn.sigmoid(wt["aaa_a0"][l] + al @ wt["aaa_a2"][l])
    g = jax.nn.sigmoid(gl) @ wt["gate_g2"][l]
    if l == 0:
        v_first = v
    else:
        v = v + (v_first - v) * jax.nn.sigmoid(wt["vres_v0"][l - 1] + parts[6] @ wt["vres_v2"][l - 1])
    heads = lambda t: t.reshape(B, T, N_RWKV_HEADS, HEAD_DIM)
    kk = heads(k * wt["k_k"][l])
    kk = kk / jnp.maximum(jnp.sqrt(jnp.sum(kk * kk, -1, keepdims=True)), 1e-12)
    k = k * (1.0 + (a - 1.0) * wt["k_a"][l])
    rh, wh, kh, vh, ah = heads(r), heads(decay), heads(k), heads(v), heads(a)
    new_state, y = wkv7_scan(state, rh, wh, kh, vh, -kk, kk * ah)
    mu = y.mean(-1, keepdims=True)
    var = jnp.mean(jnp.square(y - mu), -1, keepdims=True)
    y = ((y - mu) * lax.rsqrt(var + GN_EPS)).reshape(B, T, RWKV_WIDTH) * wt["gn_g"][l] + wt["gn_b"][l]
    bonus = jnp.sum(rh * kh * wt["r_k"][l], -1, keepdims=True) * vh
    out = (y + bonus.reshape(B, T, RWKV_WIDTH)) * g
    return out, new_state, v_first


def routed_moe(x, w_router, router_bias, w_gate, w_up, w_down):
    logits = jnp.einsum("btd,de->bte", x, w_router).astype(F32)
    probs = jax.nn.softmax(logits, -1)
    sel = (probs + router_bias.astype(F32)).reshape(*probs.shape[:-1], N_EXPERT_GROUPS, EXPERTS_PER_GROUP)
    group_score = lax.top_k(sel, TOP_K)[0].sum(-1)
    best = jnp.argmax(group_score, -1)
    in_group = jnp.sum(sel * jax.nn.one_hot(best, N_EXPERT_GROUPS, dtype=F32)[..., None], -2)
    _, local = lax.top_k(in_group, TOP_K)
    idx = best[..., None] * EXPERTS_PER_GROUP + local
    w_sel = jnp.take_along_axis(probs, idx, -1)
    w_sel = w_sel / w_sel.sum(-1, keepdims=True)
    gates = jnp.sum(jax.nn.one_hot(idx, N_EXPERTS, dtype=F32) * w_sel[..., None], -2)
    h = jax.nn.silu(jnp.einsum("btd,edf->btef", x, w_gate)) * jnp.einsum("btd,edf->btef", x, w_up)
    h = h * gates[..., None].astype(h.dtype)
    return jnp.einsum("btef,efd->btd", h, w_down).astype(x.dtype)


def _trunk(x, shift_prev, cache_k, cache_v, wkv_prev, wt):
    decode = cache_k is not None
    B, T, _ = x.shape
    new_k, new_v, new_wkv, new_shift = [], [], [], []
    v_first = None
    for l in range(DEPTH):
        if l == 0:
            w_cat, mu = wt["w_in"][0], wt["mu_rwkv"][0]
        else:
            w_cat = jnp.concatenate([wt["w_in"][l], wt["w_vres_in"][l - 1]], axis=1)
            mu = jnp.concatenate([wt["mu_rwkv"][l], wt["mu_vres"][l - 1]])
        x_ext = jnp.concatenate([shift_prev[l][:, None].astype(x.dtype), x], axis=1)
        proj = jnp.einsum("btd,dc->btc", x_ext, w_cat)
        att_in = proj[:, 1:, :N_ATT_COLS]
        rw_cur = proj[:, 1:, N_ATT_COLS:]
        rw_prev = proj[:, :-1, N_ATT_COLS:]
        rw = rw_cur + (rw_prev - rw_cur) * mu
        new_shift.append(x[:, -1])
        q, k, v = _split(att_in, (ATT_WIDTH, KV_WIDTH, KV_WIDTH))
        q = q.reshape(B, T, N_KV_HEADS, Q_PER_KV, HEAD_DIM)
        k = k.reshape(B, T, N_KV_HEADS, HEAD_DIM)
        v = v.reshape(B, T, N_KV_HEADS, HEAD_DIM)
        sinks = wt["sinks"][l].reshape(N_KV_HEADS, Q_PER_KV)
        if decode:
            att = window_attention_cached(q, k, v, cache_k[l], cache_v[l], sinks)
            new_k.append(k)
            new_v.append(v)
        else:
            att = window_attention_banded(q, k, v, sinks)
            new_k.append(k[:, -WINDOW:])
            new_v.append(v[:, -WINDOW:])
        rw_out, wkv_l, v_first = rwkv7_time_mix(rw, wkv_prev[l], v_first, wt, l)
        new_wkv.append(wkv_l)
        mixed = jnp.concatenate([att.astype(x.dtype), rw_out.astype(x.dtype)], -1) @ wt["w_o"][l]
        x = layer_norm(ALPHA * x + mixed, wt["ln1_g"][l], wt["ln1_b"][l])
        ffn = routed_moe(x, wt["w_router"], wt["router_bias"], wt["w_gate"][l], wt["w_up"][l], wt["w_down"][l])
        x = layer_norm(ALPHA * x + ffn, wt["ln2_g"][l], wt["ln2_b"][l])
    return (x, jnp.stack(new_k), jnp.stack(new_v), jnp.stack(new_wkv), jnp.stack(new_shift))


def setup_inputs(seed: int = 0) -> dict:
    key = jax.random.key(seed)
    ks = iter(jax.random.split(key, 48))
    nrm = lambda shape, scale: jax.random.normal(next(ks), shape, F32) * scale
    unif = lambda shape, lo, hi: jax.random.uniform(next(ks), shape, F32, lo, hi)
    win_buf = min(WINDOW, PAST_LEN)
    D = D_MODEL
    return {
        "x_prompt": nrm((BATCH, SEQ, D), 1.0),
        "x_sample": nrm((DEC_BATCH, DEC_SEQ, D), 1.0),
        "cache_k": nrm((DEPTH, DEC_BATCH, win_buf, N_KV_HEADS, HEAD_DIM), 1.0),
        "cache_v": nrm((DEPTH, DEC_BATCH, win_buf, N_KV_HEADS, HEAD_DIM), 1.0),
        "state_wkv": nrm((DEPTH, DEC_BATCH, N_RWKV_HEADS, HEAD_DIM, HEAD_DIM), 0.3),
        "state_shift": nrm((DEPTH, DEC_BATCH, D), 1.0),
        "w_in": nrm((DEPTH, D, N_IN_COLS), D ** -0.5),
        "w_vres_in": nrm((DEPTH - 1, D, D_MV_LORA), D ** -0.5),
        "mu_rwkv": unif((DEPTH, N_RWKV_COLS), 0.1, 0.9),
        "mu_vres": unif((DEPTH - 1, D_MV_LORA), 0.1, 0.9),
        "sinks": nrm((DEPTH, N_Q_HEADS), 0.5),
        "decay_w0": unif((DEPTH, RWKV_WIDTH), -4.0, 2.0),
        "decay_w2": nrm((DEPTH, D_DECAY_LORA, RWKV_WIDTH), D_DECAY_LORA ** -0.5),
        "aaa_a0": nrm((DEPTH, RWKV_WIDTH), 0.5),
        "aaa_a2": nrm((DEPTH, D_AAA_LORA, RWKV_WIDTH), D_AAA_LORA ** -0.5),
        "vres_v0": nrm((DEPTH - 1, RWKV_WIDTH), 0.5),
        "vres_v2": nrm((DEPTH - 1, D_MV_LORA, RWKV_WIDTH), D_MV_LORA ** -0.5),
        "gate_g2": nrm((DEPTH, D_GATE_LORA, RWKV_WIDTH), D_GATE_LORA ** -0.5),
        "k_k": 0.85 + nrm((DEPTH, RWKV_WIDTH), 0.05),
        "k_a": 1.0 + nrm((DEPTH, RWKV_WIDTH), 0.05),
        "r_k": nrm((DEPTH, N_RWKV_HEADS, HEAD_DIM), 0.1),
        "gn_g": 1.0 + nrm((DEPTH, RWKV_WIDTH), 0.05),
        "gn_b": nrm((DEPTH, RWKV_WIDTH), 0.02),
        "w_o": nrm((DEPTH, D, D), BETA * D ** -0.5),
        "ln1_g": 1.0 + nrm((DEPTH, D), 0.05),
        "ln1_b": nrm((DEPTH, D), 0.02),
        "w_router": nrm((D, N_EXPERTS), D ** -0.5),
        "router_bias": nrm((N_EXPERTS,), 0.01),
        "w_gate": nrm((DEPTH, N_EXPERTS, D, EXPERT_FF), D ** -0.5),
        "w_up": nrm((DEPTH, N_EXPERTS, D, EXPERT_FF), D ** -0.5),
        "w_down": nrm((DEPTH, N_EXPERTS, EXPERT_FF, D), BETA * EXPERT_FF ** -0.5),
        "ln2_g": 1.0 + nrm((DEPTH, D), 0.05),
        "ln2_b": nrm((DEPTH, D), 0.02),
    }


def reference(x_prompt, x_sample, cache_k, cache_v, state_wkv, state_shift,
              w_in, w_vres_in, mu_rwkv, mu_vres, sinks, decay_w0, decay_w2,
              aaa_a0, aaa_a2, vres_v0, vres_v2, gate_g2, k_k, k_a, r_k, gn_g, gn_b,
              w_o, ln1_g, ln1_b, w_router, router_bias, w_gate, w_up, w_down, ln2_g, ln2_b):
    wt = dict(w_in=w_in, w_vres_in=w_vres_in, mu_rwkv=mu_rwkv, mu_vres=mu_vres, sinks=sinks,
              decay_w0=decay_w0, decay_w2=decay_w2, aaa_a0=aaa_a0, aaa_a2=aaa_a2,
              vres_v0=vres_v0, vres_v2=vres_v2, gate_g2=gate_g2, k_k=k_k, k_a=k_a, r_k=r_k,
              gn_g=gn_g, gn_b=gn_b, w_o=w_o, ln1_g=ln1_g, ln1_b=ln1_b, w_router=w_router,
              router_bias=router_bias, w_gate=w_gate, w_up=w_up, w_down=w_down,
              ln2_g=ln2_g, ln2_b=ln2_b)
    b_p = x_prompt.shape[0]
    zero_shift = jnp.zeros((DEPTH, b_p, D_MODEL), x_prompt.dtype)
    zero_wkv = jnp.zeros((DEPTH, b_p, N_RWKV_HEADS, HEAD_DIM, HEAD_DIM), F32)
    y_prompt, k_prompt, v_prompt, wkv_prompt, shift_prompt = _trunk(
        x_prompt, zero_shift, None, None, zero_wkv, wt)
    y_sample, k_sample, v_sample, wkv_sample, shift_sample = _trunk(
        x_sample, state_shift, cache_k, cache_v, state_wkv, wt)
    return (y_prompt, y_sample, k_prompt, v_prompt, wkv_prompt, shift_prompt,
            k_sample, v_sample, wkv_sample, shift_sample)
```

```python
import functools
import math

import jax
import jax.numpy as jnp
from jax import lax
from jax.experimental import pallas as pl
from jax.experimental.pallas import tpu as pltpu

F32 = jnp.float32
BF16 = jnp.bfloat16

D_MODEL = 1024
DEPTH = 4
HEAD_DIM = 64
ATT_WIDTH = 512
RWKV_WIDTH = 512
N_Q_HEADS = 8
N_KV_HEADS = 2
Q_PER_KV = 4
KV_WIDTH = 128
N_ATT_COLS = ATT_WIDTH + 2 * KV_WIDTH
WINDOW = 128
ATT_SCALE = HEAD_DIM ** -0.5
N_RWKV_HEADS = 8
D_DECAY_LORA = 64
D_AAA_LORA = 64
D_GATE_LORA = 128
D_MV_LORA = 32
DECAY_SCALE = math.exp(-0.5)
GN_EPS = 64e-5
LN_EPS = 1e-5
N_EXPERTS = 16
N_EXPERT_GROUPS = 4
EXPERTS_PER_GROUP = 4
EXPERT_FF = 512
ALPHA = (2 * DEPTH) ** 0.25

LANES = 128
VMEM_LIMIT_BYTES = 56 * 1024 * 1024
ROW_TILE = 256
MOE_ROW_TILE = 1024
WKV_CHUNK = 64
SAMPLE_ATT_BATCH = 8


def _dg(a, b, ca, cb):
    return lax.dot_general(a, b, (((ca,), (cb,)), ((), ())), preferred_element_type=F32)


def _bdot(a, b):
    return _dg(a.astype(BF16), b.astype(BF16), 1, 0)


def _split2(x):
    hi = x.astype(BF16)
    lo = (x - hi.astype(F32)).astype(BF16)
    return hi, lo


def _dot3(a, b, ca=1, cb=0):
    ah, al = _split2(a)
    bh, bl = _split2(b)
    return _dg(ah, bh, ca, cb) + (_dg(ah, bl, ca, cb) + _dg(al, bh, ca, cb))


def _dot_exact_lhs(m_bf16, x, parts=3):
    acc = None
    rem = x
    for _ in range(parts):
        p = rem.astype(BF16)
        t = _dg(m_bf16, p, 1, 0)
        acc = t if acc is None else acc + t
        rem = rem - p.astype(F32)
    return acc


def _dot_exact_rhs(x, m_bf16, parts=3):
    acc = None
    rem = x
    for _ in range(parts):
        p = rem.astype(BF16)
        t = _dg(p, m_bf16, 1, 0)
        acc = t if acc is None else acc + t
        rem = rem - p.astype(F32)
    return acc


def _layer_norm(z, g, b):
    mu = jnp.mean(z, axis=-1, keepdims=True)
    var = jnp.mean(jnp.square(z - mu), axis=-1, keepdims=True)
    return (z - mu) * lax.rsqrt(var + LN_EPS) * g + b


def _params(*sem):
    return pltpu.CompilerParams(dimension_semantics=sem, vmem_limit_bytes=VMEM_LIMIT_BYTES)


def _const_spec(shape):
    nd = len(shape)
    return pl.BlockSpec(shape, lambda *_: (0,) * nd)


def _inproj_kernel(*refs, first):
    if first:
        (x_ref, xp_ref, watt_ref, wrw_ref, mu_ref, wda_ref, g2_ref, vec_ref, bd_ref,
         q_ref, ka_ref, va_ref, r_ref, lw_ref, k_ref, v_ref, a_ref, b_ref, g_ref, bonus_ref) = refs
    else:
        (x_ref, xp_ref, watt_ref, wrw_ref, mu_ref, wda_ref, g2_ref, vec_ref, bd_ref, vfirst_ref, v2_ref,
         q_ref, ka_ref, va_ref, r_ref, lw_ref, k_ref, v_ref, a_ref, b_ref, g_ref, bonus_ref) = refs
    xb = x_ref[...].astype(BF16)
    xpb = xp_ref[...].astype(BF16)
    qkv = _dg(xb, watt_ref[...], 1, 0)
    q_ref[...] = qkv[:, :ATT_WIDTH]
    ka_ref[...] = qkv[:, ATT_WIDTH:ATT_WIDTH + KV_WIDTH]
    va_ref[...] = qkv[:, ATT_WIDTH + KV_WIDTH:]

    pc = _dg(xb, wrw_ref[...], 1, 0)
    pp = _dg(xpb, wrw_ref[...], 1, 0)
    rw = pc + (pp - pc) * mu_ref[...]
    W = RWKV_WIDTH
    r = rw[:, 0:W]
    k = rw[:, W:2 * W]
    v = rw[:, 2 * W:3 * W]
    gl = rw[:, 3 * W:3 * W + D_GATE_LORA]
    wa = rw[:, 3 * W + D_GATE_LORA:3 * W + 2 * LANES]
    w0 = vec_ref[0:1, :]
    a0 = vec_ref[1:2, :]
    k_k = vec_ref[2:3, :]
    k_a = vec_ref[3:4, :]
    r_k = vec_ref[4:5, :]

    lane = lax.broadcasted_iota(jnp.int32, wa.shape, 1)
    wa_t = jnp.where(lane < D_DECAY_LORA, jnp.tanh(wa), wa)
    da = _bdot(wa_t, wda_ref[...])
    lw = -DECAY_SCALE * jax.nn.sigmoid(w0 + da[:, :W])
    a = jax.nn.sigmoid(a0 + da[:, W:])
    g = _bdot(jax.nn.sigmoid(gl), g2_ref[...])
    if not first:
        mv = rw[:, 3 * W + 2 * LANES:]
        v0 = vec_ref[5:6, :]
        v = v + (vfirst_ref[...] - v) * jax.nn.sigmoid(v0 + _bdot(mv, v2_ref[...]))
    bd = bd_ref[...]
    kk = k * k_k
    ssq = _dot_exact_rhs(kk * kk, bd)
    kk = kk / jnp.maximum(jnp.sqrt(ssq), 1e-12)
    k = k * (1.0 + (a - 1.0) * k_a)
    bonus = _dot_exact_rhs(r * k * r_k, bd) * v

    r_ref[...] = r
    lw_ref[...] = lw
    k_ref[...] = k
    v_ref[...] = v
    a_ref[...] = -kk
    b_ref[...] = kk * a
    g_ref[...] = g
    bonus_ref[...] = bonus


def _inproj(x, xprev, wts, v_first):
    n = x.shape[0]
    first = v_first is None
    tm = ROW_TILE
    row = lambda w: pl.BlockSpec((tm, w), lambda i: (i, 0))
    ins = [x, xprev, wts["watt"], wts["wrw"], wts["mu"], wts["wda"], wts["g2"], wts["vecA"], wts["bd"]]
    in_specs = [row(D_MODEL), row(D_MODEL)] + [_const_spec(a.shape) for a in ins[2:]]
    if not first:
        ins += [v_first, wts["v2"]]
        in_specs += [row(RWKV_WIDTH), _const_spec(wts["v2"].shape)]
    widths = [ATT_WIDTH, KV_WIDTH, KV_WIDTH] + [RWKV_WIDTH] * 8
    return pl.pallas_call(
        functools.partial(_inproj_kernel, first=first),
        grid=(n // tm,),
        in_specs=in_specs,
        out_specs=[row(w) for w in widths],
        out_shape=[jax.ShapeDtypeStruct((n, w), F32) for w in widths],
        compiler_params=_params("arbitrary"),
        name="inproj",
    )(*ins)


def _wkv_kernel(*refs, chunk, has_state):
    if has_state:
        r_ref, lw_ref, k_ref, v_ref, a_ref, b_ref, s0_ref, y_ref, sout_ref, s_scr = refs
    else:
        r_ref, lw_ref, k_ref, v_ref, a_ref, b_ref, y_ref, sout_ref, s_scr = refs
    L = chunk
    c_idx = pl.program_id(1)

    @pl.when(c_idx == 0)
    def _init():
        if has_state:
            s_scr[...] = s0_ref[0]
        else:
            s_scr[...] = jnp.zeros_like(s_scr)

    row = lax.broadcasted_iota(jnp.int32, (L, L), 0)
    col = lax.broadcasted_iota(jnp.int32, (L, L), 1)
    incl = row >= col
    strict = row > col
    tri = incl.astype(BF16)
    eye = (row == col).astype(F32)

    lw_all = lw_ref[...]
    cum_all = _dot_exact_lhs(tri, lw_all)
    tot_all = cum_all[L - 1:L, :]
    e_neg = jnp.exp(-cum_all)
    e_end = jnp.exp(tot_all - cum_all)
    b_all = b_ref[...]
    k_all = k_ref[...]
    at_all = a_ref[...] * jnp.exp(cum_all - lw_all)
    rt_all = r_ref[...] * jnp.exp(cum_all)
    bt_all = b_all * e_neg
    kt_all = k_all * e_neg
    bh_all = b_all * e_end
    kh_all = k_all * e_end
    etot_all = jnp.exp(tot_all)
    v_all = v_ref[...]

    n_double = int(round(math.log2(L))) - 1
    ys = []
    for h in range(N_RWKV_HEADS):
        sl = slice(h * HEAD_DIM, (h + 1) * HEAD_DIM)
        at, rt, bt, kt = at_all[:, sl], rt_all[:, sl], bt_all[:, sl], kt_all[:, sl]
        bh, kh, v = bh_all[:, sl], kh_all[:, sl], v_all[:, sl]
        s0 = s_scr[h]
        a_ab = jnp.where(strict, _dot3(at, bt, 1, 1), 0.0)
        a_ak = jnp.where(strict, _dot3(at, kt, 1, 1), 0.0)
        a_rb = jnp.where(incl, _dot3(rt, bt, 1, 1), 0.0)
        a_rk = jnp.where(incl, _dot3(rt, kt, 1, 1), 0.0)
        x = a_ab
        t_inv = eye + a_ab
        for _ in range(n_double):
            x = _dot3(x, x)
            t_inv = t_inv + _dot3(t_inv, x)
        rhs = _dot3(at, s0, 1, 1) + _dot3(a_ak, v)
        u = _dot3(t_inv, rhs)
        y = _dot3(rt, s0, 1, 1) + _dot3(a_rb, u) + _dot3(a_rk, v)
        s_new = s0 * etot_all[:, sl] + _dot3(u, bh, 0, 0) + _dot3(v, kh, 0, 0)
        s_scr[h] = s_new
        mu = jnp.mean(y, axis=-1, keepdims=True)
        var = jnp.mean(jnp.square(y - mu), axis=-1, keepdims=True)
        ys.append((y - mu) * lax.rsqrt(var + GN_EPS))
    y_ref[...] = jnp.concatenate(ys, axis=1)

    @pl.when(c_idx == pl.num_programs(1) - 1)
    def _fin():
        sout_ref[0] = s_scr[...]


def _wkv_scan(r, lw, k, v, a, b, state, n_batch, seq, chunk):
    n_chunks = seq // chunk
    has_state = state is not None
    tok_spec = pl.BlockSpec((chunk, RWKV_WIDTH), lambda i, c: (i * n_chunks + c, 0))
    st_spec = pl.BlockSpec((1, N_RWKV_HEADS, HEAD_DIM, HEAD_DIM), lambda i, c: (i, 0, 0, 0))
    in_specs = [tok_spec] * 6 + ([st_spec] if has_state else [])
    args = (r, lw, k, v, a, b) + ((state,) if has_state else ())
    return pl.pallas_call(
        functools.partial(_wkv_kernel, chunk=chunk, has_state=has_state),
        grid=(n_batch, n_chunks),
        in_specs=in_specs,
        out_specs=[tok_spec, st_spec],
        out_shape=[jax.ShapeDtypeStruct((n_batch * seq, RWKV_WIDTH), F32),
                   jax.ShapeDtypeStruct((n_batch, N_RWKV_HEADS, HEAD_DIM, HEAD_DIM), F32)],
        scratch_shapes=[pltpu.VMEM((N_RWKV_HEADS, HEAD_DIM, HEAD_DIM), F32)],
        compiler_params=_params("arbitrary", "arbitrary"),
        name="wkv_scan",
    )(*args)


def _sink_softmax(s, sink):
    m = sink
    for t in s:
        m = jnp.maximum(m, jnp.max(t, axis=-1, keepdims=True))
    es = [jnp.exp(t - m) for t in s]
    den = jnp.exp(sink - m)
    for e in es:
        den = den + jnp.sum(e, axis=-1, keepdims=True)
    return [e / den for e in es]


def _attn_prompt_kernel(q_ref, kc_ref, kp_ref, vc_ref, vp_ref, sink_ref, o_ref):
    L = WINDOW
    n = pl.program_id(1)
    q = q_ref[...]
    kw = jnp.concatenate([kp_ref[...], kc_ref[...]], axis=0).astype(BF16)
    vw = jnp.concatenate([vp_ref[...], vc_ref[...]], axis=0).astype(BF16)
    qi = lax.broadcasted_iota(jnp.int32, (L, 2 * L), 0)
    kj = lax.broadcasted_iota(jnp.int32, (L, 2 * L), 1)
    diff = qi + L - kj
    mask = (diff >= 0) & (diff < WINDOW) & ((kj >= L) | (n > 0))
    outs = []
    for h in range(N_Q_HEADS):
        hk = h // Q_PER_KV
        ksl = slice(hk * HEAD_DIM, (hk + 1) * HEAD_DIM)
        qh = q[:, h * HEAD_DIM:(h + 1) * HEAD_DIM].astype(BF16)
        s = _dg(qh, kw[:, ksl], 1, 1) * ATT_SCALE
        s = jnp.where(mask, s, -jnp.inf)
        (p,) = _sink_softmax([s], sink_ref[h:h + 1, 0:1])
        outs.append(_dg(p.astype(BF16), vw[:, ksl], 1, 0))
    o_ref[...] = jnp.concatenate(outs, axis=1)


def _attn_prompt(q, k, v, sink_rows, n_batch, seq):
    nb = seq // WINDOW
    cur = lambda w: pl.BlockSpec((WINDOW, w), lambda b, n: (b * nb + n, 0))
    prev = lambda w: pl.BlockSpec((WINDOW, w), lambda b, n: (b * nb + jnp.maximum(n - 1, 0), 0))
    return pl.pallas_call(
        _attn_prompt_kernel,
        grid=(n_batch, nb),
        in_specs=[cur(ATT_WIDTH), cur(KV_WIDTH), prev(KV_WIDTH), cur(KV_WIDTH), prev(KV_WIDTH),
                  _const_spec(sink_rows.shape)],
        out_specs=cur(ATT_WIDTH),
        out_shape=jax.ShapeDtypeStruct((n_batch * seq, ATT_WIDTH), F32),
        compiler_params=_params("arbitrary", "arbitrary"),
        name="attn_prompt",
    )(q, k, k, v, v, sink_rows)


def _attn_sample_kernel(q_ref, kn_ref, vn_ref, ck_ref, cv_ref, sink_ref, o_ref, *, seq):
    S = seq
    W = WINDOW
    qi1 = lax.broadcasted_iota(jnp.int32, (S, W), 0)
    kj1 = lax.broadcasted_iota(jnp.int32, (S, W), 1)
    mask_cache = kj1 > qi1
    qi2 = lax.broadcasted_iota(jnp.int32, (S, S), 0)
    kj2 = lax.broadcasted_iota(jnp.int32, (S, S), 1)
    mask_new = kj2 <= qi2

    def body(b, carry):
        rows = pl.ds(pl.multiple_of(b * S, S), S)
        q = q_ref[rows, :]
        kn = kn_ref[rows, :].astype(BF16)
        vn = vn_ref[rows, :].astype(BF16)
        ck = ck_ref[b].astype(BF16)
        cv = cv_ref[b].astype(BF16)
        outs = []
        for h in range(N_Q_HEADS):
            hk = h // Q_PER_KV
            ksl = slice(hk * HEAD_DIM, (hk + 1) * HEAD_DIM)
            qh = q[:, h * HEAD_DIM:(h + 1) * HEAD_DIM].astype(BF16)
            s1 = jnp.where(mask_cache, _dg(qh, ck[:, ksl], 1, 1) * ATT_SCALE, -jnp.inf)
            s2 = jnp.where(mask_new, _dg(qh, kn[:, ksl], 1, 1) * ATT_SCALE, -jnp.inf)
            p1, p2 = _sink_softmax([s1, s2], sink_ref[h:h + 1, 0:1])
            outs.append(_dg(p1.astype(BF16), cv[:, ksl], 1, 0) + _dg(p2.astype(BF16), vn[:, ksl], 1, 0))
        o_ref[rows, :] = jnp.concatenate(outs, axis=1)
        return carry

    lax.fori_loop(0, SAMPLE_ATT_BATCH, body, 0)


def _attn_sample(q, k, v, cache_k, cache_v, sink_rows, n_batch, seq):
    bb = SAMPLE_ATT_BATCH
    tok = lambda w: pl.BlockSpec((bb * seq, w), lambda i: (i, 0))
    cache = pl.BlockSpec((bb, WINDOW, KV_WIDTH), lambda i: (i, 0, 0))
    return pl.pallas_call(
        functools.partial(_attn_sample_kernel, seq=seq),
        grid=(n_batch // bb,),
        in_specs=[tok(ATT_WIDTH), tok(KV_WIDTH), tok(KV_WIDTH), cache, cache, _const_spec(sink_rows.shape)],
        out_specs=tok(ATT_WIDTH),
        out_shape=jax.ShapeDtypeStruct((n_batch * seq, ATT_WIDTH), F32),
        compiler_params=_params("arbitrary"),
        name="attn_sample",
    )(q, k, v, cache_k, cache_v, sink_rows)


def _second_max4(a, b, c, d):
    return jnp.maximum(jnp.maximum(jnp.minimum(a, b), jnp.minimum(c, d)),
                       jnp.minimum(jnp.maximum(a, b), jnp.maximum(c, d)))


def _route(logits_t, bias_col):
    G, E = N_EXPERT_GROUPS, EXPERTS_PER_GROUP
    m = jnp.max(logits_t, axis=0, keepdims=True)
    ex = jnp.exp(logits_t - m)
    probs = ex / jnp.sum(ex, axis=0, keepdims=True)
    sel = probs + bias_col
    p = [probs[e:e + 1, :] for e in range(N_EXPERTS)]
    s = [sel[e:e + 1, :] for e in range(N_EXPERTS)]
    gs = []
    for g in range(G):
        a, b, c, d = s[E * g:E * g + E]
        top1 = jnp.maximum(jnp.maximum(a, b), jnp.maximum(c, d))
        gs.append(top1 + _second_max4(a, b, c, d))
    best = jnp.zeros_like(gs[0], dtype=jnp.int32)
    best_s = gs[0]
    for g in range(1, G):
        upd = gs[g] > best_s
        best = jnp.where(upd, g, best)
        best_s = jnp.where(upd, gs[g], best_s)

    def pick(vals, j):
        out = vals[j]
        for g in range(1, G):
            out = jnp.where(best == g, vals[E * g + j], out)
        return out

    ig = [pick(s, j) for j in range(E)]
    pg = [pick(p, j) for j in range(E)]
    l1 = jnp.zeros_like(best)
    v1 = ig[0]
    for j in range(1, E):
        upd = ig[j] > v1
        l1 = jnp.where(upd, j, l1)
        v1 = jnp.where(upd, ig[j], v1)
    l2 = jnp.full_like(best, -1)
    v2 = jnp.full_like(v1, -jnp.inf)
    for j in range(E):
        upd = (l1 != j) & (ig[j] > v2)
        l2 = jnp.where(upd, j, l2)
        v2 = jnp.where(upd, ig[j], v2)
    zero = jnp.zeros_like(v1)
    w1 = zero
    w2 = zero
    for j in range(E):
        w1 = jnp.where(l1 == j, pg[j], w1)
        w2 = jnp.where(l2 == j, pg[j], w2)
    wsum = w1 + w2
    w1 = w1 / wsum
    w2 = w2 / wsum
    rows = []
    for e in range(N_EXPERTS):
        g, j = divmod(e, E)
        in_g = best == g
        rows.append(jnp.where(in_g & (l1 == j), w1, zero) + jnp.where(in_g & (l2 == j), w2, zero))
    return jnp.concatenate(rows, axis=0)


def _post_kernel(yn_ref, bonus_ref, g_ref, att_ref, x_ref, woa_ref, wor_ref, gn_ref, ln_ref,
                 wrt_ref, rb_ref, x1_ref, gates_ref):
    rw_out = (yn_ref[...] * gn_ref[0:1, :] + gn_ref[1:2, :] + bonus_ref[...]) * g_ref[...]
    mixed = _bdot(att_ref[...], woa_ref[...]) + _bdot(rw_out, wor_ref[...])
    x1 = _layer_norm(ALPHA * x_ref[...] + mixed, ln_ref[0:1, :], ln_ref[1:2, :])
    x1_ref[...] = x1
    logits_t = _dot3(wrt_ref[...], x1, 1, 1)
    gates_t = _route(logits_t, rb_ref[:, 0:1])
    pad = jnp.zeros((LANES - N_EXPERTS, gates_t.shape[1]), F32)
    gates_ref[...] = jnp.concatenate([gates_t, pad], axis=0).T


def _post(yn, bonus, g, att, x, wts, glob):
    n = x.shape[0]
    tm = ROW_TILE
    row = lambda w: pl.BlockSpec((tm, w), lambda i: (i, 0))
    consts = [wts["woa"], wts["wor"], wts["gn"], wts["ln1"], glob["wrt"], glob["rb"]]
    return pl.pallas_call(
        _post_kernel,
        grid=(n // tm,),
        in_specs=[row(RWKV_WIDTH)] * 3 + [row(ATT_WIDTH), row(D_MODEL)] + [_const_spec(a.shape) for a in consts],
        out_specs=[row(D_MODEL), row(LANES)],
        out_shape=[jax.ShapeDtypeStruct((n, D_MODEL), F32), jax.ShapeDtypeStruct((n, LANES), F32)],
        compiler_params=_params("arbitrary"),
        name="post_mix",
    )(yn, bonus, g, att, x, *consts)


def _moe_kernel(x_ref, gates_ref, wg_ref, wu_ref, wd_ref, ln_ref, o_ref, acc_ref, xb_ref):
    e = pl.program_id(1)

    @pl.when(e == 0)
    def _init():
        acc_ref[...] = jnp.zeros_like(acc_ref)
        xb_ref[...] = x_ref[...].astype(BF16)

    xb = xb_ref[...]
    gates = gates_ref[...]
    lane = lax.broadcasted_iota(jnp.int32, gates.shape, 1)
    gcol = jnp.sum(jnp.where(lane == e, gates, 0.0), axis=1, keepdims=True)
    hg = _dg(xb, wg_ref[0], 1, 0)
    hu = _dg(xb, wu_ref[0], 1, 0)
    h = jax.nn.silu(hg) * hu * gcol
    acc_ref[...] += _dg(h.astype(BF16), wd_ref[0], 1, 0)

    @pl.when(e == pl.num_programs(1) - 1)
    def _fin():
        o_ref[...] = _layer_norm(ALPHA * x_ref[...] + acc_ref[...], ln_ref[0:1, :], ln_ref[1:2, :])


def _moe(x1, gates, wts):
    n = x1.shape[0]
    tm = MOE_ROW_TILE
    return pl.pallas_call(
        _moe_kernel,
        grid=(n // tm, N_EXPERTS),
        in_specs=[pl.BlockSpec((tm, D_MODEL), lambda i, e: (i, 0)),
                  pl.BlockSpec((tm, LANES), lambda i, e: (i, 0)),
                  pl.BlockSpec((1, D_MODEL, EXPERT_FF), lambda i, e: (e, 0, 0)),
                  pl.BlockSpec((1, D_MODEL, EXPERT_FF), lambda i, e: (e, 0, 0)),
                  pl.BlockSpec((1, EXPERT_FF, D_MODEL), lambda i, e: (e, 0, 0)),
                  _const_spec(wts["ln2"].shape)],
        out_specs=pl.BlockSpec((tm, D_MODEL), lambda i, e: (i, 0)),
        out_shape=jax.ShapeDtypeStruct((n, D_MODEL), F32),
        scratch_shapes=[pltpu.VMEM((tm, D_MODEL), F32), pltpu.VMEM((tm, D_MODEL), BF16)],
        compiler_params=_params("arbitrary", "arbitrary"),
        name="moe",
    )(x1, gates, wts["wg"], wts["wu"], wts["wd"], wts["ln2"])


def _rows8(vectors, width):
    rows = [v.reshape(1, width).astype(F32) for v in vectors]
    rows.append(jnp.zeros((8 - len(rows), width), F32))
    return jnp.concatenate(rows, axis=0)


def _prep_layer(l, p):
    W = RWKV_WIDTH
    w_l = p["w_in"][l]
    rwc = w_l[:, N_ATT_COLS:]
    mu = p["mu_rwkv"][l]
    o_w, o_k, o_v, o_a, o_g = W, W + D_DECAY_LORA, 2 * W + D_DECAY_LORA, 3 * W + D_DECAY_LORA, 3 * W + 2 * D_DECAY_LORA

    def reorder(t):
        parts = [t[..., 0:W], t[..., o_k:o_k + W], t[..., o_v:o_v + W], t[..., o_g:o_g + D_GATE_LORA],
                 t[..., o_w:o_w + D_DECAY_LORA], t[..., o_a:o_a + D_AAA_LORA]]
        return parts

    w_parts = reorder(rwc)
    mu_parts = reorder(mu)
    if l > 0:
        padw = LANES - D_MV_LORA
        w_parts += [p["w_vres_in"][l - 1], jnp.zeros((D_MODEL, padw), F32)]
        mu_parts += [p["mu_vres"][l - 1], jnp.zeros((padw,), F32)]
    wrw = jnp.concatenate(w_parts, axis=1).astype(BF16)
    mu_row = jnp.concatenate(mu_parts).reshape(1, -1)
    zero = jnp.zeros((D_DECAY_LORA, W), F32)
    wda = jnp.concatenate([jnp.concatenate([p["decay_w2"][l], zero], axis=1),
                           jnp.concatenate([zero, p["aaa_a2"][l]], axis=1)], axis=0).astype(BF16)
    vecs = [p["decay_w0"][l], p["aaa_a0"][l], p["k_k"][l], p["k_a"][l], p["r_k"][l].reshape(W)]
    out = {
        "watt": w_l[:, :N_ATT_COLS].astype(BF16),
        "wrw": wrw,
        "mu": mu_row,
        "wda": wda,
        "g2": p["gate_g2"][l].astype(BF16),
        "woa": p["w_o"][l][:ATT_WIDTH].astype(BF16),
        "wor": p["w_o"][l][ATT_WIDTH:].astype(BF16),
        "gn": _rows8([p["gn_g"][l], p["gn_b"][l]], W),
        "ln1": _rows8([p["ln1_g"][l], p["ln1_b"][l]], D_MODEL),
        "ln2": _rows8([p["ln2_g"][l], p["ln2_b"][l]], D_MODEL),
        "wg": p["w_gate"][l].astype(BF16),
        "wu": p["w_up"][l].astype(BF16),
        "wd": p["w_down"][l].astype(BF16),
        "sink_rows": jnp.broadcast_to(p["sinks"][l].reshape(N_Q_HEADS, 1), (N_Q_HEADS, LANES)).astype(F32),
    }
    if l > 0:
        vecs.append(p["vres_v0"][l - 1])
        out["v2"] = jnp.concatenate([p["vres_v2"][l - 1], jnp.zeros((LANES - D_MV_LORA, W), F32)], axis=0).astype(BF16)
    out["vecA"] = _rows8(vecs, W)
    hid = jnp.arange(W) // HEAD_DIM
    out["bd"] = (hid[:, None] == hid[None, :]).astype(BF16)
    return out


def _trunk(x3, shift_prev, cache_k, cache_v, wkv_prev, layer_wts, glob):
    decode = cache_k is not None
    n_batch, seq, _ = x3.shape
    x = x3.reshape(n_batch * seq, D_MODEL)
    new_k, new_v, new_wkv, new_shift = [], [], [], []
    v_first = None
    for l in range(DEPTH):
        wts = layer_wts[l]
        x_b = x.reshape(n_batch, seq, D_MODEL)
        xprev = jnp.concatenate([shift_prev[l][:, None, :], x_b[:, :-1]], axis=1).reshape(n_batch * seq, D_MODEL)
        new_shift.append(x_b[:, -1])
        q, ka, va, r, lw, k, v, a, b, g, bonus = _inproj(x, xprev, wts, v_first)
        if l == 0:
            v_first = v
        if decode:
            ck = cache_k[l].reshape(n_batch, WINDOW, KV_WIDTH)
            cv = cache_v[l].reshape(n_batch, WINDOW, KV_WIDTH)
            att = _attn_sample(q, ka, va, ck, cv, wts["sink_rows"], n_batch, seq)
            new_k.append(ka.reshape(n_batch, seq, N_KV_HEADS, HEAD_DIM))
            new_v.append(va.reshape(n_batch, seq, N_KV_HEADS, HEAD_DIM))
            yn, s_out = _wkv_scan(r, lw, k, v, a, b, wkv_prev[l], n_batch, seq, seq)
        else:
            att = _attn_prompt(q, ka, va, wts["sink_rows"], n_batch, seq)
            new_k.append(ka.reshape(n_batch, seq, N_KV_HEADS, HEAD_DIM)[:, -WINDOW:])
            new_v.append(va.reshape(n_batch, seq, N_KV_HEADS, HEAD_DIM)[:, -WINDOW:])
            yn, s_out = _wkv_scan(r, lw, k, v, a, b, None, n_batch, seq, WKV_CHUNK)
        new_wkv.append(s_out)
        x1, gates = _post(yn, bonus, g, att, x, wts, glob)
        x = _moe(x1, gates, wts)
    return (x.reshape(n_batch, seq, D_MODEL), jnp.stack(new_k), jnp.stack(new_v), jnp.stack(new_wkv),
            jnp.stack(new_shift))


def kernel(x_prompt, x_sample, cache_k, cache_v, state_wkv, state_shift, w_in, w_vres_in, mu_rwkv, mu_vres, sinks, decay_w0, decay_w2, aaa_a0, aaa_a2, vres_v0, vres_v2, gate_g2, k_k, k_a, r_k, gn_g, gn_b, w_o, ln1_g, ln1_b, w_router, router_bias, w_gate, w_up, w_down, ln2_g, ln2_b):
    p = dict(w_in=w_in, w_vres_in=w_vres_in, mu_rwkv=mu_rwkv, mu_vres=mu_vres, sinks=sinks,
             decay_w0=decay_w0, decay_w2=decay_w2, aaa_a0=aaa_a0, aaa_a2=aaa_a2,
             vres_v0=vres_v0, vres_v2=vres_v2, gate_g2=gate_g2, k_k=k_k, k_a=k_a, r_k=r_k,
             gn_g=gn_g, gn_b=gn_b, w_o=w_o, ln1_g=ln1_g, ln1_b=ln1_b,
             w_gate=w_gate, w_up=w_up, w_down=w_down, ln2_g=ln2_g, ln2_b=ln2_b)
    layer_wts = [_prep_layer(l, p) for l in range(DEPTH)]
    glob = {
        "wrt": w_router.T.astype(F32),
        "rb": jnp.broadcast_to(router_bias.reshape(N_EXPERTS, 1), (N_EXPERTS, LANES)).astype(F32),
    }
    b_p = x_prompt.shape[0]
    zero_shift = jnp.zeros((DEPTH, b_p, D_MODEL), x_prompt.dtype)
    y_p, k_p, v_p, wkv_p, shift_p = _trunk(x_prompt, zero_shift, None, None, None, layer_wts, glob)
    y_s, k_s, v_s, wkv_s, shift_s = _trunk(x_sample, state_shift, cache_k, cache_v, state_wkv, layer_wts, glob)
    return (y_p, y_s, k_p, v_p, wkv_p, shift_p, k_s, v_s, wkv_s, shift_s)
```

```python
import functools
import math

import jax
import jax.numpy as jnp
from jax import lax
from jax.experimental import pallas as pl
from jax.experimental.pallas import tpu as pltpu

F32 = jnp.float32
BF16 = jnp.bfloat16

D_MODEL = 1024
DEPTH = 4
HEAD_DIM = 64
ATT_WIDTH = 512
RWKV_WIDTH = 512
N_Q_HEADS = 8
N_KV_HEADS = 2
Q_PER_KV = 4
KV_WIDTH = 128
N_ATT_COLS = ATT_WIDTH + 2 * KV_WIDTH
WINDOW = 128
ATT_SCALE = HEAD_DIM ** -0.5
N_RWKV_HEADS = 8
D_DECAY_LORA = 64
D_AAA_LORA = 64
D_GATE_LORA = 128
D_MV_LORA = 32
DECAY_SCALE = math.exp(-0.5)
GN_EPS = 64e-5
LN_EPS = 1e-5
N_EXPERTS = 16
N_EXPERT_GROUPS = 4
EXPERTS_PER_GROUP = 4
EXPERT_FF = 512
ALPHA = (2 * DEPTH) ** 0.25

LANES = 128
SUBLANES = 8
VMEM_LIMIT_BYTES = 56 * 1024 * 1024
ROW_TILE = 256
MOE_ROW_TILE = 1024
WKV_CHUNK = 128
SAMPLE_ATT_BATCH = 8
SAMPLE_WKV_GROUP = 8


def _dg(a, b, ca, cb):
    return lax.dot_general(a, b, (((ca,), (cb,)), ((), ())), preferred_element_type=F32)


def _bdot(a, b):
    return _dg(a.astype(BF16), b.astype(BF16), 1, 0)


def _split2(x):
    hi = x.astype(BF16)
    lo = (x - hi.astype(F32)).astype(BF16)
    return hi, lo


def _dot3(a, b, ca=1, cb=0):
    ah, al = _split2(a)
    bh, bl = _split2(b)
    return _dg(ah, bh, ca, cb) + (_dg(ah, bl, ca, cb) + _dg(al, bh, ca, cb))


def _dot_exact_lhs(m_bf16, x, parts=3):
    acc = None
    rem = x
    for _ in range(parts):
        p = rem.astype(BF16)
        t = _dg(m_bf16, p, 1, 0)
        acc = t if acc is None else acc + t
        rem = rem - p.astype(F32)
    return acc


def _dot_exact_rhs(x, m_bf16, parts=3):
    acc = None
    rem = x
    for _ in range(parts):
        p = rem.astype(BF16)
        t = _dg(p, m_bf16, 1, 0)
        acc = t if acc is None else acc + t
        rem = rem - p.astype(F32)
    return acc


def _layer_norm(z, g, b):
    mu = jnp.mean(z, axis=-1, keepdims=True)
    var = jnp.mean(jnp.square(z - mu), axis=-1, keepdims=True)
    return (z - mu) * lax.rsqrt(var + LN_EPS) * g + b


def _params(*sem):
    return pltpu.CompilerParams(dimension_semantics=sem, vmem_limit_bytes=VMEM_LIMIT_BYTES)


def _const_spec(shape):
    nd = len(shape)
    return pl.BlockSpec(shape, lambda *_: (0,) * nd)


def _inproj_kernel(*refs, first, seq):
    if first:
        (x_ref, st_ref, watt_ref, wrw_ref, mu_ref, wda_ref, g2_ref, vec_ref, bd_ref,
         q_ref, ka_ref, va_ref, r_ref, lw_ref, k_ref, v_ref, a_ref, b_ref, g_ref, bonus_ref, carry_ref) = refs
    else:
        (x_ref, st_ref, watt_ref, wrw_ref, mu_ref, wda_ref, g2_ref, vec_ref, bd_ref, vfirst_ref, v2_ref,
         q_ref, ka_ref, va_ref, r_ref, lw_ref, k_ref, v_ref, a_ref, b_ref, g_ref, bonus_ref, carry_ref) = refs
    i = pl.program_id(0)
    x = x_ref[...]
    tm = x.shape[0]
    qkv = _dg(x.astype(BF16), watt_ref[...], 1, 0)
    q_ref[...] = qkv[:, :ATT_WIDTH]
    ka_ref[...] = qkv[:, ATT_WIDTH:ATT_WIDTH + KV_WIDTH]
    va_ref[...] = qkv[:, ATT_WIDTH + KV_WIDTH:]

    whole_tiles = seq >= tm
    st = jnp.broadcast_to(st_ref[0], (2 * SUBLANES, D_MODEL)) if whole_tiles else st_ref[...]
    pe = _dg(jnp.concatenate([x, st], axis=0).astype(BF16), wrw_ref[...], 1, 0)
    pc = pe[:tm]
    pst = pe[tm:]
    rowid = lax.broadcasted_iota(jnp.int32, pc.shape, 0)
    pp = pltpu.roll(pc, 1, 0)
    if whole_tiles:
        @pl.when(i == 0)
        def _init():
            carry_ref[...] = jnp.zeros_like(carry_ref)

        prev0 = jnp.where(i % (seq // tm) == 0, pst[0:1], carry_ref[SUBLANES - 1:SUBLANES, :])
        pp = jnp.where(rowid == 0, prev0, pp)
        carry_ref[...] = pc[tm - SUBLANES:]
    else:
        n_st = tm // seq
        er = lax.broadcasted_iota(jnp.int32, (tm, n_st), 0)
        ec = lax.broadcasted_iota(jnp.int32, (tm, n_st), 1)
        expand = (er == ec * seq).astype(BF16)
        pp = jnp.where(rowid % seq == 0, _dot_exact_lhs(expand, pst), pp)
    rw = pc + (pp - pc) * mu_ref[...]
    W = RWKV_WIDTH
    r = rw[:, 0:W]
    k = rw[:, W:2 * W]
    v = rw[:, 2 * W:3 * W]
    gl = rw[:, 3 * W:3 * W + D_GATE_LORA]
    wa = rw[:, 3 * W + D_GATE_LORA:3 * W + 2 * LANES]
    w0 = vec_ref[0:1, :]
    a0 = vec_ref[1:2, :]
    k_k = vec_ref[2:3, :]
    k_a = vec_ref[3:4, :]
    r_k = vec_ref[4:5, :]

    lane = lax.broadcasted_iota(jnp.int32, wa.shape, 1)
    wa_t = jnp.where(lane < D_DECAY_LORA, jnp.tanh(wa), wa)
    da = _bdot(wa_t, wda_ref[...])
    lw = -DECAY_SCALE * jax.nn.sigmoid(w0 + da[:, :W])
    a = jax.nn.sigmoid(a0 + da[:, W:])
    g = _bdot(jax.nn.sigmoid(gl), g2_ref[...])
    if not first:
        mv = rw[:, 3 * W + 2 * LANES:]
        v0 = vec_ref[5:6, :]
        v = v + (vfirst_ref[...] - v) * jax.nn.sigmoid(v0 + _bdot(mv, v2_ref[...]))
    bd = bd_ref[...]
    kk = k * k_k
    ssq = _dot_exact_rhs(kk * kk, bd)
    kk = kk / jnp.maximum(jnp.sqrt(ssq), 1e-12)
    k = k * (1.0 + (a - 1.0) * k_a)
    bonus = _dot_exact_rhs(r * k * r_k, bd) * v

    r_ref[...] = r
    lw_ref[...] = lw
    k_ref[...] = k
    v_ref[...] = v
    a_ref[...] = -kk
    b_ref[...] = kk * a
    g_ref[...] = g
    bonus_ref[...] = bonus


def _inproj(x, shift_state, wts, v_first, seq):
    n = x.shape[0]
    first = v_first is None
    tm = ROW_TILE
    row = lambda w: pl.BlockSpec((tm, w), lambda i: (i, 0))
    if seq >= tm:
        assert seq % tm == 0
        st = shift_state.reshape(-1, 1, D_MODEL)
        st_spec = pl.BlockSpec((1, 1, D_MODEL), lambda i: (i // (seq // tm), 0, 0))
    else:
        assert tm % seq == 0
        st = shift_state
        st_spec = pl.BlockSpec((tm // seq, D_MODEL), lambda i: (i, 0))
    ins = [x, st, wts["watt"], wts["wrw"], wts["mu"], wts["wda"], wts["g2"], wts["vecA"], wts["bd"]]
    in_specs = [row(D_MODEL), st_spec] + [_const_spec(a.shape) for a in ins[2:]]
    if not first:
        ins += [v_first, wts["v2"]]
        in_specs += [row(RWKV_WIDTH), _const_spec(wts["v2"].shape)]
    widths = [ATT_WIDTH, KV_WIDTH, KV_WIDTH] + [RWKV_WIDTH] * 8
    return pl.pallas_call(
        functools.partial(_inproj_kernel, first=first, seq=seq),
        grid=(n // tm,),
        in_specs=in_specs,
        out_specs=[row(w) for w in widths],
        out_shape=[jax.ShapeDtypeStruct((n, w), F32) for w in widths],
        scratch_shapes=[pltpu.VMEM((SUBLANES, wts["wrw"].shape[1]), F32)],
        compiler_params=_params("arbitrary"),
        name="inproj",
    )(*ins)


def _dot3s(a_sp, b_sp, ca=1, cb=0):
    (ah, al), (bh, bl) = a_sp, b_sp
    return _dg(ah, bh, ca, cb) + (_dg(ah, bl, ca, cb) + _dg(al, bh, ca, cb))


def _wkv_kernel(*refs, group, seq_chunk, has_state):
    if has_state:
        r_ref, lw_ref, k_ref, v_ref, a_ref, b_ref, s0_ref, y_ref, sout_ref, s_scr = refs
    else:
        r_ref, lw_ref, k_ref, v_ref, a_ref, b_ref, y_ref, sout_ref, s_scr = refs
    G, Ls = group, seq_chunk
    L = G * Ls
    H, N = N_RWKV_HEADS, HEAD_DIM
    c_idx = pl.program_id(1)

    @pl.when(c_idx == 0)
    def _init():
        if has_state:
            s_scr[...] = s0_ref[...]
        else:
            s_scr[...] = jnp.zeros_like(s_scr)

    row = lax.broadcasted_iota(jnp.int32, (L, L), 0)
    col = lax.broadcasted_iota(jnp.int32, (L, L), 1)
    same = (row // Ls) == (col // Ls)
    incl = same & (row >= col)
    strict = same & (row > col)
    eye = (row == col).astype(F32)
    row2 = lax.broadcasted_iota(jnp.int32, (2 * L, 2 * L), 0)
    col2 = lax.broadcasted_iota(jnp.int32, (2 * L, 2 * L), 1)
    t_q = row2 % L
    t_k = col2 % L
    mask2 = ((t_q // Ls) == (t_k // Ls)) & ((t_q > t_k) | ((row2 >= L) & (t_q == t_k)))

    lw_all = lw_ref[...]
    cum_all = _dot_exact_lhs(incl.astype(BF16), lw_all)
    if G == 1:
        tot_all = jnp.broadcast_to(cum_all[L - 1:L, :], cum_all.shape)
    else:
        tot_all = _dot_exact_lhs(same.astype(BF16), lw_all)
    e_end = jnp.exp(tot_all - cum_all)
    etot_all = jnp.exp(tot_all)
    a_all = a_ref[...]
    r_all = r_ref[...]
    b_all = b_ref[...]
    k_all = k_ref[...]
    v_all = v_ref[...]
    ar_all = jnp.concatenate([a_all * jnp.exp(cum_all - lw_all), r_all * jnp.exp(cum_all)], axis=0)
    ar_sp = _split2(ar_all)
    if G == 1:
        mid = cum_all[L // 2 - 1:L // 2, :]
        e_neg = jnp.exp(mid - cum_all)
        arc_sp = _split2(jnp.concatenate([a_all * jnp.exp(cum_all - lw_all - mid), r_all * jnp.exp(cum_all - mid)],
                                         axis=0))
    else:
        e_neg = jnp.exp(-cum_all)
        arc_sp = ar_sp
    bk_sp = _split2(jnp.concatenate([b_all * e_neg, k_all * e_neg], axis=0))
    bkh_sp = _split2(jnp.concatenate([b_all * e_end, k_all * e_end], axis=0))
    heads = range(H)
    hsl = lambda t, h: t[:, h * N:(h + 1) * N]
    hsp = lambda sp, h: (hsl(sp[0], h), hsl(sp[1], h))

    m = [jnp.where(mask2, _dot3s(hsp(arc_sp, h), hsp(bk_sp, h), 1, 1), 0.0) for h in heads]
    m_sp = [_split2(t) for t in m]
    x = [t[:L, :L] for t in m]
    t_inv = [eye + t for t in x]
    for _ in range(int(round(math.log2(Ls))) - 1):
        x_sp = [_split2(t) for t in x]
        x = [_dot3s(s, s) for s in x_sp]
        x_sp = [_split2(t) for t in x]
        t_inv = [t + _dot3s(_split2(t), s) for t, s in zip(t_inv, x_sp)]

    s0 = [[s_scr[g, h] for h in heads] for g in range(G)]
    s0_sp = [[_split2(s0[g][h]) for h in heads] for g in range(G)]

    def seq_rows(sp, g, h):
        if G == 1:
            return hsp(sp, h)
        return tuple(jnp.concatenate([hsl(p, h)[g * Ls:(g + 1) * Ls], hsl(p, h)[L + g * Ls:L + (g + 1) * Ls]], axis=0)
                     for p in sp)

    st = [[_dot3s(seq_rows(ar_sp, g, h), s0_sp[g][h], 1, 1) for h in heads] for g in range(G)]
    rhs_st = [jnp.concatenate([st[g][h][:Ls] for g in range(G)], axis=0) for h in heads]
    y_st = [jnp.concatenate([st[g][h][Ls:] for g in range(G)], axis=0) for h in heads]

    v_h = [hsl(v_all, h) for h in heads]
    zeros = jnp.zeros((L, N), F32)
    m_top = [(s[0][:L], s[1][:L]) for s in m_sp]
    m_bot = [(s[0][L:], s[1][L:]) for s in m_sp]
    rhs = [rhs_st[h] + _dot3s(m_top[h], _split2(jnp.concatenate([zeros, v_h[h]], axis=0))) for h in heads]
    u = [_dot3s(_split2(t_inv[h]), _split2(rhs[h])) for h in heads]
    uv_sp = [_split2(jnp.concatenate([u[h], v_h[h]], axis=0)) for h in heads]
    y = [y_st[h] + _dot3s(m_bot[h], uv_sp[h]) for h in heads]
    for g in range(G):
        for h in heads:
            if G == 1:
                uv_g = uv_sp[h]
            else:
                uv_g = tuple(jnp.concatenate([p[g * Ls:(g + 1) * Ls], p[L + g * Ls:L + (g + 1) * Ls]], axis=0)
                             for p in uv_sp[h])
            upd = _dot3s(uv_g, seq_rows(bkh_sp, g, h), 0, 0)
            s_scr[g, h] = s0[g][h] * hsl(etot_all, h)[g * Ls:g * Ls + 1] + upd
    ys = []
    for h in heads:
        mu = jnp.mean(y[h], axis=-1, keepdims=True)
        var = jnp.mean(jnp.square(y[h] - mu), axis=-1, keepdims=True)
        ys.append((y[h] - mu) * lax.rsqrt(var + GN_EPS))
    y_ref[...] = jnp.concatenate(ys, axis=1)

    @pl.when(c_idx == pl.num_programs(1) - 1)
    def _fin():
        sout_ref[...] = s_scr[...]


def _wkv_scan(r, lw, k, v, a, b, state, n_seq, seq, seq_chunk, group):
    n_chunks = seq // seq_chunk
    assert group == 1 or n_chunks == 1
    has_state = state is not None
    rows = group * seq_chunk
    tok_spec = pl.BlockSpec((rows, RWKV_WIDTH), lambda i, c: (i * n_chunks + c, 0))
    st_spec = pl.BlockSpec((group, N_RWKV_HEADS, HEAD_DIM, HEAD_DIM), lambda i, c: (i, 0, 0, 0))
    in_specs = [tok_spec] * 6 + ([st_spec] if has_state else [])
    args = (r, lw, k, v, a, b) + ((state,) if has_state else ())
    return pl.pallas_call(
        functools.partial(_wkv_kernel, group=group, seq_chunk=seq_chunk, has_state=has_state),
        grid=(n_seq // group, n_chunks),
        in_specs=in_specs,
        out_specs=[tok_spec, st_spec],
        out_shape=[jax.ShapeDtypeStruct((n_seq * seq, RWKV_WIDTH), F32),
                   jax.ShapeDtypeStruct((n_seq, N_RWKV_HEADS, HEAD_DIM, HEAD_DIM), F32)],
        scratch_shapes=[pltpu.VMEM((group, N_RWKV_HEADS, HEAD_DIM, HEAD_DIM), F32)],
        compiler_params=_params("arbitrary", "arbitrary"),
        name="wkv_scan",
    )(*args)


def _sink_softmax(s, sink):
    m = sink
    for t in s:
        m = jnp.maximum(m, jnp.max(t, axis=-1, keepdims=True))
    es = [jnp.exp(t - m) for t in s]
    den = jnp.exp(sink - m)
    for e in es:
        den = den + jnp.sum(e, axis=-1, keepdims=True)
    return [e / den for e in es]


def _attn_prompt_kernel(q_ref, kc_ref, kp_ref, vc_ref, vp_ref, sink_ref, o_ref):
    L = WINDOW
    n = pl.program_id(1)
    q = q_ref[...].astype(BF16)
    kw = jnp.concatenate([kp_ref[...], kc_ref[...]], axis=0).astype(BF16)
    vw = jnp.concatenate([vp_ref[...], vc_ref[...]], axis=0).astype(BF16)
    qi = lax.broadcasted_iota(jnp.int32, (L, 2 * L), 0)
    kj = lax.broadcasted_iota(jnp.int32, (L, 2 * L), 1)
    diff = qi + L - kj
    mask = (diff >= 0) & (diff < WINDOW) & ((kj >= L) | (n > 0))
    lane = lax.broadcasted_iota(jnp.int32, vw.shape, 1)
    one = jnp.ones_like(vw)
    vext = [jnp.where(lane < HEAD_DIM, vw, one), jnp.where(lane >= HEAD_DIM, vw, one)]
    outs = []
    for h in range(N_Q_HEADS):
        hk = h // Q_PER_KV
        s = _dg(q[:, h * HEAD_DIM:(h + 1) * HEAD_DIM], kw[:, hk * HEAD_DIM:(hk + 1) * HEAD_DIM], 1, 1) * ATT_SCALE
        s = jnp.where(mask, s, -jnp.inf)
        sink = sink_ref[h:h + 1, 0:1]
        m = jnp.maximum(jnp.max(s, axis=-1, keepdims=True), sink)
        e = jnp.exp(s - m).astype(BF16)
        oe = _dg(e, vext[hk], 1, 0)
        o = oe[:, hk * HEAD_DIM:(hk + 1) * HEAD_DIM]
        rs = oe[:, (1 - hk) * HEAD_DIM:(1 - hk) * HEAD_DIM + 1]
        outs.append(o / (rs + jnp.exp(sink - m)))
    o_ref[...] = jnp.concatenate(outs, axis=1)


def _attn_prompt(q, k, v, sink_rows, n_batch, seq):
    nb = seq // WINDOW
    cur = lambda w: pl.BlockSpec((WINDOW, w), lambda b, n: (b * nb + n, 0))
    prev = lambda w: pl.BlockSpec((WINDOW, w), lambda b, n: (b * nb + jnp.maximum(n - 1, 0), 0))
    return pl.pallas_call(
        _attn_prompt_kernel,
        grid=(n_batch, nb),
        in_specs=[cur(ATT_WIDTH), cur(KV_WIDTH), prev(KV_WIDTH), cur(KV_WIDTH), prev(KV_WIDTH),
                  _const_spec(sink_rows.shape)],
        out_specs=cur(ATT_WIDTH),
        out_shape=jax.ShapeDtypeStruct((n_batch * seq, ATT_WIDTH), F32),
        compiler_params=_params("arbitrary", "arbitrary"),
        name="attn_prompt",
    )(q, k, k, v, v, sink_rows)


def _attn_sample_kernel(q_ref, kn_ref, vn_ref, ck_ref, cv_ref, sink_ref, o_ref, *, seq):
    S = seq
    W = WINDOW
    B = SAMPLE_ATT_BATCH
    R = Q_PER_KV * S
    tq1 = lax.broadcasted_iota(jnp.int32, (R, W), 0) % S
    kj1 = lax.broadcasted_iota(jnp.int32, (R, W), 1)
    mask_cache = kj1 > tq1
    tq2 = lax.broadcasted_iota(jnp.int32, (R, S), 0) % S
    kj2 = lax.broadcasted_iota(jnp.int32, (R, S), 1)
    mask_new = kj2 <= tq2
    kvs = range(N_KV_HEADS)
    hsl = lambda t, h: t[:, h * HEAD_DIM:(h + 1) * HEAD_DIM]
    group = lambda hk: range(hk * Q_PER_KV, (hk + 1) * Q_PER_KV)
    seq_rows = lambda t, b: t[b * S:(b + 1) * S]
    q = q_ref[...]
    kn = kn_ref[...]
    vn = vn_ref[...]
    qh = [hsl(q, h) for h in range(N_Q_HEADS)]
    knh = [hsl(kn, hk) for hk in kvs]
    vnh = [hsl(vn, hk) for hk in kvs]
    sinks = [jnp.concatenate([jnp.broadcast_to(sink_ref[h:h + 1, 0:1], (S, 1)) for h in group(hk)], axis=0)
             for hk in kvs]
    items = [(b, hk) for b in range(B) for hk in kvs]
    qs = [jnp.concatenate([seq_rows(qh[h], b) for h in group(hk)], axis=0).astype(BF16) for b, hk in items]
    ck = [ck_ref[b].astype(BF16) for b in range(B)]
    cv = [cv_ref[b].astype(BF16) for b in range(B)]
    s1 = [jnp.where(mask_cache, _dg(qs[i], hsl(ck[b], hk), 1, 1) * ATT_SCALE, -jnp.inf)
          for i, (b, hk) in enumerate(items)]
    s2 = [jnp.where(mask_new, _dg(qs[i], seq_rows(knh[hk], b).astype(BF16), 1, 1) * ATT_SCALE, -jnp.inf)
          for i, (b, hk) in enumerate(items)]
    ps = [_sink_softmax([s1[i], s2[i]], sinks[hk]) for i, (b, hk) in enumerate(items)]
    outs = [_dg(ps[i][0].astype(BF16), hsl(cv[b], hk), 1, 0)
            + _dg(ps[i][1].astype(BF16), seq_rows(vnh[hk], b).astype(BF16), 1, 0)
            for i, (b, hk) in enumerate(items)]
    rows = [jnp.concatenate([outs[b * N_KV_HEADS + hk][g * S:(g + 1) * S] for hk in kvs for g in range(Q_PER_KV)],
                            axis=1) for b in range(B)]
    o_ref[...] = jnp.concatenate(rows, axis=0)


def _attn_sample(q, k, v, cache_k, cache_v, sink_rows, n_batch, seq):
    bb = SAMPLE_ATT_BATCH
    tok = lambda w: pl.BlockSpec((bb * seq, w), lambda i: (i, 0))
    cache = pl.BlockSpec((bb, WINDOW, KV_WIDTH), lambda i: (i, 0, 0))
    return pl.pallas_call(
        functools.partial(_attn_sample_kernel, seq=seq),
        grid=(n_batch // bb,),
        in_specs=[tok(ATT_WIDTH), tok(KV_WIDTH), tok(KV_WIDTH), cache, cache, _const_spec(sink_rows.shape)],
        out_specs=tok(ATT_WIDTH),
        out_shape=jax.ShapeDtypeStruct((n_batch * seq, ATT_WIDTH), F32),
        compiler_params=_params("arbitrary"),
        name="attn_sample",
    )(q, k, v, cache_k, cache_v, sink_rows)


def _second_max4(a, b, c, d):
    return jnp.maximum(jnp.maximum(jnp.minimum(a, b), jnp.minimum(c, d)),
                       jnp.minimum(jnp.maximum(a, b), jnp.maximum(c, d)))


def _route(logits_t, bias_col):
    G, E = N_EXPERT_GROUPS, EXPERTS_PER_GROUP
    m = jnp.max(logits_t, axis=0, keepdims=True)
    ex = jnp.exp(logits_t - m)
    probs = ex / jnp.sum(ex, axis=0, keepdims=True)
    sel = probs + bias_col
    p = [probs[e:e + 1, :] for e in range(N_EXPERTS)]
    s = [sel[e:e + 1, :] for e in range(N_EXPERTS)]
    gs = []
    for g in range(G):
        a, b, c, d = s[E * g:E * g + E]
        top1 = jnp.maximum(jnp.maximum(a, b), jnp.maximum(c, d))
        gs.append(top1 + _second_max4(a, b, c, d))
    best = jnp.zeros_like(gs[0], dtype=jnp.int32)
    best_s = gs[0]
    for g in range(1, G):
        upd = gs[g] > best_s
        best = jnp.where(upd, g, best)
        best_s = jnp.where(upd, gs[g], best_s)

    def pick(vals, j):
        out = vals[j]
        for g in range(1, G):
            out = jnp.where(best == g, vals[E * g + j], out)
        return out

    ig = [pick(s, j) for j in range(E)]
    pg = [pick(p, j) for j in range(E)]
    l1 = jnp.zeros_like(best)
    v1 = ig[0]
    for j in range(1, E):
        upd = ig[j] > v1
        l1 = jnp.where(upd, j, l1)
        v1 = jnp.where(upd, ig[j], v1)
    l2 = jnp.full_like(best, -1)
    v2 = jnp.full_like(v1, -jnp.inf)
    for j in range(E):
        upd = (l1 != j) & (ig[j] > v2)
        l2 = jnp.where(upd, j, l2)
        v2 = jnp.where(upd, ig[j], v2)
    zero = jnp.zeros_like(v1)
    w1 = zero
    w2 = zero
    for j in range(E):
        w1 = jnp.where(l1 == j, pg[j], w1)
        w2 = jnp.where(l2 == j, pg[j], w2)
    wsum = w1 + w2
    w1 = w1 / wsum
    w2 = w2 / wsum
    rows = []
    for e in range(N_EXPERTS):
        g, j = divmod(e, E)
        in_g = best == g
        rows.append(jnp.where(in_g & (l1 == j), w1, zero) + jnp.where(in_g & (l2 == j), w2, zero))
    return jnp.concatenate(rows, axis=0)


def _post_kernel(yn_ref, bonus_ref, g_ref, att_ref, x_ref, woa_ref, wor_ref, gn_ref, ln_ref,
                 wrt_ref, rb_ref, x1_ref, gates_ref):
    rw_out = (yn_ref[...] * gn_ref[0:1, :] + gn_ref[1:2, :] + bonus_ref[...]) * g_ref[...]
    mixed = _bdot(att_ref[...], woa_ref[...]) + _bdot(rw_out, wor_ref[...])
    x1 = _layer_norm(ALPHA * x_ref[...] + mixed, ln_ref[0:1, :], ln_ref[1:2, :])
    x1_ref[...] = x1
    logits_t = _dot3(wrt_ref[...], x1, 1, 1)
    gates_t = _route(logits_t, rb_ref[:, 0:1])
    pad = jnp.zeros((LANES - N_EXPERTS, gates_t.shape[1]), F32)
    gates_ref[...] = jnp.concatenate([gates_t, pad], axis=0).T


def _post(yn, bonus, g, att, x, wts, glob):
    n = x.shape[0]
    tm = ROW_TILE
    row = lambda w: pl.BlockSpec((tm, w), lambda i: (i, 0))
    consts = [wts["woa"], wts["wor"], wts["gn"], wts["ln1"], glob["wrt"], glob["rb"]]
    return pl.pallas_call(
        _post_kernel,
        grid=(n // tm,),
        in_specs=[row(RWKV_WIDTH)] * 3 + [row(ATT_WIDTH), row(D_MODEL)] + [_const_spec(a.shape) for a in consts],
        out_specs=[row(D_MODEL), row(LANES)],
        out_shape=[jax.ShapeDtypeStruct((n, D_MODEL), F32), jax.ShapeDtypeStruct((n, LANES), F32)],
        compiler_params=_params("arbitrary"),
        name="post_mix",
    )(yn, bonus, g, att, x, *consts)


def _moe_kernel(x_ref, gates_ref, wg_ref, wu_ref, wd_ref, ln_ref, o_ref, acc_ref, xb_ref):
    e = pl.program_id(1)

    @pl.when(e == 0)
    def _init():
        acc_ref[...] = jnp.zeros_like(acc_ref)
        xb_ref[...] = x_ref[...].astype(BF16)

    xb = xb_ref[...]
    gates = gates_ref[...]
    lane = lax.broadcasted_iota(jnp.int32, gates.shape, 1)
    gcol = jnp.sum(jnp.where(lane == e, gates, 0.0), axis=1, keepdims=True)
    hg = _dg(xb, wg_ref[0], 1, 0)
    hu = _dg(xb, wu_ref[0], 1, 0)
    h = jax.nn.silu(hg) * hu * gcol
    acc_ref[...] += _dg(h.astype(BF16), wd_ref[0], 1, 0)

    @pl.when(e == pl.num_programs(1) - 1)
    def _fin():
        o_ref[...] = _layer_norm(ALPHA * x_ref[...] + acc_ref[...], ln_ref[0:1, :], ln_ref[1:2, :])


def _moe(x1, gates, wts):
    n = x1.shape[0]
    tm = MOE_ROW_TILE
    return pl.pallas_call(
        _moe_kernel,
        grid=(n // tm, N_EXPERTS),
        in_specs=[pl.BlockSpec((tm, D_MODEL), lambda i, e: (i, 0)),
                  pl.BlockSpec((tm, LANES), lambda i, e: (i, 0)),
                  pl.BlockSpec((1, D_MODEL, EXPERT_FF), lambda i, e: (e, 0, 0)),
                  pl.BlockSpec((1, D_MODEL, EXPERT_FF), lambda i, e: (e, 0, 0)),
                  pl.BlockSpec((1, EXPERT_FF, D_MODEL), lambda i, e: (e, 0, 0)),
                  _const_spec(wts["ln2"].shape)],
        out_specs=pl.BlockSpec((tm, D_MODEL), lambda i, e: (i, 0)),
        out_shape=jax.ShapeDtypeStruct((n, D_MODEL), F32),
        scratch_shapes=[pltpu.VMEM((tm, D_MODEL), F32), pltpu.VMEM((tm, D_MODEL), BF16)],
        compiler_params=_params("arbitrary", "arbitrary"),
        name="moe",
    )(x1, gates, wts["wg"], wts["wu"], wts["wd"], wts["ln2"])


def _rows8(vectors, width):
    rows = [v.reshape(1, width).astype(F32) for v in vectors]
    rows.append(jnp.zeros((8 - len(rows), width), F32))
    return jnp.concatenate(rows, axis=0)


def _prep_layer(l, p):
    W = RWKV_WIDTH
    w_l = p["w_in"][l]
    rwc = w_l[:, N_ATT_COLS:]
    mu = p["mu_rwkv"][l]
    o_w, o_k, o_v, o_a, o_g = W, W + D_DECAY_LORA, 2 * W + D_DECAY_LORA, 3 * W + D_DECAY_LORA, 3 * W + 2 * D_DECAY_LORA

    def reorder(t):
        parts = [t[..., 0:W], t[..., o_k:o_k + W], t[..., o_v:o_v + W], t[..., o_g:o_g + D_GATE_LORA],
                 t[..., o_w:o_w + D_DECAY_LORA], t[..., o_a:o_a + D_AAA_LORA]]
        return parts

    w_parts = reorder(rwc)
    mu_parts = reorder(mu)
    if l > 0:
        padw = LANES - D_MV_LORA
        w_parts += [p["w_vres_in"][l - 1], jnp.zeros((D_MODEL, padw), F32)]
        mu_parts += [p["mu_vres"][l - 1], jnp.zeros((padw,), F32)]
    wrw = jnp.concatenate(w_parts, axis=1).astype(BF16)
    mu_row = jnp.concatenate(mu_parts).reshape(1, -1)
    zero = jnp.zeros((D_DECAY_LORA, W), F32)
    wda = jnp.concatenate([jnp.concatenate([p["decay_w2"][l], zero], axis=1),
                           jnp.concatenate([zero, p["aaa_a2"][l]], axis=1)], axis=0).astype(BF16)
    vecs = [p["decay_w0"][l], p["aaa_a0"][l], p["k_k"][l], p["k_a"][l], p["r_k"][l].reshape(W)]
    out = {
        "watt": w_l[:, :N_ATT_COLS].astype(BF16),
        "wrw": wrw,
        "mu": mu_row,
        "wda": wda,
        "g2": p["gate_g2"][l].astype(BF16),
        "woa": p["w_o"][l][:ATT_WIDTH].astype(BF16),
        "wor": p["w_o"][l][ATT_WIDTH:].astype(BF16),
        "gn": _rows8([p["gn_g"][l], p["gn_b"][l]], W),
        "ln1": _rows8([p["ln1_g"][l], p["ln1_b"][l]], D_MODEL),
        "ln2": _rows8([p["ln2_g"][l], p["ln2_b"][l]], D_MODEL),
        "wg": p["w_gate"][l].astype(BF16),
        "wu": p["w_up"][l].astype(BF16),
        "wd": p["w_down"][l].astype(BF16),
        "sink_rows": jnp.broadcast_to(p["sinks"][l].reshape(N_Q_HEADS, 1), (N_Q_HEADS, LANES)).astype(F32),
    }
    if l > 0:
        vecs.append(p["vres_v0"][l - 1])
        out["v2"] = jnp.concatenate([p["vres_v2"][l - 1], jnp.zeros((LANES - D_MV_LORA, W), F32)], axis=0).astype(BF16)
    out["vecA"] = _rows8(vecs, W)
    hid = jnp.arange(W) // HEAD_DIM
    out["bd"] = (hid[:, None] == hid[None, :]).astype(BF16)
    return out


def _trunk(x3, shift_prev, cache_k, cache_v, wkv_prev, layer_wts, glob):
    decode = cache_k is not None
    n_batch, seq, _ = x3.shape
    x = x3.reshape(n_batch * seq, D_MODEL)
    new_k, new_v, new_wkv, new_shift = [], [], [], []
    v_first = None
    for l in range(DEPTH):
        wts = layer_wts[l]
        new_shift.append(x.reshape(n_batch, seq, D_MODEL)[:, -1])
        q, ka, va, r, lw, k, v, a, b, g, bonus = _inproj(x, shift_prev[l], wts, v_first, seq)
        if l == 0:
            v_first = v
        if decode:
            ck = cache_k[l].reshape(n_batch, WINDOW, KV_WIDTH)
            cv = cache_v[l].reshape(n_batch, WINDOW, KV_WIDTH)
            att = _attn_sample(q, ka, va, ck, cv, wts["sink_rows"], n_batch, seq)
            new_k.append(ka.reshape(n_batch, seq, N_KV_HEADS, HEAD_DIM))
            new_v.append(va.reshape(n_batch, seq, N_KV_HEADS, HEAD_DIM))
            yn, s_out = _wkv_scan(r, lw, k, v, a, b, wkv_prev[l], n_batch, seq, seq, SAMPLE_WKV_GROUP)
        else:
            att = _attn_prompt(q, ka, va, wts["sink_rows"], n_batch, seq)
            new_k.append(ka.reshape(n_batch, seq, N_KV_HEADS, HEAD_DIM)[:, -WINDOW:])
            new_v.append(va.reshape(n_batch, seq, N_KV_HEADS, HEAD_DIM)[:, -WINDOW:])
            yn, s_out = _wkv_scan(r, lw, k, v, a, b, None, n_batch, seq, WKV_CHUNK, 1)
        new_wkv.append(s_out)
        x1, gates = _post(yn, bonus, g, att, x, wts, glob)
        x = _moe(x1, gates, wts)
    return (x.reshape(n_batch, seq, D_MODEL), jnp.stack(new_k), jnp.stack(new_v), jnp.stack(new_wkv),
            jnp.stack(new_shift))


def kernel(x_prompt, x_sample, cache_k, cache_v, state_wkv, state_shift, w_in, w_vres_in, mu_rwkv, mu_vres, sinks, decay_w0, decay_w2, aaa_a0, aaa_a2, vres_v0, vres_v2, gate_g2, k_k, k_a, r_k, gn_g, gn_b, w_o, ln1_g, ln1_b, w_router, router_bias, w_gate, w_up, w_down, ln2_g, ln2_b):
    p = dict(w_in=w_in, w_vres_in=w_vres_in, mu_rwkv=mu_rwkv, mu_vres=mu_vres, sinks=sinks,
             decay_w0=decay_w0, decay_w2=decay_w2, aaa_a0=aaa_a0, aaa_a2=aaa_a2,
             vres_v0=vres_v0, vres_v2=vres_v2, gate_g2=gate_g2, k_k=k_k, k_a=k_a, r_k=r_k,
             gn_g=gn_g, gn_b=gn_b, w_o=w_o, ln1_g=ln1_g, ln1_b=ln1_b,
             w_gate=w_gate, w_up=w_up, w_down=w_down, ln2_g=ln2_g, ln2_b=ln2_b)
    layer_wts = [_prep_layer(l, p) for l in range(DEPTH)]
    glob = {
        "wrt": w_router.T.astype(F32),
        "rb": jnp.broadcast_to(router_bias.reshape(N_EXPERTS, 1), (N_EXPERTS, LANES)).astype(F32),
    }
    b_p = x_prompt.shape[0]
    zero_shift = jnp.zeros((DEPTH, b_p, D_MODEL), x_prompt.dtype)
    y_p, k_p, v_p, wkv_p, shift_p = _trunk(x_prompt, zero_shift, None, None, None, layer_wts, glob)
    y_s, k_s, v_s, wkv_s, shift_s = _trunk(x_sample, state_shift, cache_k, cache_v, state_wkv, layer_wts, glob)
    return (y_p, y_s, k_p, v_p, wkv_p, shift_p, k_s, v_s, wkv_s, shift_s)
```

```python
import functools
import math

import jax
import jax.numpy as jnp
from jax import lax
from jax.experimental import pallas as pl
from jax.experimental.pallas import tpu as pltpu

F32 = jnp.float32
BF16 = jnp.bfloat16

D_MODEL = 1024
DEPTH = 4
HEAD_DIM = 64
ATT_WIDTH = 512
RWKV_WIDTH = 512
N_Q_HEADS = 8
N_KV_HEADS = 2
Q_PER_KV = 4
KV_WIDTH = 128
N_ATT_COLS = ATT_WIDTH + 2 * KV_WIDTH
WINDOW = 128
ATT_SCALE = HEAD_DIM ** -0.5
N_RWKV_HEADS = 8
D_DECAY_LORA = 64
D_AAA_LORA = 64
D_GATE_LORA = 128
D_MV_LORA = 32
DECAY_SCALE = math.exp(-0.5)
GN_EPS = 64e-5
LN_EPS = 1e-5
N_EXPERTS = 16
N_EXPERT_GROUPS = 4
EXPERTS_PER_GROUP = 4
EXPERT_FF = 512
ALPHA = (2 * DEPTH) ** 0.25

LANES = 128
SUBLANES = 8
VMEM_LIMIT_BYTES = 56 * 1024 * 1024
ROW_TILE = 256
MOE_ROW_TILE = 1024
WKV_CHUNK = 128
SAMPLE_ATT_BATCH = 8
SAMPLE_WKV_GROUP = 8


def _dg(a, b, ca, cb):
    return lax.dot_general(a, b, (((ca,), (cb,)), ((), ())), preferred_element_type=F32)


def _bdot(a, b):
    return _dg(a.astype(BF16), b.astype(BF16), 1, 0)


def _split2(x):
    hi = x.astype(BF16)
    lo = (x - hi.astype(F32)).astype(BF16)
    return hi, lo


def _dot3(a, b, ca=1, cb=0):
    ah, al = _split2(a)
    bh, bl = _split2(b)
    return _dg(ah, bh, ca, cb) + (_dg(ah, bl, ca, cb) + _dg(al, bh, ca, cb))


def _dot_exact_lhs(m_bf16, x, parts=3):
    acc = None
    rem = x
    for _ in range(parts):
        p = rem.astype(BF16)
        t = _dg(m_bf16, p, 1, 0)
        acc = t if acc is None else acc + t
        rem = rem - p.astype(F32)
    return acc


def _dot_exact_rhs(x, m_bf16, parts=3):
    acc = None
    rem = x
    for _ in range(parts):
        p = rem.astype(BF16)
        t = _dg(p, m_bf16, 1, 0)
        acc = t if acc is None else acc + t
        rem = rem - p.astype(F32)
    return acc


def _layer_norm(z, g, b):
    mu = jnp.mean(z, axis=-1, keepdims=True)
    var = jnp.mean(jnp.square(z - mu), axis=-1, keepdims=True)
    return (z - mu) * lax.rsqrt(var + LN_EPS) * g + b


def _params(*sem):
    return pltpu.CompilerParams(dimension_semantics=sem, vmem_limit_bytes=VMEM_LIMIT_BYTES)


def _const_spec(shape):
    nd = len(shape)
    return pl.BlockSpec(shape, lambda *_: (0,) * nd)


def _inproj_kernel(*refs, first, seq):
    if first:
        (x_ref, st_ref, watt_ref, wrw_ref, mu_ref, wda_ref, g2_ref, vec_ref, bd_ref,
         q_ref, ka_ref, va_ref, r_ref, lw_ref, k_ref, v_ref, a_ref, b_ref, g_ref, bonus_ref, carry_ref) = refs
    else:
        (x_ref, st_ref, watt_ref, wrw_ref, mu_ref, wda_ref, g2_ref, vec_ref, bd_ref, vfirst_ref, v2_ref,
         q_ref, ka_ref, va_ref, r_ref, lw_ref, k_ref, v_ref, a_ref, b_ref, g_ref, bonus_ref, carry_ref) = refs
    i = pl.program_id(0)
    x = x_ref[...]
    tm = x.shape[0]
    qkv = _dg(x.astype(BF16), watt_ref[...], 1, 0)
    q_ref[...] = qkv[:, :ATT_WIDTH]
    ka_ref[...] = qkv[:, ATT_WIDTH:ATT_WIDTH + KV_WIDTH]
    va_ref[...] = qkv[:, ATT_WIDTH + KV_WIDTH:]

    whole_tiles = seq >= tm
    st = jnp.broadcast_to(st_ref[0], (2 * SUBLANES, D_MODEL)) if whole_tiles else st_ref[...]
    pe = _dg(jnp.concatenate([x, st], axis=0).astype(BF16), wrw_ref[...], 1, 0)
    pc = pe[:tm]
    pst = pe[tm:]
    rowid = lax.broadcasted_iota(jnp.int32, pc.shape, 0)
    pp = pltpu.roll(pc, 1, 0)
    if whole_tiles:
        @pl.when(i == 0)
        def _init():
            carry_ref[...] = jnp.zeros_like(carry_ref)

        prev0 = jnp.where(i % (seq // tm) == 0, pst[0:1], carry_ref[SUBLANES - 1:SUBLANES, :])
        pp = jnp.where(rowid == 0, prev0, pp)
        carry_ref[...] = pc[tm - SUBLANES:]
    else:
        n_st = tm // seq
        er = lax.broadcasted_iota(jnp.int32, (tm, n_st), 0)
        ec = lax.broadcasted_iota(jnp.int32, (tm, n_st), 1)
        expand = (er == ec * seq).astype(BF16)
        pp = jnp.where(rowid % seq == 0, _dot_exact_lhs(expand, pst), pp)
    rw = pc + (pp - pc) * mu_ref[...]
    W = RWKV_WIDTH
    r = rw[:, 0:W]
    k = rw[:, W:2 * W]
    v = rw[:, 2 * W:3 * W]
    gl = rw[:, 3 * W:3 * W + D_GATE_LORA]
    wa = rw[:, 3 * W + D_GATE_LORA:3 * W + 2 * LANES]
    w0 = vec_ref[0:1, :]
    a0 = vec_ref[1:2, :]
    k_k = vec_ref[2:3, :]
    k_a = vec_ref[3:4, :]
    r_k = vec_ref[4:5, :]

    lane = lax.broadcasted_iota(jnp.int32, wa.shape, 1)
    wa_t = jnp.where(lane < D_DECAY_LORA, jnp.tanh(wa), wa)
    da = _bdot(wa_t, wda_ref[...])
    lw = -DECAY_SCALE * jax.nn.sigmoid(w0 + da[:, :W])
    a = jax.nn.sigmoid(a0 + da[:, W:])
    g = _bdot(jax.nn.sigmoid(gl), g2_ref[...])
    if not first:
        mv = rw[:, 3 * W + 2 * LANES:]
        v0 = vec_ref[5:6, :]
        v = v + (vfirst_ref[...] - v) * jax.nn.sigmoid(v0 + _bdot(mv, v2_ref[...]))
    bd = bd_ref[...]
    kk = k * k_k
    ssq = _dot_exact_rhs(kk * kk, bd)
    kk = kk / jnp.maximum(jnp.sqrt(ssq), 1e-12)
    k = k * (1.0 + (a - 1.0) * k_a)
    bonus = _dot_exact_rhs(r * k * r_k, bd) * v

    r_ref[...] = r
    lw_ref[...] = lw
    k_ref[...] = k
    v_ref[...] = v
    a_ref[...] = -kk
    b_ref[...] = kk * a
    g_ref[...] = g
    bonus_ref[...] = bonus


def _inproj(x, shift_state, wts, v_first, seq):
    n = x.shape[0]
    first = v_first is None
    tm = ROW_TILE
    row = lambda w: pl.BlockSpec((tm, w), lambda i: (i, 0))
    if seq >= tm:
        assert seq % tm == 0
        st = shift_state.reshape(-1, 1, D_MODEL)
        st_spec = pl.BlockSpec((1, 1, D_MODEL), lambda i: (i // (seq // tm), 0, 0))
    else:
        assert tm % seq == 0
        st = shift_state
        st_spec = pl.BlockSpec((tm // seq, D_MODEL), lambda i: (i, 0))
    ins = [x, st, wts["watt"], wts["wrw"], wts["mu"], wts["wda"], wts["g2"], wts["vecA"], wts["bd"]]
    in_specs = [row(D_MODEL), st_spec] + [_const_spec(a.shape) for a in ins[2:]]
    if not first:
        ins += [v_first, wts["v2"]]
        in_specs += [row(RWKV_WIDTH), _const_spec(wts["v2"].shape)]
    widths = [ATT_WIDTH, KV_WIDTH, KV_WIDTH] + [RWKV_WIDTH] * 8
    return pl.pallas_call(
        functools.partial(_inproj_kernel, first=first, seq=seq),
        grid=(n // tm,),
        in_specs=in_specs,
        out_specs=[row(w) for w in widths],
        out_shape=[jax.ShapeDtypeStruct((n, w), F32) for w in widths],
        scratch_shapes=[pltpu.VMEM((SUBLANES, wts["wrw"].shape[1]), F32)],
        compiler_params=_params("arbitrary"),
        name="inproj",
    )(*ins)


def _dot3s(a_sp, b_sp, ca=1, cb=0):
    (ah, al), (bh, bl) = a_sp, b_sp
    return _dg(ah, bh, ca, cb) + (_dg(ah, bl, ca, cb) + _dg(al, bh, ca, cb))


def _dot1s(a_sp, b_sp, ca=1, cb=0):
    return _dg(a_sp[0], b_sp[0], ca, cb)


def _wkv_kernel(*refs, group, seq_chunk, has_state):
    if has_state:
        r_ref, lw_ref, k_ref, v_ref, a_ref, b_ref, s0_ref, y_ref, sout_ref, s_scr = refs
    else:
        r_ref, lw_ref, k_ref, v_ref, a_ref, b_ref, y_ref, sout_ref, s_scr = refs
    G, Ls = group, seq_chunk
    L = G * Ls
    H, N = N_RWKV_HEADS, HEAD_DIM
    c_idx = pl.program_id(1)

    @pl.when(c_idx == 0)
    def _init():
        if has_state:
            s_scr[...] = s0_ref[...]
        else:
            s_scr[...] = jnp.zeros_like(s_scr)

    row = lax.broadcasted_iota(jnp.int32, (L, L), 0)
    col = lax.broadcasted_iota(jnp.int32, (L, L), 1)
    same = (row // Ls) == (col // Ls)
    incl = same & (row >= col)
    strict = same & (row > col)
    eye = (row == col).astype(F32)
    row2 = lax.broadcasted_iota(jnp.int32, (2 * L, 2 * L), 0)
    col2 = lax.broadcasted_iota(jnp.int32, (2 * L, 2 * L), 1)
    t_q = row2 % L
    t_k = col2 % L
    mask2 = ((t_q // Ls) == (t_k // Ls)) & ((t_q > t_k) | ((row2 >= L) & (t_q == t_k)))

    lw_all = lw_ref[...]
    cum_all = _dot_exact_lhs(incl.astype(BF16), lw_all)
    if G == 1:
        tot_all = jnp.broadcast_to(cum_all[L - 1:L, :], cum_all.shape)
    else:
        tot_all = _dot_exact_lhs(same.astype(BF16), lw_all)
    e_end = jnp.exp(tot_all - cum_all)
    etot_all = jnp.exp(tot_all)
    a_all = a_ref[...]
    r_all = r_ref[...]
    b_all = b_ref[...]
    k_all = k_ref[...]
    v_all = v_ref[...]
    ar_all = jnp.concatenate([a_all * jnp.exp(cum_all - lw_all), r_all * jnp.exp(cum_all)], axis=0)
    ar_sp = _split2(ar_all)
    if G == 1:
        mid = cum_all[L // 2 - 1:L // 2, :]
        e_neg = jnp.exp(mid - cum_all)
        arc_sp = _split2(jnp.concatenate([a_all * jnp.exp(cum_all - lw_all - mid), r_all * jnp.exp(cum_all - mid)],
                                         axis=0))
    else:
        e_neg = jnp.exp(-cum_all)
        arc_sp = ar_sp
    bk_sp = _split2(jnp.concatenate([b_all * e_neg, k_all * e_neg], axis=0))
    bkh_sp = _split2(jnp.concatenate([b_all * e_end, k_all * e_end], axis=0))
    heads = range(H)
    hsl = lambda t, h: t[:, h * N:(h + 1) * N]
    hsp = lambda sp, h: (hsl(sp[0], h), hsl(sp[1], h))

    m = [jnp.where(mask2, _dg(hsl(arc_sp[0], h), hsl(bk_sp[0], h), 1, 1), 0.0) for h in heads]
    m_sp = [_split2(t) for t in m]
    a_ab = [t[:L, :L] for t in m]
    d = [eye + jnp.where((row // 2) == (col // 2), t, 0.0) for t in a_ab]
    n = 4
    while n <= Ls:
        off = ((row // n) == (col // n)) & ((row // (n // 2)) != (col // (n // 2)))
        d_b = [t.astype(BF16) for t in d]
        dn = [_dg(t, jnp.where(off, s, 0.0).astype(BF16), 1, 0) for t, s in zip(d_b, a_ab)]
        d = [t + _dg(p.astype(BF16), tb, 1, 0) for t, p, tb in zip(d, dn, d_b)]
        n *= 2
    resid = [eye - t + _dot3s((s[0][:L, :L], s[1][:L, :L]), _split2(t)) for t, s in zip(d, m_sp)]
    t_inv = [t + _dg(t.astype(BF16), r.astype(BF16), 1, 0) for t, r in zip(d, resid)]

    s0 = [[s_scr[g, h] for h in heads] for g in range(G)]
    s0_sp = [[_split2(s0[g][h]) for h in heads] for g in range(G)]

    def seq_rows(sp, g, h):
        if G == 1:
            return hsp(sp, h)
        return tuple(jnp.concatenate([hsl(p, h)[g * Ls:(g + 1) * Ls], hsl(p, h)[L + g * Ls:L + (g + 1) * Ls]], axis=0)
                     for p in sp)

    st = [[_dot1s(seq_rows(ar_sp, g, h), s0_sp[g][h], 1, 1) for h in heads] for g in range(G)]
    rhs_st = [jnp.concatenate([st[g][h][:Ls] for g in range(G)], axis=0) for h in heads]
    y_st = [jnp.concatenate([st[g][h][Ls:] for g in range(G)], axis=0) for h in heads]

    v_h = [hsl(v_all, h) for h in heads]
    zeros = jnp.zeros((L, N), F32)
    m_top = [(s[0][:L], s[1][:L]) for s in m_sp]
    m_bot = [(s[0][L:], s[1][L:]) for s in m_sp]
    rhs = [rhs_st[h] + _dot1s(m_top[h], _split2(jnp.concatenate([zeros, v_h[h]], axis=0))) for h in heads]
    u = [_dot1s(_split2(t_inv[h]), _split2(rhs[h])) for h in heads]
    uv_sp = [_split2(jnp.concatenate([u[h], v_h[h]], axis=0)) for h in heads]
    y = [y_st[h] + _dg(m_bot[h][0], uv_sp[h][0], 1, 0) for h in heads]
    for g in range(G):
        for h in heads:
            if G == 1:
                uv_g = uv_sp[h]
            else:
                uv_g = tuple(jnp.concatenate([p[g * Ls:(g + 1) * Ls], p[L + g * Ls:L + (g + 1) * Ls]], axis=0)
                             for p in uv_sp[h])
            upd = _dot1s(uv_g, seq_rows(bkh_sp, g, h), 0, 0)
            s_scr[g, h] = s0[g][h] * hsl(etot_all, h)[g * Ls:g * Ls + 1] + upd
    ys = []
    for h in heads:
        mu = jnp.mean(y[h], axis=-1, keepdims=True)
        var = jnp.mean(jnp.square(y[h] - mu), axis=-1, keepdims=True)
        ys.append((y[h] - mu) * lax.rsqrt(var + GN_EPS))
    y_ref[...] = jnp.concatenate(ys, axis=1)

    @pl.when(c_idx == pl.num_programs(1) - 1)
    def _fin():
        sout_ref[...] = s_scr[...]


def _wkv_scan(r, lw, k, v, a, b, state, n_seq, seq, seq_chunk, group):
    n_chunks = seq // seq_chunk
    assert group == 1 or n_chunks == 1
    has_state = state is not None
    rows = group * seq_chunk
    tok_spec = pl.BlockSpec((rows, RWKV_WIDTH), lambda i, c: (i * n_chunks + c, 0))
    st_spec = pl.BlockSpec((group, N_RWKV_HEADS, HEAD_DIM, HEAD_DIM), lambda i, c: (i, 0, 0, 0))
    in_specs = [tok_spec] * 6
    args = (r, lw, k, v, a, b)
    if has_state:
        states, layer = state
        in_specs.append(pl.BlockSpec((None, group, N_RWKV_HEADS, HEAD_DIM, HEAD_DIM),
                                     lambda i, c: (layer, i, 0, 0, 0)))
        args += (states,)
    return pl.pallas_call(
        functools.partial(_wkv_kernel, group=group, seq_chunk=seq_chunk, has_state=has_state),
        grid=(n_seq // group, n_chunks),
        in_specs=in_specs,
        out_specs=[tok_spec, st_spec],
        out_shape=[jax.ShapeDtypeStruct((n_seq * seq, RWKV_WIDTH), F32),
                   jax.ShapeDtypeStruct((n_seq, N_RWKV_HEADS, HEAD_DIM, HEAD_DIM), F32)],
        scratch_shapes=[pltpu.VMEM((group, N_RWKV_HEADS, HEAD_DIM, HEAD_DIM), F32)],
        compiler_params=_params("arbitrary", "arbitrary"),
        name="wkv_scan",
    )(*args)


def _sink_softmax(s, sink):
    m = sink
    for t in s:
        m = jnp.maximum(m, jnp.max(t, axis=-1, keepdims=True))
    es = [jnp.exp(t - m) for t in s]
    den = jnp.exp(sink - m)
    for e in es:
        den = den + jnp.sum(e, axis=-1, keepdims=True)
    return [e / den for e in es]


def _attn_prompt_kernel(q_ref, kc_ref, kp_ref, vc_ref, vp_ref, sink_ref, o_ref):
    L = WINDOW
    n = pl.program_id(1)
    q = q_ref[...].astype(BF16)
    kw = jnp.concatenate([kp_ref[...], kc_ref[...]], axis=0).astype(BF16)
    vw = jnp.concatenate([vp_ref[...], vc_ref[...]], axis=0).astype(BF16)
    qi = lax.broadcasted_iota(jnp.int32, (L, 2 * L), 0)
    kj = lax.broadcasted_iota(jnp.int32, (L, 2 * L), 1)
    diff = qi + L - kj
    mask = (diff >= 0) & (diff < WINDOW) & ((kj >= L) | (n > 0))
    lane = lax.broadcasted_iota(jnp.int32, vw.shape, 1)
    one = jnp.ones_like(vw)
    vext = [jnp.where(lane < HEAD_DIM, vw, one), jnp.where(lane >= HEAD_DIM, vw, one)]
    heads = range(N_Q_HEADS)
    hsl = lambda t, h: t[:, h * HEAD_DIM:(h + 1) * HEAD_DIM]
    ss = [jnp.where(mask, _dg(hsl(q, h), hsl(kw, h // Q_PER_KV), 1, 1) * ATT_SCALE, -jnp.inf) for h in heads]
    sinks = [sink_ref[h:h + 1, 0:1] for h in heads]
    ms = [jnp.maximum(jnp.max(ss[h], axis=-1, keepdims=True), sinks[h]) for h in heads]
    es = [jnp.exp(ss[h] - ms[h]).astype(BF16) for h in heads]
    oes = [_dg(es[h], vext[h // Q_PER_KV], 1, 0) for h in heads]
    outs = []
    for h in heads:
        hk = h // Q_PER_KV
        rs = oes[h][:, (1 - hk) * HEAD_DIM:(1 - hk) * HEAD_DIM + 1]
        outs.append(hsl(oes[h], hk) / (rs + jnp.exp(sinks[h] - ms[h])))
    o_ref[...] = jnp.concatenate(outs, axis=1)


def _attn_prompt(q, k, v, sink_rows, n_batch, seq):
    nb = seq // WINDOW
    cur = lambda w: pl.BlockSpec((WINDOW, w), lambda b, n: (b * nb + n, 0))
    prev = lambda w: pl.BlockSpec((WINDOW, w), lambda b, n: (b * nb + jnp.maximum(n - 1, 0), 0))
    return pl.pallas_call(
        _attn_prompt_kernel,
        grid=(n_batch, nb),
        in_specs=[cur(ATT_WIDTH), cur(KV_WIDTH), prev(KV_WIDTH), cur(KV_WIDTH), prev(KV_WIDTH),
                  _const_spec(sink_rows.shape)],
        out_specs=cur(ATT_WIDTH),
        out_shape=jax.ShapeDtypeStruct((n_batch * seq, ATT_WIDTH), F32),
        compiler_params=_params("arbitrary", "arbitrary"),
        name="attn_prompt",
    )(q, k, k, v, v, sink_rows)


def _attn_sample_kernel(q_ref, kn_ref, vn_ref, ck_ref, cv_ref, sink_ref, o_ref, *, seq):
    S = seq
    W = WINDOW
    B = SAMPLE_ATT_BATCH
    R = Q_PER_KV * S
    tq1 = lax.broadcasted_iota(jnp.int32, (R, W), 0) % S
    kj1 = lax.broadcasted_iota(jnp.int32, (R, W), 1)
    mask_cache = kj1 > tq1
    tq2 = lax.broadcasted_iota(jnp.int32, (R, S), 0) % S
    kj2 = lax.broadcasted_iota(jnp.int32, (R, S), 1)
    mask_new = kj2 <= tq2
    kvs = range(N_KV_HEADS)
    hsl = lambda t, h: t[:, h * HEAD_DIM:(h + 1) * HEAD_DIM]
    group = lambda hk: range(hk * Q_PER_KV, (hk + 1) * Q_PER_KV)
    seq_rows = lambda t, b: t[b * S:(b + 1) * S]
    q = q_ref[...]
    kn = kn_ref[...]
    vn = vn_ref[...]
    qh = [hsl(q, h) for h in range(N_Q_HEADS)]
    knh = [hsl(kn, hk) for hk in kvs]
    vnh = [hsl(vn, hk) for hk in kvs]
    sinks = [jnp.concatenate([jnp.broadcast_to(sink_ref[h:h + 1, 0:1], (S, 1)) for h in group(hk)], axis=0)
             for hk in kvs]
    items = [(b, hk) for b in range(B) for hk in kvs]
    qs = [jnp.concatenate([seq_rows(qh[h], b) for h in group(hk)], axis=0).astype(BF16) for b, hk in items]
    ck = [ck_ref[b].astype(BF16) for b in range(B)]
    cv = [cv_ref[b].astype(BF16) for b in range(B)]
    s1 = [jnp.where(mask_cache, _dg(qs[i], hsl(ck[b], hk), 1, 1) * ATT_SCALE, -jnp.inf)
          for i, (b, hk) in enumerate(items)]
    s2 = [jnp.where(mask_new, _dg(qs[i], seq_rows(knh[hk], b).astype(BF16), 1, 1) * ATT_SCALE, -jnp.inf)
          for i, (b, hk) in enumerate(items)]
    ps = [_sink_softmax([s1[i], s2[i]], sinks[hk]) for i, (b, hk) in enumerate(items)]
    outs = [_dg(ps[i][0].astype(BF16), hsl(cv[b], hk), 1, 0)
            + _dg(ps[i][1].astype(BF16), seq_rows(vnh[hk], b).astype(BF16), 1, 0)
            for i, (b, hk) in enumerate(items)]
    rows = [jnp.concatenate([outs[b * N_KV_HEADS + hk][g * S:(g + 1) * S] for hk in kvs for g in range(Q_PER_KV)],
                            axis=1) for b in range(B)]
    o_ref[...] = jnp.concatenate(rows, axis=0)


def _attn_sample(q, k, v, cache_k, cache_v, sink_rows, n_batch, seq):
    bb = SAMPLE_ATT_BATCH
    tok = lambda w: pl.BlockSpec((bb * seq, w), lambda i: (i, 0))
    cache = pl.BlockSpec((bb, WINDOW, KV_WIDTH), lambda i: (i, 0, 0))
    return pl.pallas_call(
        functools.partial(_attn_sample_kernel, seq=seq),
        grid=(n_batch // bb,),
        in_specs=[tok(ATT_WIDTH), tok(KV_WIDTH), tok(KV_WIDTH), cache, cache, _const_spec(sink_rows.shape)],
        out_specs=tok(ATT_WIDTH),
        out_shape=jax.ShapeDtypeStruct((n_batch * seq, ATT_WIDTH), F32),
        compiler_params=_params("arbitrary"),
        name="attn_sample",
    )(q, k, v, cache_k, cache_v, sink_rows)


def _second_max4(a, b, c, d):
    return jnp.maximum(jnp.maximum(jnp.minimum(a, b), jnp.minimum(c, d)),
                       jnp.minimum(jnp.maximum(a, b), jnp.maximum(c, d)))


def _route(logits_t, bias_col):
    G, E = N_EXPERT_GROUPS, EXPERTS_PER_GROUP
    m = jnp.max(logits_t, axis=0, keepdims=True)
    ex = jnp.exp(logits_t - m)
    probs = ex / jnp.sum(ex, axis=0, keepdims=True)
    sel = probs + bias_col
    p = [probs[e:e + 1, :] for e in range(N_EXPERTS)]
    s = [sel[e:e + 1, :] for e in range(N_EXPERTS)]
    gs = []
    for g in range(G):
        a, b, c, d = s[E * g:E * g + E]
        top1 = jnp.maximum(jnp.maximum(a, b), jnp.maximum(c, d))
        gs.append(top1 + _second_max4(a, b, c, d))
    best = jnp.zeros_like(gs[0], dtype=jnp.int32)
    best_s = gs[0]
    for g in range(1, G):
        upd = gs[g] > best_s
        best = jnp.where(upd, g, best)
        best_s = jnp.where(upd, gs[g], best_s)

    def pick(vals, j):
        out = vals[j]
        for g in range(1, G):
            out = jnp.where(best == g, vals[E * g + j], out)
        return out

    ig = [pick(s, j) for j in range(E)]
    pg = [pick(p, j) for j in range(E)]
    l1 = jnp.zeros_like(best)
    v1 = ig[0]
    for j in range(1, E):
        upd = ig[j] > v1
        l1 = jnp.where(upd, j, l1)
        v1 = jnp.where(upd, ig[j], v1)
    l2 = jnp.full_like(best, -1)
    v2 = jnp.full_like(v1, -jnp.inf)
    for j in range(E):
        upd = (l1 != j) & (ig[j] > v2)
        l2 = jnp.where(upd, j, l2)
        v2 = jnp.where(upd, ig[j], v2)
    zero = jnp.zeros_like(v1)
    w1 = zero
    w2 = zero
    for j in range(E):
        w1 = jnp.where(l1 == j, pg[j], w1)
        w2 = jnp.where(l2 == j, pg[j], w2)
    wsum = w1 + w2
    w1 = w1 / wsum
    w2 = w2 / wsum
    rows = []
    for e in range(N_EXPERTS):
        g, j = divmod(e, E)
        in_g = best == g
        rows.append(jnp.where(in_g & (l1 == j), w1, zero) + jnp.where(in_g & (l2 == j), w2, zero))
    return jnp.concatenate(rows, axis=0)


def _post_kernel(yn_ref, bonus_ref, g_ref, att_ref, x_ref, woa_ref, wor_ref, gn_ref, ln_ref,
                 wrt_ref, rb_ref, x1_ref, gates_ref):
    rw_out = (yn_ref[...] * gn_ref[0:1, :] + gn_ref[1:2, :] + bonus_ref[...]) * g_ref[...]
    mixed = _bdot(att_ref[...], woa_ref[...]) + _bdot(rw_out, wor_ref[...])
    x1 = _layer_norm(ALPHA * x_ref[...] + mixed, ln_ref[0:1, :], ln_ref[1:2, :])
    x1_ref[...] = x1
    logits_t = _dot3(wrt_ref[...], x1, 1, 1)
    gates_t = _route(logits_t, rb_ref[:, 0:1])
    pad = jnp.zeros((LANES - N_EXPERTS, gates_t.shape[1]), F32)
    gates_ref[...] = jnp.concatenate([gates_t, pad], axis=0).T


def _post(yn, bonus, g, att, x, wts, glob):
    n = x.shape[0]
    tm = ROW_TILE
    row = lambda w: pl.BlockSpec((tm, w), lambda i: (i, 0))
    consts = [wts["woa"], wts["wor"], wts["gn"], wts["ln1"], glob["wrt"], glob["rb"]]
    return pl.pallas_call(
        _post_kernel,
        grid=(n // tm,),
        in_specs=[row(RWKV_WIDTH)] * 3 + [row(ATT_WIDTH), row(D_MODEL)] + [_const_spec(a.shape) for a in consts],
        out_specs=[row(D_MODEL), row(LANES)],
        out_shape=[jax.ShapeDtypeStruct((n, D_MODEL), F32), jax.ShapeDtypeStruct((n, LANES), F32)],
        compiler_params=_params("arbitrary"),
        name="post_mix",
    )(yn, bonus, g, att, x, *consts)


def _moe_kernel(x_ref, gates_ref, wg_ref, wu_ref, wd_ref, ln_ref, o_ref, acc_ref, xb_ref):
    e = pl.program_id(1)

    @pl.when(e == 0)
    def _init():
        acc_ref[...] = jnp.zeros_like(acc_ref)
        xb_ref[...] = x_ref[...].astype(BF16)

    xb = xb_ref[...]
    gates = gates_ref[...]
    lane = lax.broadcasted_iota(jnp.int32, gates.shape, 1)
    gcol = jnp.sum(jnp.where(lane == e, gates, 0.0), axis=1, keepdims=True)
    hg = _dg(xb, wg_ref[0, 0], 1, 0)
    hu = _dg(xb, wu_ref[0, 0], 1, 0)
    h = jax.nn.silu(hg) * hu * gcol
    acc_ref[...] += _dg(h.astype(BF16), wd_ref[0, 0], 1, 0)

    @pl.when(e == pl.num_programs(1) - 1)
    def _fin():
        o_ref[...] = _layer_norm(ALPHA * x_ref[...] + acc_ref[...], ln_ref[0:1, :], ln_ref[1:2, :])


def _moe(x1, gates, wts, glob, l):
    n = x1.shape[0]
    tm = MOE_ROW_TILE
    return pl.pallas_call(
        _moe_kernel,
        grid=(n // tm, N_EXPERTS),
        in_specs=[pl.BlockSpec((tm, D_MODEL), lambda i, e: (i, 0)),
                  pl.BlockSpec((tm, LANES), lambda i, e: (i, 0)),
                  pl.BlockSpec((1, 1, D_MODEL, EXPERT_FF), lambda i, e: (l, e, 0, 0)),
                  pl.BlockSpec((1, 1, D_MODEL, EXPERT_FF), lambda i, e: (l, e, 0, 0)),
                  pl.BlockSpec((1, 1, EXPERT_FF, D_MODEL), lambda i, e: (l, e, 0, 0)),
                  _const_spec(wts["ln2"].shape)],
        out_specs=pl.BlockSpec((tm, D_MODEL), lambda i, e: (i, 0)),
        out_shape=jax.ShapeDtypeStruct((n, D_MODEL), F32),
        scratch_shapes=[pltpu.VMEM((tm, D_MODEL), F32), pltpu.VMEM((tm, D_MODEL), BF16)],
        compiler_params=_params("arbitrary", "arbitrary"),
        name="moe",
    )(x1, gates, glob["wg"], glob["wu"], glob["wd"], wts["ln2"])


def _rows8(vectors, width):
    rows = [v.reshape(1, width).astype(F32) for v in vectors]
    rows.append(jnp.zeros((8 - len(rows), width), F32))
    return jnp.concatenate(rows, axis=0)


def _prep_layer(l, p):
    W = RWKV_WIDTH
    w_l = p["w_in"][l]
    rwc = w_l[:, N_ATT_COLS:]
    mu = p["mu_rwkv"][l]
    o_w, o_k, o_v, o_a, o_g = W, W + D_DECAY_LORA, 2 * W + D_DECAY_LORA, 3 * W + D_DECAY_LORA, 3 * W + 2 * D_DECAY_LORA

    def reorder(t):
        parts = [t[..., 0:W], t[..., o_k:o_k + W], t[..., o_v:o_v + W], t[..., o_g:o_g + D_GATE_LORA],
                 t[..., o_w:o_w + D_DECAY_LORA], t[..., o_a:o_a + D_AAA_LORA]]
        return parts

    w_parts = reorder(rwc)
    mu_parts = reorder(mu)
    if l > 0:
        padw = LANES - D_MV_LORA
        w_parts += [p["w_vres_in"][l - 1], jnp.zeros((D_MODEL, padw), F32)]
        mu_parts += [p["mu_vres"][l - 1], jnp.zeros((padw,), F32)]
    wrw = jnp.concatenate(w_parts, axis=1).astype(BF16)
    mu_row = jnp.concatenate(mu_parts).reshape(1, -1)
    zero = jnp.zeros((D_DECAY_LORA, W), F32)
    wda = jnp.concatenate([jnp.concatenate([p["decay_w2"][l], zero], axis=1),
                           jnp.concatenate([zero, p["aaa_a2"][l]], axis=1)], axis=0).astype(BF16)
    vecs = [p["decay_w0"][l], p["aaa_a0"][l], p["k_k"][l], p["k_a"][l], p["r_k"][l].reshape(W)]
    out = {
        "watt": w_l[:, :N_ATT_COLS].astype(BF16),
        "wrw": wrw,
        "mu": mu_row,
        "wda": wda,
        "g2": p["gate_g2"][l].astype(BF16),
        "woa": p["w_o"][l][:ATT_WIDTH].astype(BF16),
        "wor": p["w_o"][l][ATT_WIDTH:].astype(BF16),
        "gn": _rows8([p["gn_g"][l], p["gn_b"][l]], W),
        "ln1": _rows8([p["ln1_g"][l], p["ln1_b"][l]], D_MODEL),
        "ln2": _rows8([p["ln2_g"][l], p["ln2_b"][l]], D_MODEL),
        "sink_rows": jnp.broadcast_to(p["sinks"][l].reshape(N_Q_HEADS, 1), (N_Q_HEADS, LANES)).astype(F32),
    }
    if l > 0:
        vecs.append(p["vres_v0"][l - 1])
        out["v2"] = jnp.concatenate([p["vres_v2"][l - 1], jnp.zeros((LANES - D_MV_LORA, W), F32)], axis=0).astype(BF16)
    out["vecA"] = _rows8(vecs, W)
    hid = jnp.arange(W) // HEAD_DIM
    out["bd"] = (hid[:, None] == hid[None, :]).astype(BF16)
    return out


def _trunk(x3, shift_prev, cache_k, cache_v, wkv_prev, layer_wts, glob):
    decode = cache_k is not None
    n_batch, seq, _ = x3.shape
    x = x3.reshape(n_batch * seq, D_MODEL)
    new_k, new_v, new_wkv, new_shift = [], [], [], []
    v_first = None
    for l in range(DEPTH):
        wts = layer_wts[l]
        new_shift.append(x.reshape(n_batch, seq, D_MODEL)[:, -1])
        q, ka, va, r, lw, k, v, a, b, g, bonus = _inproj(x, shift_prev[l], wts, v_first, seq)
        if l == 0:
            v_first = v
        if decode:
            ck = cache_k[l].reshape(n_batch, WINDOW, KV_WIDTH)
            cv = cache_v[l].reshape(n_batch, WINDOW, KV_WIDTH)
            att = _attn_sample(q, ka, va, ck, cv, wts["sink_rows"], n_batch, seq)
            new_k.append(ka.reshape(n_batch, seq, N_KV_HEADS, HEAD_DIM))
            new_v.append(va.reshape(n_batch, seq, N_KV_HEADS, HEAD_DIM))
            yn, s_out = _wkv_scan(r, lw, k, v, a, b, (wkv_prev, l), n_batch, seq, seq, SAMPLE_WKV_GROUP)
        else:
            att = _attn_prompt(q, ka, va, wts["sink_rows"], n_batch, seq)
            last = lambda t: t.reshape(n_batch, seq, KV_WIDTH)[:, -WINDOW:].reshape(n_batch, WINDOW, N_KV_HEADS, HEAD_DIM)
            new_k.append(last(ka))
            new_v.append(last(va))
            yn, s_out = _wkv_scan(r, lw, k, v, a, b, None, n_batch, seq, WKV_CHUNK, 1)
        new_wkv.append(s_out)
        x1, gates = _post(yn, bonus, g, att, x, wts, glob)
        x = _moe(x1, gates, wts, glob, l)
    return (x.reshape(n_batch, seq, D_MODEL), jnp.stack(new_k), jnp.stack(new_v), jnp.stack(new_wkv),
            jnp.stack(new_shift))


def kernel(x_prompt, x_sample, cache_k, cache_v, state_wkv, state_shift, w_in, w_vres_in, mu_rwkv, mu_vres, sinks, decay_w0, decay_w2, aaa_a0, aaa_a2, vres_v0, vres_v2, gate_g2, k_k, k_a, r_k, gn_g, gn_b, w_o, ln1_g, ln1_b, w_router, router_bias, w_gate, w_up, w_down, ln2_g, ln2_b):
    p = dict(w_in=w_in, w_vres_in=w_vres_in, mu_rwkv=mu_rwkv, mu_vres=mu_vres, sinks=sinks,
             decay_w0=decay_w0, decay_w2=decay_w2, aaa_a0=aaa_a0, aaa_a2=aaa_a2,
             vres_v0=vres_v0, vres_v2=vres_v2, gate_g2=gate_g2, k_k=k_k, k_a=k_a, r_k=r_k,
             gn_g=gn_g, gn_b=gn_b, w_o=w_o, ln1_g=ln1_g, ln1_b=ln1_b,
             w_gate=w_gate, w_up=w_up, w_down=w_down, ln2_g=ln2_g, ln2_b=ln2_b)
    layer_wts = [_prep_layer(l, p) for l in range(DEPTH)]
    glob = {
        "wrt": w_router.T.astype(F32),
        "rb": jnp.broadcast_to(router_bias.reshape(N_EXPERTS, 1), (N_EXPERTS, LANES)).astype(F32),
        "wg": w_gate.astype(BF16),
        "wu": w_up.astype(BF16),
        "wd": w_down.astype(BF16),
    }
    b_p = x_prompt.shape[0]
    zero_shift = jnp.zeros((DEPTH, b_p, D_MODEL), x_prompt.dtype)
    y_p, k_p, v_p, wkv_p, shift_p = _trunk(x_prompt, zero_shift, None, None, None, layer_wts, glob)
    y_s, k_s, v_s, wkv_s, shift_s = _trunk(x_sample, state_shift, cache_k, cache_v, state_wkv, layer_wts, glob)
    return (y_p, y_s, k_p, v_p, wkv_p, shift_p, k_s, v_s, wkv_s, shift_s)
```

```python
import functools
import math

import jax
import jax.numpy as jnp
from jax import lax
from jax.experimental import pallas as pl
from jax.experimental.pallas import tpu as pltpu

F32 = jnp.float32
BF16 = jnp.bfloat16

D_MODEL = 1024
DEPTH = 4
HEAD_DIM = 64
ATT_WIDTH = 512
RWKV_WIDTH = 512
N_Q_HEADS = 8
N_KV_HEADS = 2
Q_PER_KV = 4
KV_WIDTH = 128
N_ATT_COLS = ATT_WIDTH + 2 * KV_WIDTH
WINDOW = 128
ATT_SCALE = HEAD_DIM ** -0.5
N_RWKV_HEADS = 8
D_DECAY_LORA = 64
D_AAA_LORA = 64
D_GATE_LORA = 128
D_MV_LORA = 32
DECAY_SCALE = math.exp(-0.5)
GN_EPS = 64e-5
LN_EPS = 1e-5
N_EXPERTS = 16
N_EXPERT_GROUPS = 4
EXPERTS_PER_GROUP = 4
EXPERT_FF = 512
ALPHA = (2 * DEPTH) ** 0.25

LANES = 128
SUBLANES = 8
VMEM_LIMIT_BYTES = 56 * 1024 * 1024
ROW_TILE = 256
MOE_BLOCK = 256
MOE_POS_TILE = 512
MOE_DMA_TILE = 256
N_EXPERTS_PER_TOKEN = 2
WKV_CHUNK = 128
SAMPLE_ATT_BATCH = 8
SAMPLE_WKV_GROUP = 8


def _dg(a, b, ca, cb):
    return lax.dot_general(a, b, (((ca,), (cb,)), ((), ())), preferred_element_type=F32)


def _bdot(a, b):
    return _dg(a.astype(BF16), b.astype(BF16), 1, 0)


def _split2(x):
    hi = x.astype(BF16)
    lo = (x - hi.astype(F32)).astype(BF16)
    return hi, lo


def _dot3(a, b, ca=1, cb=0):
    ah, al = _split2(a)
    bh, bl = _split2(b)
    return _dg(ah, bh, ca, cb) + (_dg(ah, bl, ca, cb) + _dg(al, bh, ca, cb))


def _dot_exact_lhs(m_bf16, x, parts=3):
    acc = None
    rem = x
    for _ in range(parts):
        p = rem.astype(BF16)
        t = _dg(m_bf16, p, 1, 0)
        acc = t if acc is None else acc + t
        rem = rem - p.astype(F32)
    return acc


def _dot_exact_rhs(x, m_bf16, parts=3):
    acc = None
    rem = x
    for _ in range(parts):
        p = rem.astype(BF16)
        t = _dg(p, m_bf16, 1, 0)
        acc = t if acc is None else acc + t
        rem = rem - p.astype(F32)
    return acc


def _layer_norm(z, g, b):
    mu = jnp.mean(z, axis=-1, keepdims=True)
    var = jnp.mean(jnp.square(z - mu), axis=-1, keepdims=True)
    return (z - mu) * lax.rsqrt(var + LN_EPS) * g + b


def _params(*sem):
    return pltpu.CompilerParams(dimension_semantics=sem, vmem_limit_bytes=VMEM_LIMIT_BYTES)


def _const_spec(shape):
    nd = len(shape)
    return pl.BlockSpec(shape, lambda *_: (0,) * nd)


def _inproj_kernel(*refs, first, seq):
    if first:
        (x_ref, st_ref, watt_ref, wrw_ref, mu_ref, wda_ref, g2_ref, vec_ref, bd_ref,
         q_ref, ka_ref, va_ref, r_ref, lw_ref, k_ref, v_ref, a_ref, b_ref, g_ref, bonus_ref, carry_ref) = refs
    else:
        (x_ref, st_ref, watt_ref, wrw_ref, mu_ref, wda_ref, g2_ref, vec_ref, bd_ref, vfirst_ref, v2_ref,
         q_ref, ka_ref, va_ref, r_ref, lw_ref, k_ref, v_ref, a_ref, b_ref, g_ref, bonus_ref, carry_ref) = refs
    i = pl.program_id(0)
    x = x_ref[...]
    tm = x.shape[0]
    qkv = _dg(x.astype(BF16), watt_ref[...], 1, 0)
    q_ref[...] = qkv[:, :ATT_WIDTH]
    ka_ref[...] = qkv[:, ATT_WIDTH:ATT_WIDTH + KV_WIDTH]
    va_ref[...] = qkv[:, ATT_WIDTH + KV_WIDTH:]

    whole_tiles = seq >= tm
    st = jnp.broadcast_to(st_ref[0], (2 * SUBLANES, D_MODEL)) if whole_tiles else st_ref[...]
    pe = _dg(jnp.concatenate([x, st], axis=0).astype(BF16), wrw_ref[...], 1, 0)
    pc = pe[:tm]
    pst = pe[tm:]
    rowid = lax.broadcasted_iota(jnp.int32, pc.shape, 0)
    pp = pltpu.roll(pc, 1, 0)
    if whole_tiles:
        @pl.when(i == 0)
        def _init():
            carry_ref[...] = jnp.zeros_like(carry_ref)

        prev0 = jnp.where(i % (seq // tm) == 0, pst[0:1], carry_ref[SUBLANES - 1:SUBLANES, :])
        pp = jnp.where(rowid == 0, prev0, pp)
        carry_ref[...] = pc[tm - SUBLANES:]
    else:
        n_st = tm // seq
        er = lax.broadcasted_iota(jnp.int32, (tm, n_st), 0)
        ec = lax.broadcasted_iota(jnp.int32, (tm, n_st), 1)
        expand = (er == ec * seq).astype(BF16)
        pp = jnp.where(rowid % seq == 0, _dot_exact_lhs(expand, pst), pp)
    rw = pc + (pp - pc) * mu_ref[...]
    W = RWKV_WIDTH
    r = rw[:, 0:W]
    k = rw[:, W:2 * W]
    v = rw[:, 2 * W:3 * W]
    gl = rw[:, 3 * W:3 * W + D_GATE_LORA]
    wa = rw[:, 3 * W + D_GATE_LORA:3 * W + 2 * LANES]
    w0 = vec_ref[0:1, :]
    a0 = vec_ref[1:2, :]
    k_k = vec_ref[2:3, :]
    k_a = vec_ref[3:4, :]
    r_k = vec_ref[4:5, :]

    lane = lax.broadcasted_iota(jnp.int32, wa.shape, 1)
    wa_t = jnp.where(lane < D_DECAY_LORA, jnp.tanh(wa), wa)
    da = _bdot(wa_t, wda_ref[...])
    lw = -DECAY_SCALE * jax.nn.sigmoid(w0 + da[:, :W])
    a = jax.nn.sigmoid(a0 + da[:, W:])
    g = _bdot(jax.nn.sigmoid(gl), g2_ref[...])
    if not first:
        mv = rw[:, 3 * W + 2 * LANES:]
        v0 = vec_ref[5:6, :]
        v = v + (vfirst_ref[...] - v) * jax.nn.sigmoid(v0 + _bdot(mv, v2_ref[...]))
    bd = bd_ref[...]
    kk = k * k_k
    ssq = _dot_exact_rhs(kk * kk, bd)
    kk = kk / jnp.maximum(jnp.sqrt(ssq), 1e-12)
    k = k * (1.0 + (a - 1.0) * k_a)
    bonus = _dot_exact_rhs(r * k * r_k, bd) * v

    r_ref[...] = r
    lw_ref[...] = lw
    k_ref[...] = k
    v_ref[...] = v
    a_ref[...] = -kk
    b_ref[...] = kk * a
    g_ref[...] = g
    bonus_ref[...] = bonus


def _inproj(x, shift_state, wts, v_first, seq):
    n = x.shape[0]
    first = v_first is None
    tm = ROW_TILE
    row = lambda w: pl.BlockSpec((tm, w), lambda i: (i, 0))
    if seq >= tm:
        assert seq % tm == 0
        st = shift_state.reshape(-1, 1, D_MODEL)
        st_spec = pl.BlockSpec((1, 1, D_MODEL), lambda i: (i // (seq // tm), 0, 0))
    else:
        assert tm % seq == 0
        st = shift_state
        st_spec = pl.BlockSpec((tm // seq, D_MODEL), lambda i: (i, 0))
    ins = [x, st, wts["watt"], wts["wrw"], wts["mu"], wts["wda"], wts["g2"], wts["vecA"], wts["bd"]]
    in_specs = [row(D_MODEL), st_spec] + [_const_spec(a.shape) for a in ins[2:]]
    if not first:
        ins += [v_first, wts["v2"]]
        in_specs += [row(RWKV_WIDTH), _const_spec(wts["v2"].shape)]
    widths = [ATT_WIDTH, KV_WIDTH, KV_WIDTH] + [RWKV_WIDTH] * 8
    return pl.pallas_call(
        functools.partial(_inproj_kernel, first=first, seq=seq),
        grid=(n // tm,),
        in_specs=in_specs,
        out_specs=[row(w) for w in widths],
        out_shape=[jax.ShapeDtypeStruct((n, w), F32) for w in widths],
        scratch_shapes=[pltpu.VMEM((SUBLANES, wts["wrw"].shape[1]), F32)],
        compiler_params=_params("arbitrary"),
        name="inproj",
    )(*ins)


def _dot3s(a_sp, b_sp, ca=1, cb=0):
    (ah, al), (bh, bl) = a_sp, b_sp
    return _dg(ah, bh, ca, cb) + (_dg(ah, bl, ca, cb) + _dg(al, bh, ca, cb))


def _dot1s(a_sp, b_sp, ca=1, cb=0):
    return _dg(a_sp[0], b_sp[0], ca, cb)


def _wkv_kernel(*refs, group, seq_chunk, has_state):
    if has_state:
        r_ref, lw_ref, k_ref, v_ref, a_ref, b_ref, s0_ref, y_ref, sout_ref, s_scr = refs
    else:
        r_ref, lw_ref, k_ref, v_ref, a_ref, b_ref, y_ref, sout_ref, s_scr = refs
    G, Ls = group, seq_chunk
    L = G * Ls
    H, N = N_RWKV_HEADS, HEAD_DIM
    c_idx = pl.program_id(1)

    @pl.when(c_idx == 0)
    def _init():
        if has_state:
            s_scr[...] = s0_ref[...]
        else:
            s_scr[...] = jnp.zeros_like(s_scr)

    row = lax.broadcasted_iota(jnp.int32, (L, L), 0)
    col = lax.broadcasted_iota(jnp.int32, (L, L), 1)
    same = (row // Ls) == (col // Ls)
    incl = same & (row >= col)
    strict = same & (row > col)
    eye = (row == col).astype(F32)
    row2 = lax.broadcasted_iota(jnp.int32, (2 * L, 2 * L), 0)
    col2 = lax.broadcasted_iota(jnp.int32, (2 * L, 2 * L), 1)
    t_q = row2 % L
    t_k = col2 % L
    mask2 = ((t_q // Ls) == (t_k // Ls)) & ((t_q > t_k) | ((row2 >= L) & (t_q == t_k)))

    lw_all = lw_ref[...]
    cum_all = _dot_exact_lhs(incl.astype(BF16), lw_all)
    if G == 1:
        tot_all = jnp.broadcast_to(cum_all[L - 1:L, :], cum_all.shape)
    else:
        tot_all = _dot_exact_lhs(same.astype(BF16), lw_all)
    e_end = jnp.exp(tot_all - cum_all)
    etot_all = jnp.exp(tot_all)
    a_all = a_ref[...]
    r_all = r_ref[...]
    b_all = b_ref[...]
    k_all = k_ref[...]
    v_all = v_ref[...]
    ar_all = jnp.concatenate([a_all * jnp.exp(cum_all - lw_all), r_all * jnp.exp(cum_all)], axis=0)
    ar_sp = _split2(ar_all)
    if G == 1:
        mid = cum_all[L // 2 - 1:L // 2, :]
        e_neg = jnp.exp(mid - cum_all)
        arc_sp = _split2(jnp.concatenate([a_all * jnp.exp(cum_all - lw_all - mid), r_all * jnp.exp(cum_all - mid)],
                                         axis=0))
    else:
        e_neg = jnp.exp(-cum_all)
        arc_sp = ar_sp
    bk_sp = _split2(jnp.concatenate([b_all * e_neg, k_all * e_neg], axis=0))
    bkh_sp = _split2(jnp.concatenate([b_all * e_end, k_all * e_end], axis=0))
    heads = range(H)
    hsl = lambda t, h: t[:, h * N:(h + 1) * N]
    hsp = lambda sp, h: (hsl(sp[0], h), hsl(sp[1], h))

    m = [jnp.where(mask2, _dg(hsl(arc_sp[0], h), hsl(bk_sp[0], h), 1, 1), 0.0) for h in heads]
    m_sp = [_split2(t) for t in m]
    a_ab = [t[:L, :L] for t in m]
    d = [eye + jnp.where((row // 2) == (col // 2), t, 0.0) for t in a_ab]
    n = 4
    while n <= Ls:
        off = ((row // n) == (col // n)) & ((row // (n // 2)) != (col // (n // 2)))
        d_b = [t.astype(BF16) for t in d]
        dn = [_dg(t, jnp.where(off, s, 0.0).astype(BF16), 1, 0) for t, s in zip(d_b, a_ab)]
        d = [t + _dg(p.astype(BF16), tb, 1, 0) for t, p, tb in zip(d, dn, d_b)]
        n *= 2
    resid = [eye - t + _dot3s((s[0][:L, :L], s[1][:L, :L]), _split2(t)) for t, s in zip(d, m_sp)]
    t_inv = [t + _dg(t.astype(BF16), r.astype(BF16), 1, 0) for t, r in zip(d, resid)]

    s0 = [[s_scr[g, h] for h in heads] for g in range(G)]
    s0_sp = [[_split2(s0[g][h]) for h in heads] for g in range(G)]

    def seq_rows(sp, g, h):
        if G == 1:
            return hsp(sp, h)
        return tuple(jnp.concatenate([hsl(p, h)[g * Ls:(g + 1) * Ls], hsl(p, h)[L + g * Ls:L + (g + 1) * Ls]], axis=0)
                     for p in sp)

    st = [[_dot1s(seq_rows(ar_sp, g, h), s0_sp[g][h], 1, 1) for h in heads] for g in range(G)]
    rhs_st = [jnp.concatenate([st[g][h][:Ls] for g in range(G)], axis=0) for h in heads]
    y_st = [jnp.concatenate([st[g][h][Ls:] for g in range(G)], axis=0) for h in heads]

    v_h = [hsl(v_all, h) for h in heads]
    zeros = jnp.zeros((L, N), F32)
    m_top = [(s[0][:L], s[1][:L]) for s in m_sp]
    m_bot = [(s[0][L:], s[1][L:]) for s in m_sp]
    rhs = [rhs_st[h] + _dot1s(m_top[h], _split2(jnp.concatenate([zeros, v_h[h]], axis=0))) for h in heads]
    u = [_dot1s(_split2(t_inv[h]), _split2(rhs[h])) for h in heads]
    uv_sp = [_split2(jnp.concatenate([u[h], v_h[h]], axis=0)) for h in heads]
    y = [y_st[h] + _dg(m_bot[h][0], uv_sp[h][0], 1, 0) for h in heads]
    for g in range(G):
        for h in heads:
            if G == 1:
                uv_g = uv_sp[h]
            else:
                uv_g = tuple(jnp.concatenate([p[g * Ls:(g + 1) * Ls], p[L + g * Ls:L + (g + 1) * Ls]], axis=0)
                             for p in uv_sp[h])
            upd = _dot1s(uv_g, seq_rows(bkh_sp, g, h), 0, 0)
            s_scr[g, h] = s0[g][h] * hsl(etot_all, h)[g * Ls:g * Ls + 1] + upd
    ys = []
    for h in heads:
        mu = jnp.mean(y[h], axis=-1, keepdims=True)
        var = jnp.mean(jnp.square(y[h] - mu), axis=-1, keepdims=True)
        ys.append((y[h] - mu) * lax.rsqrt(var + GN_EPS))
    y_ref[...] = jnp.concatenate(ys, axis=1)

    @pl.when(c_idx == pl.num_programs(1) - 1)
    def _fin():
        sout_ref[...] = s_scr[...]


def _wkv_scan(r, lw, k, v, a, b, state, n_seq, seq, seq_chunk, group):
    n_chunks = seq // seq_chunk
    assert group == 1 or n_chunks == 1
    has_state = state is not None
    rows = group * seq_chunk
    tok_spec = pl.BlockSpec((rows, RWKV_WIDTH), lambda i, c: (i * n_chunks + c, 0))
    st_spec = pl.BlockSpec((group, N_RWKV_HEADS, HEAD_DIM, HEAD_DIM), lambda i, c: (i, 0, 0, 0))
    in_specs = [tok_spec] * 6
    args = (r, lw, k, v, a, b)
    if has_state:
        states, layer = state
        in_specs.append(pl.BlockSpec((None, group, N_RWKV_HEADS, HEAD_DIM, HEAD_DIM),
                                     lambda i, c: (layer, i, 0, 0, 0)))
        args += (states,)
    return pl.pallas_call(
        functools.partial(_wkv_kernel, group=group, seq_chunk=seq_chunk, has_state=has_state),
        grid=(n_seq // group, n_chunks),
        in_specs=in_specs,
        out_specs=[tok_spec, st_spec],
        out_shape=[jax.ShapeDtypeStruct((n_seq * seq, RWKV_WIDTH), F32),
                   jax.ShapeDtypeStruct((n_seq, N_RWKV_HEADS, HEAD_DIM, HEAD_DIM), F32)],
        scratch_shapes=[pltpu.VMEM((group, N_RWKV_HEADS, HEAD_DIM, HEAD_DIM), F32)],
        compiler_params=_params("arbitrary", "arbitrary"),
        name="wkv_scan",
    )(*args)


def _sink_softmax(s, sink):
    m = sink
    for t in s:
        m = jnp.maximum(m, jnp.max(t, axis=-1, keepdims=True))
    es = [jnp.exp(t - m) for t in s]
    den = jnp.exp(sink - m)
    for e in es:
        den = den + jnp.sum(e, axis=-1, keepdims=True)
    return [e / den for e in es]


def _attn_prompt_kernel(q_ref, kc_ref, kp_ref, vc_ref, vp_ref, sink_ref, o_ref):
    L = WINDOW
    n = pl.program_id(1)
    q = q_ref[...].astype(BF16)
    kw = jnp.concatenate([kp_ref[...], kc_ref[...]], axis=0).astype(BF16)
    vw = jnp.concatenate([vp_ref[...], vc_ref[...]], axis=0).astype(BF16)
    qi = lax.broadcasted_iota(jnp.int32, (L, 2 * L), 0)
    kj = lax.broadcasted_iota(jnp.int32, (L, 2 * L), 1)
    diff = qi + L - kj
    mask = (diff >= 0) & (diff < WINDOW) & ((kj >= L) | (n > 0))
    lane = lax.broadcasted_iota(jnp.int32, vw.shape, 1)
    one = jnp.ones_like(vw)
    vext = [jnp.where(lane < HEAD_DIM, vw, one), jnp.where(lane >= HEAD_DIM, vw, one)]
    heads = range(N_Q_HEADS)
    hsl = lambda t, h: t[:, h * HEAD_DIM:(h + 1) * HEAD_DIM]
    ss = [jnp.where(mask, _dg(hsl(q, h), hsl(kw, h // Q_PER_KV), 1, 1) * ATT_SCALE, -jnp.inf) for h in heads]
    sinks = [sink_ref[h:h + 1, 0:1] for h in heads]
    ms = [jnp.maximum(jnp.max(ss[h], axis=-1, keepdims=True), sinks[h]) for h in heads]
    es = [jnp.exp(ss[h] - ms[h]).astype(BF16) for h in heads]
    oes = [_dg(es[h], vext[h // Q_PER_KV], 1, 0) for h in heads]
    outs = []
    for h in heads:
        hk = h // Q_PER_KV
        rs = oes[h][:, (1 - hk) * HEAD_DIM:(1 - hk) * HEAD_DIM + 1]
        outs.append(hsl(oes[h], hk) / (rs + jnp.exp(sinks[h] - ms[h])))
    o_ref[...] = jnp.concatenate(outs, axis=1)


def _attn_prompt(q, k, v, sink_rows, n_batch, seq):
    nb = seq // WINDOW
    cur = lambda w: pl.BlockSpec((WINDOW, w), lambda b, n: (b * nb + n, 0))
    prev = lambda w: pl.BlockSpec((WINDOW, w), lambda b, n: (b * nb + jnp.maximum(n - 1, 0), 0))
    return pl.pallas_call(
        _attn_prompt_kernel,
        grid=(n_batch, nb),
        in_specs=[cur(ATT_WIDTH), cur(KV_WIDTH), prev(KV_WIDTH), cur(KV_WIDTH), prev(KV_WIDTH),
                  _const_spec(sink_rows.shape)],
        out_specs=cur(ATT_WIDTH),
        out_shape=jax.ShapeDtypeStruct((n_batch * seq, ATT_WIDTH), F32),
        compiler_params=_params("arbitrary", "arbitrary"),
        name="attn_prompt",
    )(q, k, k, v, v, sink_rows)


def _attn_sample_kernel(q_ref, kn_ref, vn_ref, ck_ref, cv_ref, sink_ref, o_ref, *, seq):
    S = seq
    W = WINDOW
    B = SAMPLE_ATT_BATCH
    R = Q_PER_KV * S
    tq1 = lax.broadcasted_iota(jnp.int32, (R, W), 0) % S
    kj1 = lax.broadcasted_iota(jnp.int32, (R, W), 1)
    mask_cache = kj1 > tq1
    tq2 = lax.broadcasted_iota(jnp.int32, (R, S), 0) % S
    kj2 = lax.broadcasted_iota(jnp.int32, (R, S), 1)
    mask_new = kj2 <= tq2
    kvs = range(N_KV_HEADS)
    hsl = lambda t, h: t[:, h * HEAD_DIM:(h + 1) * HEAD_DIM]
    group = lambda hk: range(hk * Q_PER_KV, (hk + 1) * Q_PER_KV)
    seq_rows = lambda t, b: t[b * S:(b + 1) * S]
    q = q_ref[...]
    kn = kn_ref[...]
    vn = vn_ref[...]
    qh = [hsl(q, h) for h in range(N_Q_HEADS)]
    knh = [hsl(kn, hk) for hk in kvs]
    vnh = [hsl(vn, hk) for hk in kvs]
    sinks = [jnp.concatenate([jnp.broadcast_to(sink_ref[h:h + 1, 0:1], (S, 1)) for h in group(hk)], axis=0)
             for hk in kvs]
    items = [(b, hk) for b in range(B) for hk in kvs]
    qs = [jnp.concatenate([seq_rows(qh[h], b) for h in group(hk)], axis=0).astype(BF16) for b, hk in items]
    ck = [ck_ref[b].astype(BF16) for b in range(B)]
    cv = [cv_ref[b].astype(BF16) for b in range(B)]
    s1 = [jnp.where(mask_cache, _dg(qs[i], hsl(ck[b], hk), 1, 1) * ATT_SCALE, -jnp.inf)
          for i, (b, hk) in enumerate(items)]
    s2 = [jnp.where(mask_new, _dg(qs[i], seq_rows(knh[hk], b).astype(BF16), 1, 1) * ATT_SCALE, -jnp.inf)
          for i, (b, hk) in enumerate(items)]
    ps = [_sink_softmax([s1[i], s2[i]], sinks[hk]) for i, (b, hk) in enumerate(items)]
    outs = [_dg(ps[i][0].astype(BF16), hsl(cv[b], hk), 1, 0)
            + _dg(ps[i][1].astype(BF16), seq_rows(vnh[hk], b).astype(BF16), 1, 0)
            for i, (b, hk) in enumerate(items)]
    rows = [jnp.concatenate([outs[b * N_KV_HEADS + hk][g * S:(g + 1) * S] for hk in kvs for g in range(Q_PER_KV)],
                            axis=1) for b in range(B)]
    o_ref[...] = jnp.concatenate(rows, axis=0)


def _attn_sample(q, k, v, cache_k, cache_v, sink_rows, n_batch, seq):
    bb = SAMPLE_ATT_BATCH
    tok = lambda w: pl.BlockSpec((bb * seq, w), lambda i: (i, 0))
    cache = pl.BlockSpec((bb, WINDOW, KV_WIDTH), lambda i: (i, 0, 0))
    return pl.pallas_call(
        functools.partial(_attn_sample_kernel, seq=seq),
        grid=(n_batch // bb,),
        in_specs=[tok(ATT_WIDTH), tok(KV_WIDTH), tok(KV_WIDTH), cache, cache, _const_spec(sink_rows.shape)],
        out_specs=tok(ATT_WIDTH),
        out_shape=jax.ShapeDtypeStruct((n_batch * seq, ATT_WIDTH), F32),
        compiler_params=_params("arbitrary"),
        name="attn_sample",
    )(q, k, v, cache_k, cache_v, sink_rows)


def _second_max4(a, b, c, d):
    return jnp.maximum(jnp.maximum(jnp.minimum(a, b), jnp.minimum(c, d)),
                       jnp.minimum(jnp.maximum(a, b), jnp.maximum(c, d)))


def _route(logits_t, bias_col):
    G, E = N_EXPERT_GROUPS, EXPERTS_PER_GROUP
    m = jnp.max(logits_t, axis=0, keepdims=True)
    ex = jnp.exp(logits_t - m)
    probs = ex / jnp.sum(ex, axis=0, keepdims=True)
    sel = probs + bias_col
    p = [probs[e:e + 1, :] for e in range(N_EXPERTS)]
    s = [sel[e:e + 1, :] for e in range(N_EXPERTS)]
    gs = []
    for g in range(G):
        a, b, c, d = s[E * g:E * g + E]
        top1 = jnp.maximum(jnp.maximum(a, b), jnp.maximum(c, d))
        gs.append(top1 + _second_max4(a, b, c, d))
    best = jnp.zeros_like(gs[0], dtype=jnp.int32)
    best_s = gs[0]
    for g in range(1, G):
        upd = gs[g] > best_s
        best = jnp.where(upd, g, best)
        best_s = jnp.where(upd, gs[g], best_s)

    def pick(vals, j):
        out = vals[j]
        for g in range(1, G):
            out = jnp.where(best == g, vals[E * g + j], out)
        return out

    ig = [pick(s, j) for j in range(E)]
    pg = [pick(p, j) for j in range(E)]
    l1 = jnp.zeros_like(best)
    v1 = ig[0]
    for j in range(1, E):
        upd = ig[j] > v1
        l1 = jnp.where(upd, j, l1)
        v1 = jnp.where(upd, ig[j], v1)
    l2 = jnp.full_like(best, -1)
    v2 = jnp.full_like(v1, -jnp.inf)
    for j in range(E):
        upd = (l1 != j) & (ig[j] > v2)
        l2 = jnp.where(upd, j, l2)
        v2 = jnp.where(upd, ig[j], v2)
    zero = jnp.zeros_like(v1)
    w1 = zero
    w2 = zero
    for j in range(E):
        w1 = jnp.where(l1 == j, pg[j], w1)
        w2 = jnp.where(l2 == j, pg[j], w2)
    wsum = w1 + w2
    w1 = w1 / wsum
    w2 = w2 / wsum
    e1 = (best * E + l1).astype(F32)
    e2 = (best * E + l2).astype(F32)
    return jnp.concatenate([e1, e2, w1, w2, zero, zero, zero, zero], axis=0)


def _post_kernel(yn_ref, bonus_ref, g_ref, att_ref, x_ref, woa_ref, wor_ref, gn_ref, ln_ref,
                 wrt_ref, rb_ref, x1_ref, gates_ref):
    rw_out = (yn_ref[...] * gn_ref[0:1, :] + gn_ref[1:2, :] + bonus_ref[...]) * g_ref[...]
    mixed = _bdot(att_ref[...], woa_ref[...]) + _bdot(rw_out, wor_ref[...])
    x1 = _layer_norm(ALPHA * x_ref[...] + mixed, ln_ref[0:1, :], ln_ref[1:2, :])
    x1_ref[...] = x1
    logits_t = _dot3(wrt_ref[...], x1, 1, 1)
    route_t = _route(logits_t, rb_ref[:, 0:1])
    pad = jnp.zeros((LANES - route_t.shape[0], route_t.shape[1]), F32)
    gates_ref[...] = jnp.concatenate([route_t, pad], axis=0).T


def _post(yn, bonus, g, att, x, wts, glob):
    n = x.shape[0]
    tm = ROW_TILE
    row = lambda w: pl.BlockSpec((tm, w), lambda i: (i, 0))
    consts = [wts["woa"], wts["wor"], wts["gn"], wts["ln1"], glob["wrt"], glob["rb"]]
    return pl.pallas_call(
        _post_kernel,
        grid=(n // tm,),
        in_specs=[row(RWKV_WIDTH)] * 3 + [row(ATT_WIDTH), row(D_MODEL)] + [_const_spec(a.shape) for a in consts],
        out_specs=[row(D_MODEL), row(LANES)],
        out_shape=[jax.ShapeDtypeStruct((n, D_MODEL), F32), jax.ShapeDtypeStruct((n, LANES), F32)],
        compiler_params=_params("arbitrary"),
        name="post_mix",
    )(yn, bonus, g, att, x, *consts)


def _moe_positions_kernel(gates_ref, pos_ref, cnt_ref, cnt_scr, offs_scr, carry_scr):
    ph = pl.program_id(0)
    i = pl.program_id(1)
    g = gates_ref[...]
    tm = g.shape[0]
    lane = lax.broadcasted_iota(jnp.int32, g.shape, 1).astype(F32)
    oh1 = (lane == g[:, 0:1]).astype(F32)
    oh2 = (lane == g[:, 1:2]).astype(F32)
    oh = oh1 + oh2

    @pl.when((ph == 0) & (i == 0))
    def _zero():
        cnt_scr[...] = jnp.zeros_like(cnt_scr)

    @pl.when(ph == 0)
    def _count():
        cnt_scr[...] += jnp.sum(oh, axis=0, keepdims=True)
        pos_ref[...] = jnp.zeros_like(pos_ref)

    @pl.when((ph == 1) & (i == 0))
    def _offsets():
        cnt = cnt_scr[...]
        padded = jnp.floor((cnt + (MOE_BLOCK - 1)) * (1.0 / MOE_BLOCK)) * MOE_BLOCK
        r = lax.broadcasted_iota(jnp.int32, (LANES, LANES), 0)
        c = lax.broadcasted_iota(jnp.int32, (LANES, LANES), 1)
        offs_scr[...] = _dot_exact_rhs(padded, (r < c).astype(BF16))
        carry_scr[...] = jnp.zeros_like(carry_scr)
        cnt_ref[...] = cnt

    @pl.when(ph == 1)
    def _rank():
        r = lax.broadcasted_iota(jnp.int32, (tm, tm), 0)
        c = lax.broadcasted_iota(jnp.int32, (tm, tm), 1)
        before = _dg((r > c).astype(BF16), oh.astype(BF16), 1, 0)
        base = before + carry_scr[0:1, :] + offs_scr[0:1, :]
        p1 = jnp.sum(oh1 * base, axis=1, keepdims=True)
        p2 = jnp.sum(oh2 * base, axis=1, keepdims=True)
        pos_ref[...] = jnp.where(lane == 0.0, p1, jnp.where(lane == 1.0, p2, 0.0))
        carry_scr[...] += jnp.sum(oh, axis=0, keepdims=True)


def _moe_positions(gates):
    n = gates.shape[0]
    tm = min(MOE_POS_TILE, n)
    stat = pltpu.VMEM((SUBLANES, LANES), F32)
    return pl.pallas_call(
        _moe_positions_kernel,
        grid=(2, n // tm),
        in_specs=[pl.BlockSpec((tm, LANES), lambda ph, i: (i, 0))],
        out_specs=[pl.BlockSpec((tm, LANES), lambda ph, i: (i * ph, 0)), _const_spec((SUBLANES, LANES))],
        out_shape=[jax.ShapeDtypeStruct((n, LANES), F32), jax.ShapeDtypeStruct((SUBLANES, LANES), F32)],
        scratch_shapes=[stat, stat, stat],
        compiler_params=_params("arbitrary", "arbitrary"),
        name="moe_positions",
    )(gates)


def _moe_dispatch_kernel(pos_ref, x_ref, xs_in_ref, xs_ref, sem):
    del xs_in_ref
    tm = x_ref.shape[0]

    def row_copy(r, k):
        return pltpu.make_async_copy(x_ref.at[pl.ds(r, 1), :], xs_ref.at[pl.ds(pos_ref[k, r], 1), :], sem)

    def issue(r, carry):
        row_copy(r, 0).start()
        row_copy(r, 1).start()
        return carry

    def drain(r, carry):
        row_copy(r, 0).wait()
        row_copy(r, 1).wait()
        return carry

    lax.fori_loop(0, tm, issue, 0)
    lax.fori_loop(0, tm, drain, 0)


def _moe_dispatch(x1, pos, n_slots):
    n = x1.shape[0]
    tm = MOE_DMA_TILE
    xs0 = jnp.zeros((n_slots, D_MODEL), F32)
    return pl.pallas_call(
        _moe_dispatch_kernel,
        grid=(n // tm,),
        in_specs=[pl.BlockSpec((2, tm), lambda i: (0, i), memory_space=pltpu.SMEM),
                  pl.BlockSpec((tm, D_MODEL), lambda i: (i, 0)),
                  pl.BlockSpec(memory_space=pl.ANY)],
        out_specs=pl.BlockSpec(memory_space=pl.ANY),
        out_shape=jax.ShapeDtypeStruct((n_slots, D_MODEL), F32),
        scratch_shapes=[pltpu.SemaphoreType.DMA],
        input_output_aliases={2: 0},
        compiler_params=_params("arbitrary"),
        name="moe_dispatch",
    )(pos, x1, xs0)


def _moe_experts_kernel(blk_expert_ref, n_used_ref, xs_ref, wg_ref, wu_ref, wd_ref, ys_ref):
    del blk_expert_ref
    b = pl.program_id(0)

    @pl.when(b < n_used_ref[0])
    def _compute():
        xb = xs_ref[...].astype(BF16)
        h = jax.nn.silu(_dg(xb, wg_ref[0, 0], 1, 0)) * _dg(xb, wu_ref[0, 0], 1, 0)
        ys_ref[...] = _dg(h.astype(BF16), wd_ref[0, 0], 1, 0)

    @pl.when(b >= n_used_ref[0])
    def _skip():
        ys_ref[...] = jnp.zeros_like(ys_ref)


def _moe_experts(xs, blk_expert, n_used, glob, l):
    n_slots = xs.shape[0]
    R = MOE_BLOCK
    last = lambda b, nu: jnp.minimum(b, nu[0] - 1)
    wspec = lambda s: pl.BlockSpec((1, 1) + s, lambda b, be, nu: (l, be[last(b, nu)], 0, 0))
    return pl.pallas_call(
        _moe_experts_kernel,
        grid_spec=pltpu.PrefetchScalarGridSpec(
            num_scalar_prefetch=2,
            grid=(n_slots // R,),
            in_specs=[pl.BlockSpec((R, D_MODEL), lambda b, be, nu: (last(b, nu), 0)),
                      wspec((D_MODEL, EXPERT_FF)), wspec((D_MODEL, EXPERT_FF)), wspec((EXPERT_FF, D_MODEL))],
            out_specs=pl.BlockSpec((R, D_MODEL), lambda b, be, nu: (b, 0)),
        ),
        out_shape=jax.ShapeDtypeStruct((n_slots, D_MODEL), F32),
        compiler_params=_params("arbitrary"),
        name="moe_experts",
    )(blk_expert, n_used, xs, glob["wg"], glob["wu"], glob["wd"])


def _moe_combine_kernel(pos_ref, x_ref, gates_ref, ys_ref, ln_ref, o_ref, buf, sem):
    tm = x_ref.shape[0]

    def row_copy(r, k):
        return pltpu.make_async_copy(ys_ref.at[pl.ds(pos_ref[k, r], 1), :], buf.at[k, pl.ds(r, 1), :], sem)

    def issue(r, carry):
        row_copy(r, 0).start()
        row_copy(r, 1).start()
        return carry

    def drain(r, carry):
        row_copy(r, 0).wait()
        row_copy(r, 1).wait()
        return carry

    lax.fori_loop(0, tm, issue, 0)
    lax.fori_loop(0, tm, drain, 0)
    g = gates_ref[...]
    ffn = g[:, 2:3] * buf[0] + g[:, 3:4] * buf[1]
    o_ref[...] = _layer_norm(ALPHA * x_ref[...] + ffn, ln_ref[0:1, :], ln_ref[1:2, :])


def _moe_combine(x1, gates, pos, ys, wts):
    n = x1.shape[0]
    tm = MOE_DMA_TILE
    return pl.pallas_call(
        _moe_combine_kernel,
        grid=(n // tm,),
        in_specs=[pl.BlockSpec((2, tm), lambda i: (0, i), memory_space=pltpu.SMEM),
                  pl.BlockSpec((tm, D_MODEL), lambda i: (i, 0)),
                  pl.BlockSpec((tm, LANES), lambda i: (i, 0)),
                  pl.BlockSpec(memory_space=pl.ANY),
                  _const_spec(wts["ln2"].shape)],
        out_specs=pl.BlockSpec((tm, D_MODEL), lambda i: (i, 0)),
        out_shape=jax.ShapeDtypeStruct((n, D_MODEL), F32),
        scratch_shapes=[pltpu.VMEM((2, tm, D_MODEL), F32), pltpu.SemaphoreType.DMA],
        compiler_params=_params("arbitrary"),
        name="moe_combine",
    )(pos, x1, gates, ys, wts["ln2"])


def _moe(x1, gates, wts, glob, l):
    n = x1.shape[0]
    R = MOE_BLOCK
    n_slots = N_EXPERTS_PER_TOKEN * n + N_EXPERTS * R
    pos_f, cnt = _moe_positions(gates)
    pos = pos_f[:, :N_EXPERTS_PER_TOKEN].astype(jnp.int32).T
    blocks = (cnt[0, :N_EXPERTS].astype(jnp.int32) + (R - 1)) // R
    ends = jnp.cumsum(blocks)
    n_used = ends[-1:]
    blk_expert = jnp.sum(jnp.arange(n_slots // R, dtype=jnp.int32)[:, None] >= ends[None, :], axis=1)
    blk_expert = jnp.minimum(blk_expert, N_EXPERTS - 1).astype(jnp.int32)
    xs = _moe_dispatch(x1, pos, n_slots)
    ys = _moe_experts(xs, blk_expert, n_used, glob, l)
    return _moe_combine(x1, gates, pos, ys, wts)


def _rows8(vectors, width):
    rows = [v.reshape(1, width).astype(F32) for v in vectors]
    rows.append(jnp.zeros((8 - len(rows), width), F32))
    return jnp.concatenate(rows, axis=0)


def _prep_layer(l, p):
    W = RWKV_WIDTH
    w_l = p["w_in"][l]
    rwc = w_l[:, N_ATT_COLS:]
    mu = p["mu_rwkv"][l]
    o_w, o_k, o_v, o_a, o_g = W, W + D_DECAY_LORA, 2 * W + D_DECAY_LORA, 3 * W + D_DECAY_LORA, 3 * W + 2 * D_DECAY_LORA

    def reorder(t):
        parts = [t[..., 0:W], t[..., o_k:o_k + W], t[..., o_v:o_v + W], t[..., o_g:o_g + D_GATE_LORA],
                 t[..., o_w:o_w + D_DECAY_LORA], t[..., o_a:o_a + D_AAA_LORA]]
        return parts

    w_parts = reorder(rwc)
    mu_parts = reorder(mu)
    if l > 0:
        padw = LANES - D_MV_LORA
        w_parts += [p["w_vres_in"][l - 1], jnp.zeros((D_MODEL, padw), F32)]
        mu_parts += [p["mu_vres"][l - 1], jnp.zeros((padw,), F32)]
    wrw = jnp.concatenate(w_parts, axis=1).astype(BF16)
    mu_row = jnp.concatenate(mu_parts).reshape(1, -1)
    zero = jnp.zeros((D_DECAY_LORA, W), F32)
    wda = jnp.concatenate([jnp.concatenate([p["decay_w2"][l], zero], axis=1),
                           jnp.concatenate([zero, p["aaa_a2"][l]], axis=1)], axis=0).astype(BF16)
    vecs = [p["decay_w0"][l], p["aaa_a0"][l], p["k_k"][l], p["k_a"][l], p["r_k"][l].reshape(W)]
    out = {
        "watt": w_l[:, :N_ATT_COLS].astype(BF16),
        "wrw": wrw,
        "mu": mu_row,
        "wda": wda,
        "g2": p["gate_g2"][l].astype(BF16),
        "woa": p["w_o"][l][:ATT_WIDTH].astype(BF16),
        "wor": p["w_o"][l][ATT_WIDTH:].astype(BF16),
        "gn": _rows8([p["gn_g"][l], p["gn_b"][l]], W),
        "ln1": _rows8([p["ln1_g"][l], p["ln1_b"][l]], D_MODEL),
        "ln2": _rows8([p["ln2_g"][l], p["ln2_b"][l]], D_MODEL),
        "sink_rows": jnp.broadcast_to(p["sinks"][l].reshape(N_Q_HEADS, 1), (N_Q_HEADS, LANES)).astype(F32),
    }
    if l > 0:
        vecs.append(p["vres_v0"][l - 1])
        out["v2"] = jnp.concatenate([p["vres_v2"][l - 1], jnp.zeros((LANES - D_MV_LORA, W), F32)], axis=0).astype(BF16)
    out["vecA"] = _rows8(vecs, W)
    hid = jnp.arange(W) // HEAD_DIM
    out["bd"] = (hid[:, None] == hid[None, :]).astype(BF16)
    return out


def _trunk(x3, shift_prev, cache_k, cache_v, wkv_prev, layer_wts, glob):
    decode = cache_k is not None
    n_batch, seq, _ = x3.shape
    x = x3.reshape(n_batch * seq, D_MODEL)
    new_k, new_v, new_wkv, new_shift = [], [], [], []
    v_first = None
    for l in range(DEPTH):
        wts = layer_wts[l]
        new_shift.append(x.reshape(n_batch, seq, D_MODEL)[:, -1])
        q, ka, va, r, lw, k, v, a, b, g, bonus = _inproj(x, shift_prev[l], wts, v_first, seq)
        if l == 0:
            v_first = v
        if decode:
            ck = cache_k[l].reshape(n_batch, WINDOW, KV_WIDTH)
            cv = cache_v[l].reshape(n_batch, WINDOW, KV_WIDTH)
            att = _attn_sample(q, ka, va, ck, cv, wts["sink_rows"], n_batch, seq)
            new_k.append(ka.reshape(n_batch, seq, N_KV_HEADS, HEAD_DIM))
            new_v.append(va.reshape(n_batch, seq, N_KV_HEADS, HEAD_DIM))
            yn, s_out = _wkv_scan(r, lw, k, v, a, b, (wkv_prev, l), n_batch, seq, seq, SAMPLE_WKV_GROUP)
        else:
            att = _attn_prompt(q, ka, va, wts["sink_rows"], n_batch, seq)
            last = lambda t: t.reshape(n_batch, seq, KV_WIDTH)[:, -WINDOW:].reshape(n_batch, WINDOW, N_KV_HEADS, HEAD_DIM)
            new_k.append(last(ka))
            new_v.append(last(va))
            yn, s_out = _wkv_scan(r, lw, k, v, a, b, None, n_batch, seq, WKV_CHUNK, 1)
        new_wkv.append(s_out)
        x1, gates = _post(yn, bonus, g, att, x, wts, glob)
        x = _moe(x1, gates, wts, glob, l)
    return (x.reshape(n_batch, seq, D_MODEL), jnp.stack(new_k), jnp.stack(new_v), jnp.stack(new_wkv),
            jnp.stack(new_shift))


def kernel(x_prompt, x_sample, cache_k, cache_v, state_wkv, state_shift, w_in, w_vres_in, mu_rwkv, mu_vres, sinks, decay_w0, decay_w2, aaa_a0, aaa_a2, vres_v0, vres_v2, gate_g2, k_k, k_a, r_k, gn_g, gn_b, w_o, ln1_g, ln1_b, w_router, router_bias, w_gate, w_up, w_down, ln2_g, ln2_b):
    p = dict(w_in=w_in, w_vres_in=w_vres_in, mu_rwkv=mu_rwkv, mu_vres=mu_vres, sinks=sinks,
             decay_w0=decay_w0, decay_w2=decay_w2, aaa_a0=aaa_a0, aaa_a2=aaa_a2,
             vres_v0=vres_v0, vres_v2=vres_v2, gate_g2=gate_g2, k_k=k_k, k_a=k_a, r_k=r_k,
             gn_g=gn_g, gn_b=gn_b, w_o=w_o, ln1_g=ln1_g, ln1_b=ln1_b,
             w_gate=w_gate, w_up=w_up, w_down=w_down, ln2_g=ln2_g, ln2_b=ln2_b)
    layer_wts = [_prep_layer(l, p) for l in range(DEPTH)]
    glob = {
        "wrt": w_router.T.astype(F32),
        "rb": jnp.broadcast_to(router_bias.reshape(N_EXPERTS, 1), (N_EXPERTS, LANES)).astype(F32),
        "wg": w_gate.astype(BF16),
        "wu": w_up.astype(BF16),
        "wd": w_down.astype(BF16),
    }
    b_p = x_prompt.shape[0]
    zero_shift = jnp.zeros((DEPTH, b_p, D_MODEL), x_prompt.dtype)
    y_p, k_p, v_p, wkv_p, shift_p = _trunk(x_prompt, zero_shift, None, None, None, layer_wts, glob)
    y_s, k_s, v_s, wkv_s, shift_s = _trunk(x_sample, state_shift, cache_k, cache_v, state_wkv, layer_wts, glob)
    return (y_p, y_s, k_p, v_p, wkv_p, shift_p, k_s, v_s, wkv_s, shift_s)
```

```python
import functools
import math

import jax
import jax.numpy as jnp
from jax import lax
from jax.experimental import pallas as pl
from jax.experimental.pallas import tpu as pltpu

F32 = jnp.float32
BF16 = jnp.bfloat16

D_MODEL = 1024
DEPTH = 4
HEAD_DIM = 64
ATT_WIDTH = 512
RWKV_WIDTH = 512
N_Q_HEADS = 8
N_KV_HEADS = 2
Q_PER_KV = 4
KV_WIDTH = 128
N_ATT_COLS = ATT_WIDTH + 2 * KV_WIDTH
WINDOW = 128
ATT_SCALE = HEAD_DIM ** -0.5
N_RWKV_HEADS = 8
D_DECAY_LORA = 64
D_AAA_LORA = 64
D_GATE_LORA = 128
D_MV_LORA = 32
DECAY_SCALE = math.exp(-0.5)
GN_EPS = 64e-5
LN_EPS = 1e-5
N_EXPERTS = 16
N_EXPERT_GROUPS = 4
EXPERTS_PER_GROUP = 4
EXPERT_FF = 512
ALPHA = (2 * DEPTH) ** 0.25

LANES = 128
SUBLANES = 8
VMEM_LIMIT_BYTES = 56 * 1024 * 1024
ROW_TILE = 256
MOE_BLOCK = 256
MOE_POS_TILE = 512
MOE_DMA_TILE = 256
N_EXPERTS_PER_TOKEN = 2
DMA_ISSUE_UNROLL = 8
WKV_CHUNK = 128
SAMPLE_ATT_BATCH = 8
SAMPLE_WKV_GROUP = 8


def _dg(a, b, ca, cb):
    return lax.dot_general(a, b, (((ca,), (cb,)), ((), ())), preferred_element_type=F32)


def _bdot(a, b):
    return _dg(a.astype(BF16), b.astype(BF16), 1, 0)


def _split2(x):
    hi = x.astype(BF16)
    lo = (x - hi.astype(F32)).astype(BF16)
    return hi, lo


def _dot3(a, b, ca=1, cb=0):
    ah, al = _split2(a)
    bh, bl = _split2(b)
    return _dg(ah, bh, ca, cb) + (_dg(ah, bl, ca, cb) + _dg(al, bh, ca, cb))


def _dot_exact_lhs(m_bf16, x, parts=3):
    acc = None
    rem = x
    for _ in range(parts):
        p = rem.astype(BF16)
        t = _dg(m_bf16, p, 1, 0)
        acc = t if acc is None else acc + t
        rem = rem - p.astype(F32)
    return acc


def _dot_exact_rhs(x, m_bf16, parts=3):
    acc = None
    rem = x
    for _ in range(parts):
        p = rem.astype(BF16)
        t = _dg(p, m_bf16, 1, 0)
        acc = t if acc is None else acc + t
        rem = rem - p.astype(F32)
    return acc


def _layer_norm(z, g, b):
    mu = jnp.mean(z, axis=-1, keepdims=True)
    var = jnp.mean(jnp.square(z - mu), axis=-1, keepdims=True)
    return (z - mu) * lax.rsqrt(var + LN_EPS) * g + b


def _params(*sem):
    return pltpu.CompilerParams(dimension_semantics=sem, vmem_limit_bytes=VMEM_LIMIT_BYTES)


def _const_spec(shape):
    nd = len(shape)
    return pl.BlockSpec(shape, lambda *_: (0,) * nd)


def _inproj_kernel(*refs, first, seq):
    if first:
        (x_ref, st_ref, watt_ref, wrw_ref, mu_ref, wda_ref, g2_ref, vec_ref, bd_ref,
         q_ref, ka_ref, va_ref, r_ref, lw_ref, k_ref, v_ref, a_ref, b_ref, g_ref, bonus_ref, carry_ref) = refs
    else:
        (x_ref, st_ref, watt_ref, wrw_ref, mu_ref, wda_ref, g2_ref, vec_ref, bd_ref, vfirst_ref, v2_ref,
         q_ref, ka_ref, va_ref, r_ref, lw_ref, k_ref, v_ref, a_ref, b_ref, g_ref, bonus_ref, carry_ref) = refs
    i = pl.program_id(0)
    x = x_ref[...]
    tm = x.shape[0]
    qkv = _dg(x.astype(BF16), watt_ref[...], 1, 0)
    q_ref[...] = qkv[:, :ATT_WIDTH]
    ka_ref[...] = qkv[:, ATT_WIDTH:ATT_WIDTH + KV_WIDTH]
    va_ref[...] = qkv[:, ATT_WIDTH + KV_WIDTH:]

    whole_tiles = seq >= tm
    st = jnp.broadcast_to(st_ref[0], (2 * SUBLANES, D_MODEL)) if whole_tiles else st_ref[...]
    pe = _dg(jnp.concatenate([x, st], axis=0).astype(BF16), wrw_ref[...], 1, 0)
    pc = pe[:tm]
    pst = pe[tm:]
    rowid = lax.broadcasted_iota(jnp.int32, pc.shape, 0)
    pp = pltpu.roll(pc, 1, 0)
    if whole_tiles:
        @pl.when(i == 0)
        def _init():
            carry_ref[...] = jnp.zeros_like(carry_ref)

        prev0 = jnp.where(i % (seq // tm) == 0, pst[0:1], carry_ref[SUBLANES - 1:SUBLANES, :])
        pp = jnp.where(rowid == 0, prev0, pp)
        carry_ref[...] = pc[tm - SUBLANES:]
    else:
        n_st = tm // seq
        er = lax.broadcasted_iota(jnp.int32, (tm, n_st), 0)
        ec = lax.broadcasted_iota(jnp.int32, (tm, n_st), 1)
        expand = (er == ec * seq).astype(BF16)
        pp = jnp.where(rowid % seq == 0, _dot_exact_lhs(expand, pst), pp)
    rw = pc + (pp - pc) * mu_ref[...]
    W = RWKV_WIDTH
    r = rw[:, 0:W]
    k = rw[:, W:2 * W]
    v = rw[:, 2 * W:3 * W]
    gl = rw[:, 3 * W:3 * W + D_GATE_LORA]
    wa = rw[:, 3 * W + D_GATE_LORA:3 * W + 2 * LANES]
    w0 = vec_ref[0:1, :]
    a0 = vec_ref[1:2, :]
    k_k = vec_ref[2:3, :]
    k_a = vec_ref[3:4, :]
    r_k = vec_ref[4:5, :]

    lane = lax.broadcasted_iota(jnp.int32, wa.shape, 1)
    wa_t = jnp.where(lane < D_DECAY_LORA, jnp.tanh(wa), wa)
    da = _bdot(wa_t, wda_ref[...])
    lw = -DECAY_SCALE * jax.nn.sigmoid(w0 + da[:, :W])
    a = jax.nn.sigmoid(a0 + da[:, W:])
    g = _bdot(jax.nn.sigmoid(gl), g2_ref[...])
    if not first:
        mv = rw[:, 3 * W + 2 * LANES:]
        v0 = vec_ref[5:6, :]
        v = v + (vfirst_ref[...] - v) * jax.nn.sigmoid(v0 + _bdot(mv, v2_ref[...]))
    bd = bd_ref[...]
    kk = k * k_k
    ssq = _dot_exact_rhs(kk * kk, bd)
    kk = kk / jnp.maximum(jnp.sqrt(ssq), 1e-12)
    k = k * (1.0 + (a - 1.0) * k_a)
    bonus = _dot_exact_rhs(r * k * r_k, bd) * v

    r_ref[...] = r
    lw_ref[...] = lw
    k_ref[...] = k
    v_ref[...] = v
    a_ref[...] = -kk
    b_ref[...] = kk * a
    g_ref[...] = g
    bonus_ref[...] = bonus


def _inproj(x, shift_state, wts, v_first, seq):
    n = x.shape[0]
    first = v_first is None
    tm = ROW_TILE
    row = lambda w: pl.BlockSpec((tm, w), lambda i: (i, 0))
    if seq >= tm:
        assert seq % tm == 0
        st = shift_state.reshape(-1, 1, D_MODEL)
        st_spec = pl.BlockSpec((1, 1, D_MODEL), lambda i: (i // (seq // tm), 0, 0))
    else:
        assert tm % seq == 0
        st = shift_state
        st_spec = pl.BlockSpec((tm // seq, D_MODEL), lambda i: (i, 0))
    ins = [x, st, wts["watt"], wts["wrw"], wts["mu"], wts["wda"], wts["g2"], wts["vecA"], wts["bd"]]
    in_specs = [row(D_MODEL), st_spec] + [_const_spec(a.shape) for a in ins[2:]]
    if not first:
        ins += [v_first, wts["v2"]]
        in_specs += [row(RWKV_WIDTH), _const_spec(wts["v2"].shape)]
    widths = [ATT_WIDTH, KV_WIDTH, KV_WIDTH] + [RWKV_WIDTH] * 8
    return pl.pallas_call(
        functools.partial(_inproj_kernel, first=first, seq=seq),
        grid=(n // tm,),
        in_specs=in_specs,
        out_specs=[row(w) for w in widths],
        out_shape=[jax.ShapeDtypeStruct((n, w), F32) for w in widths],
        scratch_shapes=[pltpu.VMEM((SUBLANES, wts["wrw"].shape[1]), F32)],
        compiler_params=_params("arbitrary"),
        name="inproj",
    )(*ins)


def _dot3s(a_sp, b_sp, ca=1, cb=0):
    (ah, al), (bh, bl) = a_sp, b_sp
    return _dg(ah, bh, ca, cb) + (_dg(ah, bl, ca, cb) + _dg(al, bh, ca, cb))


def _dot1s(a_sp, b_sp, ca=1, cb=0):
    return _dg(a_sp[0], b_sp[0], ca, cb)


def _wkv_kernel(*refs, group, seq_chunk, has_state):
    if has_state:
        r_ref, lw_ref, k_ref, v_ref, a_ref, b_ref, s0_ref, y_ref, sout_ref, s_scr = refs
    else:
        r_ref, lw_ref, k_ref, v_ref, a_ref, b_ref, y_ref, sout_ref, s_scr = refs
    G, Ls = group, seq_chunk
    L = G * Ls
    H, N = N_RWKV_HEADS, HEAD_DIM
    c_idx = pl.program_id(1)

    @pl.when(c_idx == 0)
    def _init():
        if has_state:
            s_scr[...] = s0_ref[...]
        else:
            s_scr[...] = jnp.zeros_like(s_scr)

    row = lax.broadcasted_iota(jnp.int32, (L, L), 0)
    col = lax.broadcasted_iota(jnp.int32, (L, L), 1)
    same = (row // Ls) == (col // Ls)
    incl = same & (row >= col)
    strict = same & (row > col)
    eye = (row == col).astype(F32)
    row2 = lax.broadcasted_iota(jnp.int32, (2 * L, 2 * L), 0)
    col2 = lax.broadcasted_iota(jnp.int32, (2 * L, 2 * L), 1)
    t_q = row2 % L
    t_k = col2 % L
    mask2 = ((t_q // Ls) == (t_k // Ls)) & ((t_q > t_k) | ((row2 >= L) & (t_q == t_k)))

    lw_all = lw_ref[...]
    cum_all = _dot_exact_lhs(incl.astype(BF16), lw_all)
    if G == 1:
        tot_all = jnp.broadcast_to(cum_all[L - 1:L, :], cum_all.shape)
    else:
        tot_all = _dot_exact_lhs(same.astype(BF16), lw_all)
    e_end = jnp.exp(tot_all - cum_all)
    etot_all = jnp.exp(tot_all)
    a_all = a_ref[...]
    r_all = r_ref[...]
    b_all = b_ref[...]
    k_all = k_ref[...]
    v_all = v_ref[...]
    ar_all = jnp.concatenate([a_all * jnp.exp(cum_all - lw_all), r_all * jnp.exp(cum_all)], axis=0)
    ar_sp = _split2(ar_all)
    if G == 1:
        mid = cum_all[L // 2 - 1:L // 2, :]
        e_neg = jnp.exp(mid - cum_all)
        arc_sp = _split2(jnp.concatenate([a_all * jnp.exp(cum_all - lw_all - mid), r_all * jnp.exp(cum_all - mid)],
                                         axis=0))
    else:
        e_neg = jnp.exp(-cum_all)
        arc_sp = ar_sp
    bk_sp = _split2(jnp.concatenate([b_all * e_neg, k_all * e_neg], axis=0))
    bkh_sp = _split2(jnp.concatenate([b_all * e_end, k_all * e_end], axis=0))
    heads = range(H)
    hsl = lambda t, h: t[:, h * N:(h + 1) * N]
    hsp = lambda sp, h: (hsl(sp[0], h), hsl(sp[1], h))

    m = [jnp.where(mask2, _dg(hsl(arc_sp[0], h), hsl(bk_sp[0], h), 1, 1), 0.0) for h in heads]
    m_sp = [_split2(t) for t in m]
    a_ab = [t[:L, :L] for t in m]
    d = [eye + jnp.where((row // 2) == (col // 2), t, 0.0) for t in a_ab]
    n = 4
    while n <= Ls:
        off = ((row // n) == (col // n)) & ((row // (n // 2)) != (col // (n // 2)))
        d_b = [t.astype(BF16) for t in d]
        dn = [_dg(t, jnp.where(off, s, 0.0).astype(BF16), 1, 0) for t, s in zip(d_b, a_ab)]
        d = [t + _dg(p.astype(BF16), tb, 1, 0) for t, p, tb in zip(d, dn, d_b)]
        n *= 2
    resid = [eye - t + _dot3s((s[0][:L, :L], s[1][:L, :L]), _split2(t)) for t, s in zip(d, m_sp)]
    t_inv = [t + _dg(t.astype(BF16), r.astype(BF16), 1, 0) for t, r in zip(d, resid)]

    s0 = [[s_scr[g, h] for h in heads] for g in range(G)]
    s0_sp = [[_split2(s0[g][h]) for h in heads] for g in range(G)]

    def seq_rows(sp, g, h):
        if G == 1:
            return hsp(sp, h)
        return tuple(jnp.concatenate([hsl(p, h)[g * Ls:(g + 1) * Ls], hsl(p, h)[L + g * Ls:L + (g + 1) * Ls]], axis=0)
                     for p in sp)

    st = [[_dot1s(seq_rows(ar_sp, g, h), s0_sp[g][h], 1, 1) for h in heads] for g in range(G)]
    rhs_st = [jnp.concatenate([st[g][h][:Ls] for g in range(G)], axis=0) for h in heads]
    y_st = [jnp.concatenate([st[g][h][Ls:] for g in range(G)], axis=0) for h in heads]

    v_h = [hsl(v_all, h) for h in heads]
    zeros = jnp.zeros((L, N), F32)
    m_top = [(s[0][:L], s[1][:L]) for s in m_sp]
    m_bot = [(s[0][L:], s[1][L:]) for s in m_sp]
    rhs = [rhs_st[h] + _dot1s(m_top[h], _split2(jnp.concatenate([zeros, v_h[h]], axis=0))) for h in heads]
    u = [_dot1s(_split2(t_inv[h]), _split2(rhs[h])) for h in heads]
    uv_sp = [_split2(jnp.concatenate([u[h], v_h[h]], axis=0)) for h in heads]
    y = [y_st[h] + _dg(m_bot[h][0], uv_sp[h][0], 1, 0) for h in heads]
    for g in range(G):
        for h in heads:
            if G == 1:
                uv_g = uv_sp[h]
            else:
                uv_g = tuple(jnp.concatenate([p[g * Ls:(g + 1) * Ls], p[L + g * Ls:L + (g + 1) * Ls]], axis=0)
                             for p in uv_sp[h])
            upd = _dot1s(uv_g, seq_rows(bkh_sp, g, h), 0, 0)
            s_scr[g, h] = s0[g][h] * hsl(etot_all, h)[g * Ls:g * Ls + 1] + upd
    ys = []
    for h in heads:
        mu = jnp.mean(y[h], axis=-1, keepdims=True)
        var = jnp.mean(jnp.square(y[h] - mu), axis=-1, keepdims=True)
        ys.append((y[h] - mu) * lax.rsqrt(var + GN_EPS))
    y_ref[...] = jnp.concatenate(ys, axis=1)

    @pl.when(c_idx == pl.num_programs(1) - 1)
    def _fin():
        sout_ref[...] = s_scr[...]


def _wkv_scan(r, lw, k, v, a, b, state, n_seq, seq, seq_chunk, group):
    n_chunks = seq // seq_chunk
    assert group == 1 or n_chunks == 1
    has_state = state is not None
    rows = group * seq_chunk
    tok_spec = pl.BlockSpec((rows, RWKV_WIDTH), lambda i, c: (i * n_chunks + c, 0))
    st_spec = pl.BlockSpec((group, N_RWKV_HEADS, HEAD_DIM, HEAD_DIM), lambda i, c: (i, 0, 0, 0))
    in_specs = [tok_spec] * 6
    args = (r, lw, k, v, a, b)
    if has_state:
        states, layer = state
        in_specs.append(pl.BlockSpec((None, group, N_RWKV_HEADS, HEAD_DIM, HEAD_DIM),
                                     lambda i, c: (layer, i, 0, 0, 0)))
        args += (states,)
    return pl.pallas_call(
        functools.partial(_wkv_kernel, group=group, seq_chunk=seq_chunk, has_state=has_state),
        grid=(n_seq // group, n_chunks),
        in_specs=in_specs,
        out_specs=[tok_spec, st_spec],
        out_shape=[jax.ShapeDtypeStruct((n_seq * seq, RWKV_WIDTH), F32),
                   jax.ShapeDtypeStruct((n_seq, N_RWKV_HEADS, HEAD_DIM, HEAD_DIM), F32)],
        scratch_shapes=[pltpu.VMEM((group, N_RWKV_HEADS, HEAD_DIM, HEAD_DIM), F32)],
        compiler_params=_params("arbitrary", "arbitrary"),
        name="wkv_scan",
    )(*args)


def _sink_softmax(s, sink):
    m = sink
    for t in s:
        m = jnp.maximum(m, jnp.max(t, axis=-1, keepdims=True))
    es = [jnp.exp(t - m) for t in s]
    den = jnp.exp(sink - m)
    for e in es:
        den = den + jnp.sum(e, axis=-1, keepdims=True)
    return [e / den for e in es]


def _attn_prompt_kernel(q_ref, kc_ref, kp_ref, vc_ref, vp_ref, sink_ref, o_ref):
    L = WINDOW
    n = pl.program_id(1)
    q = q_ref[...].astype(BF16)
    kw = jnp.concatenate([kp_ref[...], kc_ref[...]], axis=0).astype(BF16)
    vw = jnp.concatenate([vp_ref[...], vc_ref[...]], axis=0).astype(BF16)
    qi = lax.broadcasted_iota(jnp.int32, (L, 2 * L), 0)
    kj = lax.broadcasted_iota(jnp.int32, (L, 2 * L), 1)
    diff = qi + L - kj
    mask = (diff >= 0) & (diff < WINDOW) & ((kj >= L) | (n > 0))
    lane = lax.broadcasted_iota(jnp.int32, vw.shape, 1)
    one = jnp.ones_like(vw)
    vext = [jnp.where(lane < HEAD_DIM, vw, one), jnp.where(lane >= HEAD_DIM, vw, one)]
    heads = range(N_Q_HEADS)
    hsl = lambda t, h: t[:, h * HEAD_DIM:(h + 1) * HEAD_DIM]
    ss = [jnp.where(mask, _dg(hsl(q, h), hsl(kw, h // Q_PER_KV), 1, 1) * ATT_SCALE, -jnp.inf) for h in heads]
    sinks = [sink_ref[h:h + 1, 0:1] for h in heads]
    ms = [jnp.maximum(jnp.max(ss[h], axis=-1, keepdims=True), sinks[h]) for h in heads]
    es = [jnp.exp(ss[h] - ms[h]).astype(BF16) for h in heads]
    oes = [_dg(es[h], vext[h // Q_PER_KV], 1, 0) for h in heads]
    outs = []
    for h in heads:
        hk = h // Q_PER_KV
        rs = oes[h][:, (1 - hk) * HEAD_DIM:(1 - hk) * HEAD_DIM + 1]
        outs.append(hsl(oes[h], hk) / (rs + jnp.exp(sinks[h] - ms[h])))
    o_ref[...] = jnp.concatenate(outs, axis=1)


def _attn_prompt(q, k, v, sink_rows, n_batch, seq):
    nb = seq // WINDOW
    cur = lambda w: pl.BlockSpec((WINDOW, w), lambda b, n: (b * nb + n, 0))
    prev = lambda w: pl.BlockSpec((WINDOW, w), lambda b, n: (b * nb + jnp.maximum(n - 1, 0), 0))
    return pl.pallas_call(
        _attn_prompt_kernel,
        grid=(n_batch, nb),
        in_specs=[cur(ATT_WIDTH), cur(KV_WIDTH), prev(KV_WIDTH), cur(KV_WIDTH), prev(KV_WIDTH),
                  _const_spec(sink_rows.shape)],
        out_specs=cur(ATT_WIDTH),
        out_shape=jax.ShapeDtypeStruct((n_batch * seq, ATT_WIDTH), F32),
        compiler_params=_params("arbitrary", "arbitrary"),
        name="attn_prompt",
    )(q, k, k, v, v, sink_rows)


def _attn_sample_kernel(q_ref, kn_ref, vn_ref, ck_ref, cv_ref, sink_ref, o_ref, *, seq):
    S = seq
    W = WINDOW
    B = SAMPLE_ATT_BATCH
    R = Q_PER_KV * S
    tq1 = lax.broadcasted_iota(jnp.int32, (R, W), 0) % S
    kj1 = lax.broadcasted_iota(jnp.int32, (R, W), 1)
    mask_cache = kj1 > tq1
    tq2 = lax.broadcasted_iota(jnp.int32, (R, S), 0) % S
    kj2 = lax.broadcasted_iota(jnp.int32, (R, S), 1)
    mask_new = kj2 <= tq2
    kvs = range(N_KV_HEADS)
    hsl = lambda t, h: t[:, h * HEAD_DIM:(h + 1) * HEAD_DIM]
    group = lambda hk: range(hk * Q_PER_KV, (hk + 1) * Q_PER_KV)
    seq_rows = lambda t, b: t[b * S:(b + 1) * S]
    q = q_ref[...]
    kn = kn_ref[...]
    vn = vn_ref[...]
    qh = [hsl(q, h) for h in range(N_Q_HEADS)]
    knh = [hsl(kn, hk) for hk in kvs]
    vnh = [hsl(vn, hk) for hk in kvs]
    sinks = [jnp.concatenate([jnp.broadcast_to(sink_ref[h:h + 1, 0:1], (S, 1)) for h in group(hk)], axis=0)
             for hk in kvs]
    items = [(b, hk) for b in range(B) for hk in kvs]
    qs = [jnp.concatenate([seq_rows(qh[h], b) for h in group(hk)], axis=0).astype(BF16) for b, hk in items]
    ck = [ck_ref[b].astype(BF16) for b in range(B)]
    cv = [cv_ref[b].astype(BF16) for b in range(B)]
    s1 = [jnp.where(mask_cache, _dg(qs[i], hsl(ck[b], hk), 1, 1) * ATT_SCALE, -jnp.inf)
          for i, (b, hk) in enumerate(items)]
    s2 = [jnp.where(mask_new, _dg(qs[i], seq_rows(knh[hk], b).astype(BF16), 1, 1) * ATT_SCALE, -jnp.inf)
          for i, (b, hk) in enumerate(items)]
    ps = [_sink_softmax([s1[i], s2[i]], sinks[hk]) for i, (b, hk) in enumerate(items)]
    outs = [_dg(ps[i][0].astype(BF16), hsl(cv[b], hk), 1, 0)
            + _dg(ps[i][1].astype(BF16), seq_rows(vnh[hk], b).astype(BF16), 1, 0)
            for i, (b, hk) in enumerate(items)]
    rows = [jnp.concatenate([outs[b * N_KV_HEADS + hk][g * S:(g + 1) * S] for hk in kvs for g in range(Q_PER_KV)],
                            axis=1) for b in range(B)]
    o_ref[...] = jnp.concatenate(rows, axis=0)


def _attn_sample(q, k, v, cache_k, cache_v, sink_rows, n_batch, seq):
    bb = SAMPLE_ATT_BATCH
    tok = lambda w: pl.BlockSpec((bb * seq, w), lambda i: (i, 0))
    cache = pl.BlockSpec((bb, WINDOW, KV_WIDTH), lambda i: (i, 0, 0))
    return pl.pallas_call(
        functools.partial(_attn_sample_kernel, seq=seq),
        grid=(n_batch // bb,),
        in_specs=[tok(ATT_WIDTH), tok(KV_WIDTH), tok(KV_WIDTH), cache, cache, _const_spec(sink_rows.shape)],
        out_specs=tok(ATT_WIDTH),
        out_shape=jax.ShapeDtypeStruct((n_batch * seq, ATT_WIDTH), F32),
        compiler_params=_params("arbitrary"),
        name="attn_sample",
    )(q, k, v, cache_k, cache_v, sink_rows)


def _second_max4(a, b, c, d):
    return jnp.maximum(jnp.maximum(jnp.minimum(a, b), jnp.minimum(c, d)),
                       jnp.minimum(jnp.maximum(a, b), jnp.maximum(c, d)))


def _route(logits_t, bias_col):
    G, E = N_EXPERT_GROUPS, EXPERTS_PER_GROUP
    m = jnp.max(logits_t, axis=0, keepdims=True)
    ex = jnp.exp(logits_t - m)
    probs = ex / jnp.sum(ex, axis=0, keepdims=True)
    sel = probs + bias_col
    p = [probs[e:e + 1, :] for e in range(N_EXPERTS)]
    s = [sel[e:e + 1, :] for e in range(N_EXPERTS)]
    gs = []
    for g in range(G):
        a, b, c, d = s[E * g:E * g + E]
        top1 = jnp.maximum(jnp.maximum(a, b), jnp.maximum(c, d))
        gs.append(top1 + _second_max4(a, b, c, d))
    best = jnp.zeros_like(gs[0], dtype=jnp.int32)
    best_s = gs[0]
    for g in range(1, G):
        upd = gs[g] > best_s
        best = jnp.where(upd, g, best)
        best_s = jnp.where(upd, gs[g], best_s)

    def pick(vals, j):
        out = vals[j]
        for g in range(1, G):
            out = jnp.where(best == g, vals[E * g + j], out)
        return out

    ig = [pick(s, j) for j in range(E)]
    pg = [pick(p, j) for j in range(E)]
    l1 = jnp.zeros_like(best)
    v1 = ig[0]
    for j in range(1, E):
        upd = ig[j] > v1
        l1 = jnp.where(upd, j, l1)
        v1 = jnp.where(upd, ig[j], v1)
    l2 = jnp.full_like(best, -1)
    v2 = jnp.full_like(v1, -jnp.inf)
    for j in range(E):
        upd = (l1 != j) & (ig[j] > v2)
        l2 = jnp.where(upd, j, l2)
        v2 = jnp.where(upd, ig[j], v2)
    zero = jnp.zeros_like(v1)
    w1 = zero
    w2 = zero
    for j in range(E):
        w1 = jnp.where(l1 == j, pg[j], w1)
        w2 = jnp.where(l2 == j, pg[j], w2)
    wsum = w1 + w2
    w1 = w1 / wsum
    w2 = w2 / wsum
    e1 = (best * E + l1).astype(F32)
    e2 = (best * E + l2).astype(F32)
    return jnp.concatenate([e1, e2, w1, w2, zero, zero, zero, zero], axis=0)


def _post_kernel(yn_ref, bonus_ref, g_ref, att_ref, x_ref, woa_ref, wor_ref, gn_ref, ln_ref,
                 wrt_ref, rb_ref, x1_ref, gates_ref):
    rw_out = (yn_ref[...] * gn_ref[0:1, :] + gn_ref[1:2, :] + bonus_ref[...]) * g_ref[...]
    mixed = _bdot(att_ref[...], woa_ref[...]) + _bdot(rw_out, wor_ref[...])
    x1 = _layer_norm(ALPHA * x_ref[...] + mixed, ln_ref[0:1, :], ln_ref[1:2, :])
    x1_ref[...] = x1
    logits_t = _dot3(wrt_ref[...], x1, 1, 1)
    route_t = _route(logits_t, rb_ref[:, 0:1])
    pad = jnp.zeros((LANES - route_t.shape[0], route_t.shape[1]), F32)
    gates_ref[...] = jnp.concatenate([route_t, pad], axis=0).T


def _post(yn, bonus, g, att, x, wts, glob):
    n = x.shape[0]
    tm = ROW_TILE
    row = lambda w: pl.BlockSpec((tm, w), lambda i: (i, 0))
    consts = [wts["woa"], wts["wor"], wts["gn"], wts["ln1"], glob["wrt"], glob["rb"]]
    return pl.pallas_call(
        _post_kernel,
        grid=(n // tm,),
        in_specs=[row(RWKV_WIDTH)] * 3 + [row(ATT_WIDTH), row(D_MODEL)] + [_const_spec(a.shape) for a in consts],
        out_specs=[row(D_MODEL), row(LANES)],
        out_shape=[jax.ShapeDtypeStruct((n, D_MODEL), F32), jax.ShapeDtypeStruct((n, LANES), F32)],
        compiler_params=_params("arbitrary"),
        name="post_mix",
    )(yn, bonus, g, att, x, *consts)


def _moe_positions_kernel(gates_ref, pos_ref, cnt_ref, cnt_scr, offs_scr, carry_scr):
    ph = pl.program_id(0)
    i = pl.program_id(1)
    g = gates_ref[...]
    tm = g.shape[0]
    lane = lax.broadcasted_iota(jnp.int32, g.shape, 1).astype(F32)
    oh1 = (lane == g[:, 0:1]).astype(F32)
    oh2 = (lane == g[:, 1:2]).astype(F32)
    oh = oh1 + oh2

    @pl.when((ph == 0) & (i == 0))
    def _zero():
        cnt_scr[...] = jnp.zeros_like(cnt_scr)

    @pl.when(ph == 0)
    def _count():
        cnt_scr[...] += jnp.sum(oh, axis=0, keepdims=True)
        pos_ref[...] = jnp.zeros_like(pos_ref)

    @pl.when((ph == 1) & (i == 0))
    def _offsets():
        cnt = cnt_scr[...]
        padded = jnp.floor((cnt + (MOE_BLOCK - 1)) * (1.0 / MOE_BLOCK)) * MOE_BLOCK
        r = lax.broadcasted_iota(jnp.int32, (LANES, LANES), 0)
        c = lax.broadcasted_iota(jnp.int32, (LANES, LANES), 1)
        offs_scr[...] = _dot_exact_rhs(padded, (r < c).astype(BF16))
        carry_scr[...] = jnp.zeros_like(carry_scr)
        cnt_ref[...] = cnt

    @pl.when(ph == 1)
    def _rank():
        r = lax.broadcasted_iota(jnp.int32, (tm, tm), 0)
        c = lax.broadcasted_iota(jnp.int32, (tm, tm), 1)
        before = _dg((r > c).astype(BF16), oh.astype(BF16), 1, 0)
        base = before + carry_scr[0:1, :] + offs_scr[0:1, :]
        p1 = jnp.sum(oh1 * base, axis=1, keepdims=True)
        p2 = jnp.sum(oh2 * base, axis=1, keepdims=True)
        pos_ref[...] = jnp.where(lane == 0.0, p1, jnp.where(lane == 1.0, p2, 0.0))
        carry_scr[...] += jnp.sum(oh, axis=0, keepdims=True)


def _moe_positions(gates):
    n = gates.shape[0]
    tm = min(MOE_POS_TILE, n)
    stat = pltpu.VMEM((SUBLANES, LANES), F32)
    return pl.pallas_call(
        _moe_positions_kernel,
        grid=(2, n // tm),
        in_specs=[pl.BlockSpec((tm, LANES), lambda ph, i: (i, 0))],
        out_specs=[pl.BlockSpec((tm, LANES), lambda ph, i: (i * ph, 0)), _const_spec((SUBLANES, LANES))],
        out_shape=[jax.ShapeDtypeStruct((n, LANES), F32), jax.ShapeDtypeStruct((SUBLANES, LANES), F32)],
        scratch_shapes=[stat, stat, stat],
        compiler_params=_params("arbitrary", "arbitrary"),
        name="moe_positions",
    )(gates)


def _moe_dispatch_kernel(ends_ref, pos_ref, x_ref, xs_ref, zero_scr, sem):
    tm = x_ref.shape[0]
    R = zero_scr.shape[0]

    @pl.when(pl.program_id(0) == 0)
    def _zero_tail_blocks():
        zero_scr[...] = jnp.zeros_like(zero_scr)

        def tail_copy(e):
            return pltpu.make_async_copy(zero_scr, xs_ref.at[pl.ds((ends_ref[e] - 1) * R, R), :], sem)

        def non_empty(e):
            return ends_ref[e] > (ends_ref[e - 1] if e else 0)

        n_blocks = xs_ref.shape[0] // R
        n_used = ends_ref[N_EXPERTS - 1]

        def spare_copy(j):
            return pltpu.make_async_copy(zero_scr, xs_ref.at[pl.ds((n_used + j) * R, R), :], sem)

        for e in range(N_EXPERTS):
            pl.when(non_empty(e))(lambda e=e: tail_copy(e).start())
            pl.when(n_used + e < n_blocks)(lambda e=e: spare_copy(e).start())
        for e in range(N_EXPERTS):
            pl.when(non_empty(e))(lambda e=e: tail_copy(e).wait())
            pl.when(n_used + e < n_blocks)(lambda e=e: spare_copy(e).wait())

    def row_copy(r, k):
        return pltpu.make_async_copy(x_ref.at[pl.ds(r, 1), :], xs_ref.at[pl.ds(pos_ref[k, r], 1), :], sem)

    def issue(r, carry):
        row_copy(r, 0).start()
        row_copy(r, 1).start()
        return carry

    lax.fori_loop(0, tm, issue, 0, unroll=DMA_ISSUE_UNROLL)
    for _ in range(N_EXPERTS_PER_TOKEN):
        pltpu.make_async_copy(x_ref, xs_ref.at[pl.ds(0, tm), :], sem).wait()


def _moe_dispatch(x1, pos, ends, n_slots):
    n = x1.shape[0]
    tm = MOE_DMA_TILE
    return pl.pallas_call(
        _moe_dispatch_kernel,
        grid=(n // tm,),
        in_specs=[pl.BlockSpec(memory_space=pltpu.SMEM),
                  pl.BlockSpec((2, tm), lambda i: (0, i), memory_space=pltpu.SMEM),
                  pl.BlockSpec((tm, D_MODEL), lambda i: (i, 0))],
        out_specs=pl.BlockSpec(memory_space=pl.ANY),
        out_shape=jax.ShapeDtypeStruct((n_slots, D_MODEL), F32),
        scratch_shapes=[pltpu.VMEM((MOE_BLOCK, D_MODEL), F32), pltpu.SemaphoreType.DMA],
        compiler_params=_params("arbitrary"),
        name="moe_dispatch",
    )(ends, pos, x1)


def _moe_experts_kernel(blk_expert_ref, n_used_ref, xs_ref, wg_ref, wu_ref, wd_ref, ys_ref):
    del blk_expert_ref
    b = pl.program_id(0)

    @pl.when(b < n_used_ref[0])
    def _compute():
        xb = xs_ref[...].astype(BF16)
        h = jax.nn.silu(_dg(xb, wg_ref[0, 0], 1, 0)) * _dg(xb, wu_ref[0, 0], 1, 0)
        ys_ref[...] = _dg(h.astype(BF16), wd_ref[0, 0], 1, 0)

    @pl.when(b >= n_used_ref[0])
    def _skip():
        ys_ref[...] = jnp.zeros_like(ys_ref)


def _moe_experts(xs, blk_expert, n_used, glob, l):
    n_slots = xs.shape[0]
    R = MOE_BLOCK
    last = lambda b, nu: jnp.minimum(b, nu[0] - 1)
    wspec = lambda s: pl.BlockSpec((1, 1) + s, lambda b, be, nu: (l, be[last(b, nu)], 0, 0))
    return pl.pallas_call(
        _moe_experts_kernel,
        grid_spec=pltpu.PrefetchScalarGridSpec(
            num_scalar_prefetch=2,
            grid=(n_slots // R,),
            in_specs=[pl.BlockSpec((R, D_MODEL), lambda b, be, nu: (last(b, nu), 0)),
                      wspec((D_MODEL, EXPERT_FF)), wspec((D_MODEL, EXPERT_FF)), wspec((EXPERT_FF, D_MODEL))],
            out_specs=pl.BlockSpec((R, D_MODEL), lambda b, be, nu: (b, 0)),
        ),
        out_shape=jax.ShapeDtypeStruct((n_slots, D_MODEL), F32),
        compiler_params=_params("arbitrary"),
        name="moe_experts",
    )(blk_expert, n_used, xs, glob["wg"], glob["wu"], glob["wd"])


def _moe_combine_kernel(pos_ref, x_ref, gates_ref, ys_ref, ln_ref, o_ref, buf, sem):
    tm = x_ref.shape[0]

    def row_copy(r, k):
        return pltpu.make_async_copy(ys_ref.at[pl.ds(pos_ref[k, r], 1), :], buf.at[k, pl.ds(r, 1), :], sem)

    def issue(r, carry):
        row_copy(r, 0).start()
        row_copy(r, 1).start()
        return carry

    lax.fori_loop(0, tm, issue, 0, unroll=DMA_ISSUE_UNROLL)
    for k in range(N_EXPERTS_PER_TOKEN):
        pltpu.make_async_copy(ys_ref.at[pl.ds(0, tm), :], buf.at[k], sem).wait()
    g = gates_ref[...]
    ffn = g[:, 2:3] * buf[0] + g[:, 3:4] * buf[1]
    o_ref[...] = _layer_norm(ALPHA * x_ref[...] + ffn, ln_ref[0:1, :], ln_ref[1:2, :])


def _moe_combine(x1, gates, pos, ys, wts):
    n = x1.shape[0]
    tm = MOE_DMA_TILE
    return pl.pallas_call(
        _moe_combine_kernel,
        grid=(n // tm,),
        in_specs=[pl.BlockSpec((2, tm), lambda i: (0, i), memory_space=pltpu.SMEM),
                  pl.BlockSpec((tm, D_MODEL), lambda i: (i, 0)),
                  pl.BlockSpec((tm, LANES), lambda i: (i, 0)),
                  pl.BlockSpec(memory_space=pl.ANY),
                  _const_spec(wts["ln2"].shape)],
        out_specs=pl.BlockSpec((tm, D_MODEL), lambda i: (i, 0)),
        out_shape=jax.ShapeDtypeStruct((n, D_MODEL), F32),
        scratch_shapes=[pltpu.VMEM((2, tm, D_MODEL), F32), pltpu.SemaphoreType.DMA],
        compiler_params=_params("arbitrary"),
        name="moe_combine",
    )(pos, x1, gates, ys, wts["ln2"])


def _moe(x1, gates, wts, glob, l):
    n = x1.shape[0]
    R = MOE_BLOCK
    n_slots = N_EXPERTS_PER_TOKEN * n + N_EXPERTS * R
    pos_f, cnt = _moe_positions(gates)
    pos = pos_f[:, :N_EXPERTS_PER_TOKEN].astype(jnp.int32).T
    blocks = (cnt[0, :N_EXPERTS].astype(jnp.int32) + (R - 1)) // R
    ends = jnp.cumsum(blocks)
    n_used = ends[-1:]
    blk_expert = jnp.sum(jnp.arange(n_slots // R, dtype=jnp.int32)[:, None] >= ends[None, :], axis=1)
    blk_expert = jnp.minimum(blk_expert, N_EXPERTS - 1).astype(jnp.int32)
    xs = _moe_dispatch(x1, pos, ends.astype(jnp.int32), n_slots)
    ys = _moe_experts(xs, blk_expert, n_used, glob, l)
    return _moe_combine(x1, gates, pos, ys, wts)


def _rows8(vectors, width):
    rows = [v.reshape(1, width).astype(F32) for v in vectors]
    rows.append(jnp.zeros((8 - len(rows), width), F32))
    return jnp.concatenate(rows, axis=0)


def _prep_layer(l, p):
    W = RWKV_WIDTH
    w_l = p["w_in"][l]
    rwc = w_l[:, N_ATT_COLS:]
    mu = p["mu_rwkv"][l]
    o_w, o_k, o_v, o_a, o_g = W, W + D_DECAY_LORA, 2 * W + D_DECAY_LORA, 3 * W + D_DECAY_LORA, 3 * W + 2 * D_DECAY_LORA

    def reorder(t):
        parts = [t[..., 0:W], t[..., o_k:o_k + W], t[..., o_v:o_v + W], t[..., o_g:o_g + D_GATE_LORA],
                 t[..., o_w:o_w + D_DECAY_LORA], t[..., o_a:o_a + D_AAA_LORA]]
        return parts

    w_parts = reorder(rwc)
    mu_parts = reorder(mu)
    if l > 0:
        padw = LANES - D_MV_LORA
        w_parts += [p["w_vres_in"][l - 1], jnp.zeros((D_MODEL, padw), F32)]
        mu_parts += [p["mu_vres"][l - 1], jnp.zeros((padw,), F32)]
    wrw = jnp.concatenate(w_parts, axis=1).astype(BF16)
    mu_row = jnp.concatenate(mu_parts).reshape(1, -1)
    zero = jnp.zeros((D_DECAY_LORA, W), F32)
    wda = jnp.concatenate([jnp.concatenate([p["decay_w2"][l], zero], axis=1),
                           jnp.concatenate([zero, p["aaa_a2"][l]], axis=1)], axis=0).astype(BF16)
    vecs = [p["decay_w0"][l], p["aaa_a0"][l], p["k_k"][l], p["k_a"][l], p["r_k"][l].reshape(W)]
    out = {
        "watt": w_l[:, :N_ATT_COLS].astype(BF16),
        "wrw": wrw,
        "mu": mu_row,
        "wda": wda,
        "g2": p["gate_g2"][l].astype(BF16),
        "woa": p["w_o"][l][:ATT_WIDTH].astype(BF16),
        "wor": p["w_o"][l][ATT_WIDTH:].astype(BF16),
        "gn": _rows8([p["gn_g"][l], p["gn_b"][l]], W),
        "ln1": _rows8([p["ln1_g"][l], p["ln1_b"][l]], D_MODEL),
        "ln2": _rows8([p["ln2_g"][l], p["ln2_b"][l]], D_MODEL),
        "sink_rows": jnp.broadcast_to(p["sinks"][l].reshape(N_Q_HEADS, 1), (N_Q_HEADS, LANES)).astype(F32),
    }
    if l > 0:
        vecs.append(p["vres_v0"][l - 1])
        out["v2"] = jnp.concatenate([p["vres_v2"][l - 1], jnp.zeros((LANES - D_MV_LORA, W), F32)], axis=0).astype(BF16)
    out["vecA"] = _rows8(vecs, W)
    hid = jnp.arange(W) // HEAD_DIM
    out["bd"] = (hid[:, None] == hid[None, :]).astype(BF16)
    return out


def _trunk(x3, shift_prev, cache_k, cache_v, wkv_prev, layer_wts, glob):
    decode = cache_k is not None
    n_batch, seq, _ = x3.shape
    x = x3.reshape(n_batch * seq, D_MODEL)
    new_k, new_v, new_wkv, new_shift = [], [], [], []
    v_first = None
    for l in range(DEPTH):
        wts = layer_wts[l]
        new_shift.append(x.reshape(n_batch, seq, D_MODEL)[:, -1])
        q, ka, va, r, lw, k, v, a, b, g, bonus = _inproj(x, shift_prev[l], wts, v_first, seq)
        if l == 0:
            v_first = v
        if decode:
            ck = cache_k[l].reshape(n_batch, WINDOW, KV_WIDTH)
            cv = cache_v[l].reshape(n_batch, WINDOW, KV_WIDTH)
            att = _attn_sample(q, ka, va, ck, cv, wts["sink_rows"], n_batch, seq)
            new_k.append(ka.reshape(n_batch, seq, N_KV_HEADS, HEAD_DIM))
            new_v.append(va.reshape(n_batch, seq, N_KV_HEADS, HEAD_DIM))
            yn, s_out = _wkv_scan(r, lw, k, v, a, b, (wkv_prev, l), n_batch, seq, seq, SAMPLE_WKV_GROUP)
        else:
            att = _attn_prompt(q, ka, va, wts["sink_rows"], n_batch, seq)
            last = lambda t: t.reshape(n_batch, seq, KV_WIDTH)[:, -WINDOW:].reshape(n_batch, WINDOW, N_KV_HEADS, HEAD_DIM)
            new_k.append(last(ka))
            new_v.append(last(va))
            yn, s_out = _wkv_scan(r, lw, k, v, a, b, None, n_batch, seq, WKV_CHUNK, 1)
        new_wkv.append(s_out)
        x1, gates = _post(yn, bonus, g, att, x, wts, glob)
        x = _moe(x1, gates, wts, glob, l)
    return (x.reshape(n_batch, seq, D_MODEL), jnp.stack(new_k), jnp.stack(new_v), jnp.stack(new_wkv),
            jnp.stack(new_shift))


def kernel(x_prompt, x_sample, cache_k, cache_v, state_wkv, state_shift, w_in, w_vres_in, mu_rwkv, mu_vres, sinks, decay_w0, decay_w2, aaa_a0, aaa_a2, vres_v0, vres_v2, gate_g2, k_k, k_a, r_k, gn_g, gn_b, w_o, ln1_g, ln1_b, w_router, router_bias, w_gate, w_up, w_down, ln2_g, ln2_b):
    p = dict(w_in=w_in, w_vres_in=w_vres_in, mu_rwkv=mu_rwkv, mu_vres=mu_vres, sinks=sinks,
             decay_w0=decay_w0, decay_w2=decay_w2, aaa_a0=aaa_a0, aaa_a2=aaa_a2,
             vres_v0=vres_v0, vres_v2=vres_v2, gate_g2=gate_g2, k_k=k_k, k_a=k_a, r_k=r_k,
             gn_g=gn_g, gn_b=gn_b, w_o=w_o, ln1_g=ln1_g, ln1_b=ln1_b,
             w_gate=w_gate, w_up=w_up, w_down=w_down, ln2_g=ln2_g, ln2_b=ln2_b)
    layer_wts = [_prep_layer(l, p) for l in range(DEPTH)]
    glob = {
        "wrt": w_router.T.astype(F32),
        "rb": jnp.broadcast_to(router_bias.reshape(N_EXPERTS, 1), (N_EXPERTS, LANES)).astype(F32),
        "wg": w_gate.astype(BF16),
        "wu": w_up.astype(BF16),
        "wd": w_down.astype(BF16),
    }
    b_p = x_prompt.shape[0]
    zero_shift = jnp.zeros((DEPTH, b_p, D_MODEL), x_prompt.dtype)
    y_p, k_p, v_p, wkv_p, shift_p = _trunk(x_prompt, zero_shift, None, None, None, layer_wts, glob)
    y_s, k_s, v_s, wkv_s, shift_s = _trunk(x_sample, state_shift, cache_k, cache_v, state_wkv, layer_wts, glob)
    return (y_p, y_s, k_p, v_p, wkv_p, shift_p, k_s, v_s, wkv_s, shift_s)
```

```python
import functools
import math

import jax
import jax.numpy as jnp
from jax import lax
from jax.experimental import pallas as pl
from jax.experimental.pallas import tpu as pltpu

F32 = jnp.float32
BF16 = jnp.bfloat16

D_MODEL = 1024
DEPTH = 4
HEAD_DIM = 64
ATT_WIDTH = 512
RWKV_WIDTH = 512
N_Q_HEADS = 8
N_KV_HEADS = 2
Q_PER_KV = 4
KV_WIDTH = 128
N_ATT_COLS = ATT_WIDTH + 2 * KV_WIDTH
WINDOW = 128
ATT_SCALE = HEAD_DIM ** -0.5
N_RWKV_HEADS = 8
D_DECAY_LORA = 64
D_AAA_LORA = 64
D_GATE_LORA = 128
D_MV_LORA = 32
DECAY_SCALE = math.exp(-0.5)
GN_EPS = 64e-5
LN_EPS = 1e-5
N_EXPERTS = 16
N_EXPERT_GROUPS = 4
EXPERTS_PER_GROUP = 4
EXPERT_FF = 512
ALPHA = (2 * DEPTH) ** 0.25

LANES = 128
SUBLANES = 8
VMEM_LIMIT_BYTES = 56 * 1024 * 1024
ROW_TILE = 256
MOE_BLOCK = 512
MOE_POS_TILE = 512
MOE_DMA_TILE = 256
N_EXPERTS_PER_TOKEN = 2
DMA_ISSUE_UNROLL = 8
WKV_CHUNK = 128
SAMPLE_ATT_BATCH = 8
SAMPLE_WKV_GROUP = 8


def _dg(a, b, ca, cb):
    return lax.dot_general(a, b, (((ca,), (cb,)), ((), ())), preferred_element_type=F32)


def _bdot(a, b):
    return _dg(a.astype(BF16), b.astype(BF16), 1, 0)


def _split2(x):
    hi = x.astype(BF16)
    lo = (x - hi.astype(F32)).astype(BF16)
    return hi, lo


def _dot3(a, b, ca=1, cb=0):
    ah, al = _split2(a)
    bh, bl = _split2(b)
    return _dg(ah, bh, ca, cb) + (_dg(ah, bl, ca, cb) + _dg(al, bh, ca, cb))


def _dot_exact_lhs(m_bf16, x, parts=3):
    acc = None
    rem = x
    for _ in range(parts):
        p = rem.astype(BF16)
        t = _dg(m_bf16, p, 1, 0)
        acc = t if acc is None else acc + t
        rem = rem - p.astype(F32)
    return acc


def _dot_exact_rhs(x, m_bf16, parts=3):
    acc = None
    rem = x
    for _ in range(parts):
        p = rem.astype(BF16)
        t = _dg(p, m_bf16, 1, 0)
        acc = t if acc is None else acc + t
        rem = rem - p.astype(F32)
    return acc


def _layer_norm(z, g, b):
    mu = jnp.mean(z, axis=-1, keepdims=True)
    var = jnp.mean(jnp.square(z - mu), axis=-1, keepdims=True)
    return (z - mu) * lax.rsqrt(var + LN_EPS) * g + b


def _params(*sem):
    return pltpu.CompilerParams(dimension_semantics=sem, vmem_limit_bytes=VMEM_LIMIT_BYTES)


def _const_spec(shape):
    nd = len(shape)
    return pl.BlockSpec(shape, lambda *_: (0,) * nd)


def _inproj_kernel(*refs, first, seq):
    if first:
        (x_ref, st_ref, watt_ref, wrw_ref, mu_ref, wda_ref, g2_ref, vec_ref, bd_ref,
         q_ref, ka_ref, va_ref, r_ref, lw_ref, k_ref, v_ref, a_ref, b_ref, g_ref, bonus_ref, carry_ref) = refs
    else:
        (x_ref, st_ref, watt_ref, wrw_ref, mu_ref, wda_ref, g2_ref, vec_ref, bd_ref, vfirst_ref, v2_ref,
         q_ref, ka_ref, va_ref, r_ref, lw_ref, k_ref, v_ref, a_ref, b_ref, g_ref, bonus_ref, carry_ref) = refs
    i = pl.program_id(0)
    x = x_ref[...]
    tm = x.shape[0]
    qkv = _dg(x.astype(BF16), watt_ref[...], 1, 0)
    q_ref[...] = qkv[:, :ATT_WIDTH]
    ka_ref[...] = qkv[:, ATT_WIDTH:ATT_WIDTH + KV_WIDTH]
    va_ref[...] = qkv[:, ATT_WIDTH + KV_WIDTH:]

    whole_tiles = seq >= tm
    st = jnp.broadcast_to(st_ref[0], (2 * SUBLANES, D_MODEL)) if whole_tiles else st_ref[...]
    pe = _dg(jnp.concatenate([x, st], axis=0).astype(BF16), wrw_ref[...], 1, 0)
    pc = pe[:tm]
    pst = pe[tm:]
    rowid = lax.broadcasted_iota(jnp.int32, pc.shape, 0)
    pp = pltpu.roll(pc, 1, 0)
    if whole_tiles:
        @pl.when(i == 0)
        def _init():
            carry_ref[...] = jnp.zeros_like(carry_ref)

        prev0 = jnp.where(i % (seq // tm) == 0, pst[0:1], carry_ref[SUBLANES - 1:SUBLANES, :])
        pp = jnp.where(rowid == 0, prev0, pp)
        carry_ref[...] = pc[tm - SUBLANES:]
    else:
        n_st = tm // seq
        er = lax.broadcasted_iota(jnp.int32, (tm, n_st), 0)
        ec = lax.broadcasted_iota(jnp.int32, (tm, n_st), 1)
        expand = (er == ec * seq).astype(BF16)
        pp = jnp.where(rowid % seq == 0, _dot_exact_lhs(expand, pst), pp)
    rw = pc + (pp - pc) * mu_ref[...]
    W = RWKV_WIDTH
    r = rw[:, 0:W]
    k = rw[:, W:2 * W]
    v = rw[:, 2 * W:3 * W]
    gl = rw[:, 3 * W:3 * W + D_GATE_LORA]
    wa = rw[:, 3 * W + D_GATE_LORA:3 * W + 2 * LANES]
    w0 = vec_ref[0:1, :]
    a0 = vec_ref[1:2, :]
    k_k = vec_ref[2:3, :]
    k_a = vec_ref[3:4, :]
    r_k = vec_ref[4:5, :]

    lane = lax.broadcasted_iota(jnp.int32, wa.shape, 1)
    wa_t = jnp.where(lane < D_DECAY_LORA, jnp.tanh(wa), wa)
    da = _bdot(wa_t, wda_ref[...])
    lw = -DECAY_SCALE * jax.nn.sigmoid(w0 + da[:, :W])
    a = jax.nn.sigmoid(a0 + da[:, W:])
    g = _bdot(jax.nn.sigmoid(gl), g2_ref[...])
    if not first:
        mv = rw[:, 3 * W + 2 * LANES:]
        v0 = vec_ref[5:6, :]
        v = v + (vfirst_ref[...] - v) * jax.nn.sigmoid(v0 + _bdot(mv, v2_ref[...]))
    bd = bd_ref[...]
    kk = k * k_k
    ssq = _dot_exact_rhs(kk * kk, bd)
    kk = kk / jnp.maximum(jnp.sqrt(ssq), 1e-12)
    k = k * (1.0 + (a - 1.0) * k_a)
    bonus = _dot_exact_rhs(r * k * r_k, bd) * v

    r_ref[...] = r
    lw_ref[...] = lw
    k_ref[...] = k
    v_ref[...] = v
    a_ref[...] = -kk
    b_ref[...] = kk * a
    g_ref[...] = g
    bonus_ref[...] = bonus


def _inproj(x, shift_state, wts, v_first, seq):
    n = x.shape[0]
    first = v_first is None
    tm = ROW_TILE
    row = lambda w: pl.BlockSpec((tm, w), lambda i: (i, 0))
    if seq >= tm:
        assert seq % tm == 0
        st = shift_state.reshape(-1, 1, D_MODEL)
        st_spec = pl.BlockSpec((1, 1, D_MODEL), lambda i: (i // (seq // tm), 0, 0))
    else:
        assert tm % seq == 0
        st = shift_state
        st_spec = pl.BlockSpec((tm // seq, D_MODEL), lambda i: (i, 0))
    ins = [x, st, wts["watt"], wts["wrw"], wts["mu"], wts["wda"], wts["g2"], wts["vecA"], wts["bd"]]
    in_specs = [row(D_MODEL), st_spec] + [_const_spec(a.shape) for a in ins[2:]]
    if not first:
        ins += [v_first, wts["v2"]]
        in_specs += [row(RWKV_WIDTH), _const_spec(wts["v2"].shape)]
    widths = [ATT_WIDTH, KV_WIDTH, KV_WIDTH] + [RWKV_WIDTH] * 8
    return pl.pallas_call(
        functools.partial(_inproj_kernel, first=first, seq=seq),
        grid=(n // tm,),
        in_specs=in_specs,
        out_specs=[row(w) for w in widths],
        out_shape=[jax.ShapeDtypeStruct((n, w), F32) for w in widths],
        scratch_shapes=[pltpu.VMEM((SUBLANES, wts["wrw"].shape[1]), F32)],
        compiler_params=_params("arbitrary"),
        name="inproj",
    )(*ins)


def _dot3s(a_sp, b_sp, ca=1, cb=0):
    (ah, al), (bh, bl) = a_sp, b_sp
    return _dg(ah, bh, ca, cb) + (_dg(ah, bl, ca, cb) + _dg(al, bh, ca, cb))


def _dot1s(a_sp, b_sp, ca=1, cb=0):
    return _dg(a_sp[0], b_sp[0], ca, cb)


def _wkv_kernel(*refs, group, seq_chunk, has_state):
    if has_state:
        r_ref, lw_ref, k_ref, v_ref, a_ref, b_ref, s0_ref, y_ref, sout_ref, s_scr = refs
    else:
        r_ref, lw_ref, k_ref, v_ref, a_ref, b_ref, y_ref, sout_ref, s_scr = refs
    G, Ls = group, seq_chunk
    L = G * Ls
    H, N = N_RWKV_HEADS, HEAD_DIM
    c_idx = pl.program_id(1)

    @pl.when(c_idx == 0)
    def _init():
        if has_state:
            s_scr[...] = s0_ref[...]
        else:
            s_scr[...] = jnp.zeros_like(s_scr)

    row = lax.broadcasted_iota(jnp.int32, (L, L), 0)
    col = lax.broadcasted_iota(jnp.int32, (L, L), 1)
    same = (row // Ls) == (col // Ls)
    incl = same & (row >= col)
    strict = same & (row > col)
    eye = (row == col).astype(F32)
    row2 = lax.broadcasted_iota(jnp.int32, (2 * L, 2 * L), 0)
    col2 = lax.broadcasted_iota(jnp.int32, (2 * L, 2 * L), 1)
    t_q = row2 % L
    t_k = col2 % L
    mask2 = ((t_q // Ls) == (t_k // Ls)) & ((t_q > t_k) | ((row2 >= L) & (t_q == t_k)))

    lw_all = lw_ref[...]
    cum_all = _dot_exact_lhs(incl.astype(BF16), lw_all)
    if G == 1:
        tot_all = jnp.broadcast_to(cum_all[L - 1:L, :], cum_all.shape)
    else:
        tot_all = _dot_exact_lhs(same.astype(BF16), lw_all)
    e_end = jnp.exp(tot_all - cum_all)
    etot_all = jnp.exp(tot_all)
    a_all = a_ref[...]
    r_all = r_ref[...]
    b_all = b_ref[...]
    k_all = k_ref[...]
    v_all = v_ref[...]
    ar_all = jnp.concatenate([a_all * jnp.exp(cum_all - lw_all), r_all * jnp.exp(cum_all)], axis=0)
    ar_sp = _split2(ar_all)
    if G == 1:
        mid = cum_all[L // 2 - 1:L // 2, :]
        e_neg = jnp.exp(mid - cum_all)
        arc_sp = _split2(jnp.concatenate([a_all * jnp.exp(cum_all - lw_all - mid), r_all * jnp.exp(cum_all - mid)],
                                         axis=0))
    else:
        e_neg = jnp.exp(-cum_all)
        arc_sp = ar_sp
    bk_sp = _split2(jnp.concatenate([b_all * e_neg, k_all * e_neg], axis=0))
    bkh_sp = _split2(jnp.concatenate([b_all * e_end, k_all * e_end], axis=0))
    heads = range(H)
    hsl = lambda t, h: t[:, h * N:(h + 1) * N]
    hsp = lambda sp, h: (hsl(sp[0], h), hsl(sp[1], h))

    m = [jnp.where(mask2, _dg(hsl(arc_sp[0], h), hsl(bk_sp[0], h), 1, 1), 0.0) for h in heads]
    m_sp = [_split2(t) for t in m]
    a_ab = [t[:L, :L] for t in m]
    d = [eye + jnp.where((row // 2) == (col // 2), t, 0.0) for t in a_ab]
    n = 4
    while n <= Ls:
        off = ((row // n) == (col // n)) & ((row // (n // 2)) != (col // (n // 2)))
        d_b = [t.astype(BF16) for t in d]
        dn = [_dg(t, jnp.where(off, s, 0.0).astype(BF16), 1, 0) for t, s in zip(d_b, a_ab)]
        d = [t + _dg(p.astype(BF16), tb, 1, 0) for t, p, tb in zip(d, dn, d_b)]
        n *= 2
    resid = [eye - t + _dot3s((s[0][:L, :L], s[1][:L, :L]), _split2(t)) for t, s in zip(d, m_sp)]
    t_inv = [t + _dg(t.astype(BF16), r.astype(BF16), 1, 0) for t, r in zip(d, resid)]

    s0 = [[s_scr[g, h] for h in heads] for g in range(G)]
    s0_sp = [[_split2(s0[g][h]) for h in heads] for g in range(G)]

    def seq_rows(sp, g, h):
        if G == 1:
            return hsp(sp, h)
        return tuple(jnp.concatenate([hsl(p, h)[g * Ls:(g + 1) * Ls], hsl(p, h)[L + g * Ls:L + (g + 1) * Ls]], axis=0)
                     for p in sp)

    st = [[_dot1s(seq_rows(ar_sp, g, h), s0_sp[g][h], 1, 1) for h in heads] for g in range(G)]
    rhs_st = [jnp.concatenate([st[g][h][:Ls] for g in range(G)], axis=0) for h in heads]
    y_st = [jnp.concatenate([st[g][h][Ls:] for g in range(G)], axis=0) for h in heads]

    v_h = [hsl(v_all, h) for h in heads]
    zeros = jnp.zeros((L, N), F32)
    m_top = [(s[0][:L], s[1][:L]) for s in m_sp]
    m_bot = [(s[0][L:], s[1][L:]) for s in m_sp]
    rhs = [rhs_st[h] + _dot1s(m_top[h], _split2(jnp.concatenate([zeros, v_h[h]], axis=0))) for h in heads]
    u = [_dot1s(_split2(t_inv[h]), _split2(rhs[h])) for h in heads]
    uv_sp = [_split2(jnp.concatenate([u[h], v_h[h]], axis=0)) for h in heads]
    y = [y_st[h] + _dg(m_bot[h][0], uv_sp[h][0], 1, 0) for h in heads]
    for g in range(G):
        for h in heads:
            if G == 1:
                uv_g = uv_sp[h]
            else:
                uv_g = tuple(jnp.concatenate([p[g * Ls:(g + 1) * Ls], p[L + g * Ls:L + (g + 1) * Ls]], axis=0)
                             for p in uv_sp[h])
            upd = _dot1s(uv_g, seq_rows(bkh_sp, g, h), 0, 0)
            s_scr[g, h] = s0[g][h] * hsl(etot_all, h)[g * Ls:g * Ls + 1] + upd
    ys = []
    for h in heads:
        mu = jnp.mean(y[h], axis=-1, keepdims=True)
        var = jnp.mean(jnp.square(y[h] - mu), axis=-1, keepdims=True)
        ys.append((y[h] - mu) * lax.rsqrt(var + GN_EPS))
    y_ref[...] = jnp.concatenate(ys, axis=1)

    @pl.when(c_idx == pl.num_programs(1) - 1)
    def _fin():
        sout_ref[...] = s_scr[...]


def _wkv_scan(r, lw, k, v, a, b, state, n_seq, seq, seq_chunk, group):
    n_chunks = seq // seq_chunk
    assert group == 1 or n_chunks == 1
    has_state = state is not None
    rows = group * seq_chunk
    tok_spec = pl.BlockSpec((rows, RWKV_WIDTH), lambda i, c: (i * n_chunks + c, 0))
    st_spec = pl.BlockSpec((group, N_RWKV_HEADS, HEAD_DIM, HEAD_DIM), lambda i, c: (i, 0, 0, 0))
    in_specs = [tok_spec] * 6
    args = (r, lw, k, v, a, b)
    if has_state:
        states, layer = state
        in_specs.append(pl.BlockSpec((None, group, N_RWKV_HEADS, HEAD_DIM, HEAD_DIM),
                                     lambda i, c: (layer, i, 0, 0, 0)))
        args += (states,)
    return pl.pallas_call(
        functools.partial(_wkv_kernel, group=group, seq_chunk=seq_chunk, has_state=has_state),
        grid=(n_seq // group, n_chunks),
        in_specs=in_specs,
        out_specs=[tok_spec, st_spec],
        out_shape=[jax.ShapeDtypeStruct((n_seq * seq, RWKV_WIDTH), F32),
                   jax.ShapeDtypeStruct((n_seq, N_RWKV_HEADS, HEAD_DIM, HEAD_DIM), F32)],
        scratch_shapes=[pltpu.VMEM((group, N_RWKV_HEADS, HEAD_DIM, HEAD_DIM), F32)],
        compiler_params=_params("arbitrary", "arbitrary"),
        name="wkv_scan",
    )(*args)


def _sink_softmax(s, sink):
    m = sink
    for t in s:
        m = jnp.maximum(m, jnp.max(t, axis=-1, keepdims=True))
    es = [jnp.exp(t - m) for t in s]
    den = jnp.exp(sink - m)
    for e in es:
        den = den + jnp.sum(e, axis=-1, keepdims=True)
    return [e / den for e in es]


def _attn_prompt_kernel(q_ref, kc_ref, kp_ref, vc_ref, vp_ref, sink_ref, o_ref):
    L = WINDOW
    n = pl.program_id(1)
    q = q_ref[...].astype(BF16)
    kw = jnp.concatenate([kp_ref[...], kc_ref[...]], axis=0).astype(BF16)
    vw = jnp.concatenate([vp_ref[...], vc_ref[...]], axis=0).astype(BF16)
    qi = lax.broadcasted_iota(jnp.int32, (L, 2 * L), 0)
    kj = lax.broadcasted_iota(jnp.int32, (L, 2 * L), 1)
    diff = qi + L - kj
    mask = (diff >= 0) & (diff < WINDOW) & ((kj >= L) | (n > 0))
    lane = lax.broadcasted_iota(jnp.int32, vw.shape, 1)
    one = jnp.ones_like(vw)
    vext = [jnp.where(lane < HEAD_DIM, vw, one), jnp.where(lane >= HEAD_DIM, vw, one)]
    heads = range(N_Q_HEADS)
    hsl = lambda t, h: t[:, h * HEAD_DIM:(h + 1) * HEAD_DIM]
    ss = [jnp.where(mask, _dg(hsl(q, h), hsl(kw, h // Q_PER_KV), 1, 1) * ATT_SCALE, -jnp.inf) for h in heads]
    sinks = [sink_ref[h:h + 1, 0:1] for h in heads]
    ms = [jnp.maximum(jnp.max(ss[h], axis=-1, keepdims=True), sinks[h]) for h in heads]
    es = [jnp.exp(ss[h] - ms[h]).astype(BF16) for h in heads]
    oes = [_dg(es[h], vext[h // Q_PER_KV], 1, 0) for h in heads]
    outs = []
    for h in heads:
        hk = h // Q_PER_KV
        rs = oes[h][:, (1 - hk) * HEAD_DIM:(1 - hk) * HEAD_DIM + 1]
        outs.append(hsl(oes[h], hk) / (rs + jnp.exp(sinks[h] - ms[h])))
    o_ref[...] = jnp.concatenate(outs, axis=1)


def _attn_prompt(q, k, v, sink_rows, n_batch, seq):
    nb = seq // WINDOW
    cur = lambda w: pl.BlockSpec((WINDOW, w), lambda b, n: (b * nb + n, 0))
    prev = lambda w: pl.BlockSpec((WINDOW, w), lambda b, n: (b * nb + jnp.maximum(n - 1, 0), 0))
    return pl.pallas_call(
        _attn_prompt_kernel,
        grid=(n_batch, nb),
        in_specs=[cur(ATT_WIDTH), cur(KV_WIDTH), prev(KV_WIDTH), cur(KV_WIDTH), prev(KV_WIDTH),
                  _const_spec(sink_rows.shape)],
        out_specs=cur(ATT_WIDTH),
        out_shape=jax.ShapeDtypeStruct((n_batch * seq, ATT_WIDTH), F32),
        compiler_params=_params("arbitrary", "arbitrary"),
        name="attn_prompt",
    )(q, k, k, v, v, sink_rows)


def _attn_sample_kernel(q_ref, kn_ref, vn_ref, ck_ref, cv_ref, sink_ref, o_ref, *, seq):
    S = seq
    W = WINDOW
    B = SAMPLE_ATT_BATCH
    R = Q_PER_KV * S
    tq1 = lax.broadcasted_iota(jnp.int32, (R, W), 0) % S
    kj1 = lax.broadcasted_iota(jnp.int32, (R, W), 1)
    mask_cache = kj1 > tq1
    tq2 = lax.broadcasted_iota(jnp.int32, (R, S), 0) % S
    kj2 = lax.broadcasted_iota(jnp.int32, (R, S), 1)
    mask_new = kj2 <= tq2
    kvs = range(N_KV_HEADS)
    hsl = lambda t, h: t[:, h * HEAD_DIM:(h + 1) * HEAD_DIM]
    group = lambda hk: range(hk * Q_PER_KV, (hk + 1) * Q_PER_KV)
    seq_rows = lambda t, b: t[b * S:(b + 1) * S]
    q = q_ref[...]
    kn = kn_ref[...]
    vn = vn_ref[...]
    qh = [hsl(q, h) for h in range(N_Q_HEADS)]
    knh = [hsl(kn, hk) for hk in kvs]
    vnh = [hsl(vn, hk) for hk in kvs]
    sinks = [jnp.concatenate([jnp.broadcast_to(sink_ref[h:h + 1, 0:1], (S, 1)) for h in group(hk)], axis=0)
             for hk in kvs]
    items = [(b, hk) for b in range(B) for hk in kvs]
    qs = [jnp.concatenate([seq_rows(qh[h], b) for h in group(hk)], axis=0).astype(BF16) for b, hk in items]
    ck = [ck_ref[b].astype(BF16) for b in range(B)]
    cv = [cv_ref[b].astype(BF16) for b in range(B)]
    s1 = [jnp.where(mask_cache, _dg(qs[i], hsl(ck[b], hk), 1, 1) * ATT_SCALE, -jnp.inf)
          for i, (b, hk) in enumerate(items)]
    s2 = [jnp.where(mask_new, _dg(qs[i], seq_rows(knh[hk], b).astype(BF16), 1, 1) * ATT_SCALE, -jnp.inf)
          for i, (b, hk) in enumerate(items)]
    ps = [_sink_softmax([s1[i], s2[i]], sinks[hk]) for i, (b, hk) in enumerate(items)]
    outs = [_dg(ps[i][0].astype(BF16), hsl(cv[b], hk), 1, 0)
            + _dg(ps[i][1].astype(BF16), seq_rows(vnh[hk], b).astype(BF16), 1, 0)
            for i, (b, hk) in enumerate(items)]
    rows = [jnp.concatenate([outs[b * N_KV_HEADS + hk][g * S:(g + 1) * S] for hk in kvs for g in range(Q_PER_KV)],
                            axis=1) for b in range(B)]
    o_ref[...] = jnp.concatenate(rows, axis=0)


def _attn_sample(q, k, v, cache_k, cache_v, sink_rows, n_batch, seq):
    bb = SAMPLE_ATT_BATCH
    tok = lambda w: pl.BlockSpec((bb * seq, w), lambda i: (i, 0))
    cache = pl.BlockSpec((bb, WINDOW, KV_WIDTH), lambda i: (i, 0, 0))
    return pl.pallas_call(
        functools.partial(_attn_sample_kernel, seq=seq),
        grid=(n_batch // bb,),
        in_specs=[tok(ATT_WIDTH), tok(KV_WIDTH), tok(KV_WIDTH), cache, cache, _const_spec(sink_rows.shape)],
        out_specs=tok(ATT_WIDTH),
        out_shape=jax.ShapeDtypeStruct((n_batch * seq, ATT_WIDTH), F32),
        compiler_params=_params("arbitrary"),
        name="attn_sample",
    )(q, k, v, cache_k, cache_v, sink_rows)


def _second_max4(a, b, c, d):
    return jnp.maximum(jnp.maximum(jnp.minimum(a, b), jnp.minimum(c, d)),
                       jnp.minimum(jnp.maximum(a, b), jnp.maximum(c, d)))


def _route(logits_t, bias_col):
    G, E = N_EXPERT_GROUPS, EXPERTS_PER_GROUP
    m = jnp.max(logits_t, axis=0, keepdims=True)
    ex = jnp.exp(logits_t - m)
    probs = ex / jnp.sum(ex, axis=0, keepdims=True)
    sel = probs + bias_col
    p = [probs[e:e + 1, :] for e in range(N_EXPERTS)]
    s = [sel[e:e + 1, :] for e in range(N_EXPERTS)]
    gs = []
    for g in range(G):
        a, b, c, d = s[E * g:E * g + E]
        top1 = jnp.maximum(jnp.maximum(a, b), jnp.maximum(c, d))
        gs.append(top1 + _second_max4(a, b, c, d))
    best = jnp.zeros_like(gs[0], dtype=jnp.int32)
    best_s = gs[0]
    for g in range(1, G):
        upd = gs[g] > best_s
        best = jnp.where(upd, g, best)
        best_s = jnp.where(upd, gs[g], best_s)

    def pick(vals, j):
        out = vals[j]
        for g in range(1, G):
            out = jnp.where(best == g, vals[E * g + j], out)
        return out

    ig = [pick(s, j) for j in range(E)]
    pg = [pick(p, j) for j in range(E)]
    l1 = jnp.zeros_like(best)
    v1 = ig[0]
    for j in range(1, E):
        upd = ig[j] > v1
        l1 = jnp.where(upd, j, l1)
        v1 = jnp.where(upd, ig[j], v1)
    l2 = jnp.full_like(best, -1)
    v2 = jnp.full_like(v1, -jnp.inf)
    for j in range(E):
        upd = (l1 != j) & (ig[j] > v2)
        l2 = jnp.where(upd, j, l2)
        v2 = jnp.where(upd, ig[j], v2)
    zero = jnp.zeros_like(v1)
    w1 = zero
    w2 = zero
    for j in range(E):
        w1 = jnp.where(l1 == j, pg[j], w1)
        w2 = jnp.where(l2 == j, pg[j], w2)
    wsum = w1 + w2
    w1 = w1 / wsum
    w2 = w2 / wsum
    e1 = (best * E + l1).astype(F32)
    e2 = (best * E + l2).astype(F32)
    return jnp.concatenate([e1, e2, w1, w2, zero, zero, zero, zero], axis=0)


def _post_kernel(yn_ref, bonus_ref, g_ref, att_ref, x_ref, woa_ref, wor_ref, gn_ref, ln_ref,
                 wrt_ref, rb_ref, x1_ref, gates_ref):
    rw_out = (yn_ref[...] * gn_ref[0:1, :] + gn_ref[1:2, :] + bonus_ref[...]) * g_ref[...]
    mixed = _bdot(att_ref[...], woa_ref[...]) + _bdot(rw_out, wor_ref[...])
    x1 = _layer_norm(ALPHA * x_ref[...] + mixed, ln_ref[0:1, :], ln_ref[1:2, :])
    x1_ref[...] = x1
    logits_t = _dot3(wrt_ref[...], x1, 1, 1)
    route_t = _route(logits_t, rb_ref[:, 0:1])
    pad = jnp.zeros((LANES - route_t.shape[0], route_t.shape[1]), F32)
    gates_ref[...] = jnp.concatenate([route_t, pad], axis=0).T


def _post(yn, bonus, g, att, x, wts, glob):
    n = x.shape[0]
    tm = ROW_TILE
    row = lambda w: pl.BlockSpec((tm, w), lambda i: (i, 0))
    consts = [wts["woa"], wts["wor"], wts["gn"], wts["ln1"], glob["wrt"], glob["rb"]]
    return pl.pallas_call(
        _post_kernel,
        grid=(n // tm,),
        in_specs=[row(RWKV_WIDTH)] * 3 + [row(ATT_WIDTH), row(D_MODEL)] + [_const_spec(a.shape) for a in consts],
        out_specs=[row(D_MODEL), row(LANES)],
        out_shape=[jax.ShapeDtypeStruct((n, D_MODEL), F32), jax.ShapeDtypeStruct((n, LANES), F32)],
        compiler_params=_params("arbitrary"),
        name="post_mix",
    )(yn, bonus, g, att, x, *consts)


def _moe_positions_kernel(gates_ref, pos_ref, cnt_ref, cnt_scr, offs_scr, carry_scr):
    ph = pl.program_id(0)
    i = pl.program_id(1)
    g = gates_ref[...]
    tm = g.shape[0]
    lane = lax.broadcasted_iota(jnp.int32, g.shape, 1).astype(F32)
    oh1 = (lane == g[:, 0:1]).astype(F32)
    oh2 = (lane == g[:, 1:2]).astype(F32)
    oh = oh1 + oh2

    @pl.when((ph == 0) & (i == 0))
    def _zero():
        cnt_scr[...] = jnp.zeros_like(cnt_scr)

    @pl.when(ph == 0)
    def _count():
        cnt_scr[...] += jnp.sum(oh, axis=0, keepdims=True)
        pos_ref[...] = jnp.zeros_like(pos_ref)

    @pl.when((ph == 1) & (i == 0))
    def _offsets():
        cnt = cnt_scr[...]
        padded = jnp.floor((cnt + (MOE_BLOCK - 1)) * (1.0 / MOE_BLOCK)) * MOE_BLOCK
        r = lax.broadcasted_iota(jnp.int32, (LANES, LANES), 0)
        c = lax.broadcasted_iota(jnp.int32, (LANES, LANES), 1)
        offs_scr[...] = _dot_exact_rhs(padded, (r < c).astype(BF16))
        carry_scr[...] = jnp.zeros_like(carry_scr)
        cnt_ref[...] = cnt

    @pl.when(ph == 1)
    def _rank():
        r = lax.broadcasted_iota(jnp.int32, (tm, tm), 0)
        c = lax.broadcasted_iota(jnp.int32, (tm, tm), 1)
        before = _dg((r > c).astype(BF16), oh.astype(BF16), 1, 0)
        base = before + carry_scr[0:1, :] + offs_scr[0:1, :]
        p1 = jnp.sum(oh1 * base, axis=1, keepdims=True)
        p2 = jnp.sum(oh2 * base, axis=1, keepdims=True)
        pos_ref[...] = jnp.where(lane == 0.0, p1, jnp.where(lane == 1.0, p2, 0.0))
        carry_scr[...] += jnp.sum(oh, axis=0, keepdims=True)


def _moe_positions(gates):
    n = gates.shape[0]
    tm = min(MOE_POS_TILE, n)
    stat = pltpu.VMEM((SUBLANES, LANES), F32)
    return pl.pallas_call(
        _moe_positions_kernel,
        grid=(2, n // tm),
        in_specs=[pl.BlockSpec((tm, LANES), lambda ph, i: (i, 0))],
        out_specs=[pl.BlockSpec((tm, LANES), lambda ph, i: (i * ph, 0)), _const_spec((SUBLANES, LANES))],
        out_shape=[jax.ShapeDtypeStruct((n, LANES), F32), jax.ShapeDtypeStruct((SUBLANES, LANES), F32)],
        scratch_shapes=[stat, stat, stat],
        compiler_params=_params("arbitrary", "arbitrary"),
        name="moe_positions",
    )(gates)


def _moe_dispatch_kernel(ends_ref, pos_ref, x_ref, xs_ref, zero_scr, sem):
    tm = x_ref.shape[0]
    R = zero_scr.shape[0]

    @pl.when(pl.program_id(0) == 0)
    def _zero_tail_blocks():
        zero_scr[...] = jnp.zeros_like(zero_scr)

        def tail_copy(e):
            return pltpu.make_async_copy(zero_scr, xs_ref.at[pl.ds((ends_ref[e] - 1) * R, R), :], sem)

        def non_empty(e):
            return ends_ref[e] > (ends_ref[e - 1] if e else 0)

        n_blocks = xs_ref.shape[0] // R
        n_used = ends_ref[N_EXPERTS - 1]

        def spare_copy(j):
            return pltpu.make_async_copy(zero_scr, xs_ref.at[pl.ds((n_used + j) * R, R), :], sem)

        for e in range(N_EXPERTS):
            pl.when(non_empty(e))(lambda e=e: tail_copy(e).start())
            pl.when(n_used + e < n_blocks)(lambda e=e: spare_copy(e).start())
        for e in range(N_EXPERTS):
            pl.when(non_empty(e))(lambda e=e: tail_copy(e).wait())
            pl.when(n_used + e < n_blocks)(lambda e=e: spare_copy(e).wait())

    def row_copy(r, k):
        return pltpu.make_async_copy(x_ref.at[pl.ds(r, 1), :], xs_ref.at[pl.ds(pos_ref[k, r], 1), :], sem)

    def issue(r, carry):
        row_copy(r, 0).start()
        row_copy(r, 1).start()
        return carry

    lax.fori_loop(0, tm, issue, 0, unroll=DMA_ISSUE_UNROLL)
    for _ in range(N_EXPERTS_PER_TOKEN):
        pltpu.make_async_copy(x_ref, xs_ref.at[pl.ds(0, tm), :], sem).wait()


def _moe_dispatch(x1, pos, ends, n_slots):
    n = x1.shape[0]
    tm = MOE_DMA_TILE
    return pl.pallas_call(
        _moe_dispatch_kernel,
        grid=(n // tm,),
        in_specs=[pl.BlockSpec(memory_space=pltpu.SMEM),
                  pl.BlockSpec((2, tm), lambda i: (0, i), memory_space=pltpu.SMEM),
                  pl.BlockSpec((tm, D_MODEL), lambda i: (i, 0))],
        out_specs=pl.BlockSpec(memory_space=pl.ANY),
        out_shape=jax.ShapeDtypeStruct((n_slots, D_MODEL), F32),
        scratch_shapes=[pltpu.VMEM((MOE_BLOCK, D_MODEL), F32), pltpu.SemaphoreType.DMA],
        compiler_params=_params("arbitrary"),
        name="moe_dispatch",
    )(ends, pos, x1)


def _moe_experts_kernel(blk_expert_ref, n_used_ref, xs_ref, wg_ref, wu_ref, wd_ref, ys_ref, wg_b, wu_b, wd_b):
    b = pl.program_id(0)
    used = b < n_used_ref[0]
    new_expert = (b == 0) | (blk_expert_ref[b] != blk_expert_ref[jnp.maximum(b - 1, 0)])

    @pl.when(used & new_expert)
    def _cast_weights():
        wg_b[...] = wg_ref[0, 0].astype(BF16)
        wu_b[...] = wu_ref[0, 0].astype(BF16)
        wd_b[...] = wd_ref[0, 0].astype(BF16)

    @pl.when(used)
    def _compute():
        xb = xs_ref[...].astype(BF16)
        h = jax.nn.silu(_dg(xb, wg_b[...], 1, 0)) * _dg(xb, wu_b[...], 1, 0)
        ys_ref[...] = _dg(h.astype(BF16), wd_b[...], 1, 0)

    @pl.when(b >= n_used_ref[0])
    def _skip():
        ys_ref[...] = jnp.zeros_like(ys_ref)


def _moe_experts(xs, blk_expert, n_used, glob, l):
    n_slots = xs.shape[0]
    R = MOE_BLOCK
    last = lambda b, nu: jnp.minimum(b, nu[0] - 1)
    wspec = lambda s: pl.BlockSpec((1, 1) + s, lambda b, be, nu: (l, be[last(b, nu)], 0, 0))
    return pl.pallas_call(
        _moe_experts_kernel,
        grid_spec=pltpu.PrefetchScalarGridSpec(
            num_scalar_prefetch=2,
            grid=(n_slots // R,),
            in_specs=[pl.BlockSpec((R, D_MODEL), lambda b, be, nu: (last(b, nu), 0)),
                      wspec((D_MODEL, EXPERT_FF)), wspec((D_MODEL, EXPERT_FF)), wspec((EXPERT_FF, D_MODEL))],
            out_specs=pl.BlockSpec((R, D_MODEL), lambda b, be, nu: (b, 0)),
            scratch_shapes=[pltpu.VMEM((D_MODEL, EXPERT_FF), BF16), pltpu.VMEM((D_MODEL, EXPERT_FF), BF16),
                            pltpu.VMEM((EXPERT_FF, D_MODEL), BF16)],
        ),
        out_shape=jax.ShapeDtypeStruct((n_slots, D_MODEL), F32),
        compiler_params=_params("arbitrary"),
        name="moe_experts",
    )(blk_expert, n_used, xs, glob["wg"], glob["wu"], glob["wd"])


def _moe_combine_kernel(pos_ref, x_ref, gates_ref, ys_ref, ln_ref, o_ref, buf, sem):
    tm = x_ref.shape[0]

    def row_copy(r, k):
        return pltpu.make_async_copy(ys_ref.at[pl.ds(pos_ref[k, r], 1), :], buf.at[k, pl.ds(r, 1), :], sem)

    def issue(r, carry):
        row_copy(r, 0).start()
        row_copy(r, 1).start()
        return carry

    lax.fori_loop(0, tm, issue, 0, unroll=DMA_ISSUE_UNROLL)
    for k in range(N_EXPERTS_PER_TOKEN):
        pltpu.make_async_copy(ys_ref.at[pl.ds(0, tm), :], buf.at[k], sem).wait()
    g = gates_ref[...]
    ffn = g[:, 2:3] * buf[0] + g[:, 3:4] * buf[1]
    o_ref[...] = _layer_norm(ALPHA * x_ref[...] + ffn, ln_ref[0:1, :], ln_ref[1:2, :])


def _moe_combine(x1, gates, pos, ys, wts):
    n = x1.shape[0]
    tm = MOE_DMA_TILE
    return pl.pallas_call(
        _moe_combine_kernel,
        grid=(n // tm,),
        in_specs=[pl.BlockSpec((2, tm), lambda i: (0, i), memory_space=pltpu.SMEM),
                  pl.BlockSpec((tm, D_MODEL), lambda i: (i, 0)),
                  pl.BlockSpec((tm, LANES), lambda i: (i, 0)),
                  pl.BlockSpec(memory_space=pl.ANY),
                  _const_spec(wts["ln2"].shape)],
        out_specs=pl.BlockSpec((tm, D_MODEL), lambda i: (i, 0)),
        out_shape=jax.ShapeDtypeStruct((n, D_MODEL), F32),
        scratch_shapes=[pltpu.VMEM((2, tm, D_MODEL), F32), pltpu.SemaphoreType.DMA],
        compiler_params=_params("arbitrary"),
        name="moe_combine",
    )(pos, x1, gates, ys, wts["ln2"])


def _moe(x1, gates, wts, glob, l):
    n = x1.shape[0]
    R = MOE_BLOCK
    n_slots = N_EXPERTS_PER_TOKEN * n + N_EXPERTS * R
    pos_f, cnt = _moe_positions(gates)
    pos = pos_f[:, :N_EXPERTS_PER_TOKEN].astype(jnp.int32).T
    blocks = (cnt[0, :N_EXPERTS].astype(jnp.int32) + (R - 1)) // R
    ends = jnp.cumsum(blocks)
    n_used = ends[-1:]
    blk_expert = jnp.sum(jnp.arange(n_slots // R, dtype=jnp.int32)[:, None] >= ends[None, :], axis=1)
    blk_expert = jnp.minimum(blk_expert, N_EXPERTS - 1).astype(jnp.int32)
    xs = _moe_dispatch(x1, pos, ends.astype(jnp.int32), n_slots)
    ys = _moe_experts(xs, blk_expert, n_used, glob, l)
    return _moe_combine(x1, gates, pos, ys, wts)


def _rows8(vectors, width):
    rows = [v.reshape(1, width).astype(F32) for v in vectors]
    rows.append(jnp.zeros((8 - len(rows), width), F32))
    return jnp.concatenate(rows, axis=0)


def _prep_layer(l, p):
    W = RWKV_WIDTH
    w_l = p["w_in"][l]
    rwc = w_l[:, N_ATT_COLS:]
    mu = p["mu_rwkv"][l]
    o_w, o_k, o_v, o_a, o_g = W, W + D_DECAY_LORA, 2 * W + D_DECAY_LORA, 3 * W + D_DECAY_LORA, 3 * W + 2 * D_DECAY_LORA

    def reorder(t):
        parts = [t[..., 0:W], t[..., o_k:o_k + W], t[..., o_v:o_v + W], t[..., o_g:o_g + D_GATE_LORA],
                 t[..., o_w:o_w + D_DECAY_LORA], t[..., o_a:o_a + D_AAA_LORA]]
        return parts

    w_parts = reorder(rwc)
    mu_parts = reorder(mu)
    if l > 0:
        padw = LANES - D_MV_LORA
        w_parts += [p["w_vres_in"][l - 1], jnp.zeros((D_MODEL, padw), F32)]
        mu_parts += [p["mu_vres"][l - 1], jnp.zeros((padw,), F32)]
    wrw = jnp.concatenate(w_parts, axis=1).astype(BF16)
    mu_row = jnp.concatenate(mu_parts).reshape(1, -1)
    zero = jnp.zeros((D_DECAY_LORA, W), F32)
    wda = jnp.concatenate([jnp.concatenate([p["decay_w2"][l], zero], axis=1),
                           jnp.concatenate([zero, p["aaa_a2"][l]], axis=1)], axis=0).astype(BF16)
    vecs = [p["decay_w0"][l], p["aaa_a0"][l], p["k_k"][l], p["k_a"][l], p["r_k"][l].reshape(W)]
    out = {
        "watt": w_l[:, :N_ATT_COLS].astype(BF16),
        "wrw": wrw,
        "mu": mu_row,
        "wda": wda,
        "g2": p["gate_g2"][l].astype(BF16),
        "woa": p["w_o"][l][:ATT_WIDTH].astype(BF16),
        "wor": p["w_o"][l][ATT_WIDTH:].astype(BF16),
        "gn": _rows8([p["gn_g"][l], p["gn_b"][l]], W),
        "ln1": _rows8([p["ln1_g"][l], p["ln1_b"][l]], D_MODEL),
        "ln2": _rows8([p["ln2_g"][l], p["ln2_b"][l]], D_MODEL),
        "sink_rows": jnp.broadcast_to(p["sinks"][l].reshape(N_Q_HEADS, 1), (N_Q_HEADS, LANES)).astype(F32),
    }
    if l > 0:
        vecs.append(p["vres_v0"][l - 1])
        out["v2"] = jnp.concatenate([p["vres_v2"][l - 1], jnp.zeros((LANES - D_MV_LORA, W), F32)], axis=0).astype(BF16)
    out["vecA"] = _rows8(vecs, W)
    hid = jnp.arange(W) // HEAD_DIM
    out["bd"] = (hid[:, None] == hid[None, :]).astype(BF16)
    return out


def _trunk(x3, shift_prev, cache_k, cache_v, wkv_prev, layer_wts, glob):
    decode = cache_k is not None
    n_batch, seq, _ = x3.shape
    x = x3.reshape(n_batch * seq, D_MODEL)
    new_k, new_v, new_wkv, new_shift = [], [], [], []
    v_first = None
    for l in range(DEPTH):
        wts = layer_wts[l]
        new_shift.append(x.reshape(n_batch, seq, D_MODEL)[:, -1])
        q, ka, va, r, lw, k, v, a, b, g, bonus = _inproj(x, shift_prev[l], wts, v_first, seq)
        if l == 0:
            v_first = v
        if decode:
            ck = cache_k[l].reshape(n_batch, WINDOW, KV_WIDTH)
            cv = cache_v[l].reshape(n_batch, WINDOW, KV_WIDTH)
            att = _attn_sample(q, ka, va, ck, cv, wts["sink_rows"], n_batch, seq)
            new_k.append(ka.reshape(n_batch, seq, N_KV_HEADS, HEAD_DIM))
            new_v.append(va.reshape(n_batch, seq, N_KV_HEADS, HEAD_DIM))
            yn, s_out = _wkv_scan(r, lw, k, v, a, b, (wkv_prev[l][None], 0), n_batch, seq, seq, SAMPLE_WKV_GROUP)
        else:
            att = _attn_prompt(q, ka, va, wts["sink_rows"], n_batch, seq)
            last = lambda t: t.reshape(n_batch, seq, KV_WIDTH)[:, -WINDOW:].reshape(n_batch, WINDOW, N_KV_HEADS, HEAD_DIM)
            new_k.append(last(ka))
            new_v.append(last(va))
            yn, s_out = _wkv_scan(r, lw, k, v, a, b, None, n_batch, seq, WKV_CHUNK, 1)
        new_wkv.append(s_out)
        x1, gates = _post(yn, bonus, g, att, x, wts, glob)
        x = _moe(x1, gates, wts, glob, l)
    return (x.reshape(n_batch, seq, D_MODEL), jnp.stack(new_k), jnp.stack(new_v), jnp.stack(new_wkv),
            jnp.stack(new_shift))


def kernel(x_prompt, x_sample, cache_k, cache_v, state_wkv, state_shift, w_in, w_vres_in, mu_rwkv, mu_vres, sinks, decay_w0, decay_w2, aaa_a0, aaa_a2, vres_v0, vres_v2, gate_g2, k_k, k_a, r_k, gn_g, gn_b, w_o, ln1_g, ln1_b, w_router, router_bias, w_gate, w_up, w_down, ln2_g, ln2_b):
    p = dict(w_in=w_in, w_vres_in=w_vres_in, mu_rwkv=mu_rwkv, mu_vres=mu_vres, sinks=sinks,
             decay_w0=decay_w0, decay_w2=decay_w2, aaa_a0=aaa_a0, aaa_a2=aaa_a2,
             vres_v0=vres_v0, vres_v2=vres_v2, gate_g2=gate_g2, k_k=k_k, k_a=k_a, r_k=r_k,
             gn_g=gn_g, gn_b=gn_b, w_o=w_o, ln1_g=ln1_g, ln1_b=ln1_b,
             w_gate=w_gate, w_up=w_up, w_down=w_down, ln2_g=ln2_g, ln2_b=ln2_b)
    layer_wts = [_prep_layer(l, p) for l in range(DEPTH)]
    glob = {
        "wrt": w_router.T.astype(F32),
        "rb": jnp.broadcast_to(router_bias.reshape(N_EXPERTS, 1), (N_EXPERTS, LANES)).astype(F32),
        "wg": w_gate,
        "wu": w_up,
        "wd": w_down,
    }
    b_p = x_prompt.shape[0]
    zero_shift = jnp.zeros((DEPTH, b_p, D_MODEL), x_prompt.dtype)
    y_p, k_p, v_p, wkv_p, shift_p = _trunk(x_prompt, zero_shift, None, None, None, layer_wts, glob)
    y_s, k_s, v_s, wkv_s, shift_s = _trunk(x_sample, state_shift, cache_k, cache_v, state_wkv, layer_wts, glob)
    return (y_p, y_s, k_p, v_p, wkv_p, shift_p, k_s, v_s, wkv_s, shift_s)
```

```python
import functools
import math

import jax
import jax.numpy as jnp
from jax import lax
from jax.experimental import pallas as pl
from jax.experimental.pallas import tpu as pltpu

F32 = jnp.float32
BF16 = jnp.bfloat16

D_MODEL = 1024
DEPTH = 4
HEAD_DIM = 64
ATT_WIDTH = 512
RWKV_WIDTH = 512
N_Q_HEADS = 8
N_KV_HEADS = 2
Q_PER_KV = 4
KV_WIDTH = 128
N_ATT_COLS = ATT_WIDTH + 2 * KV_WIDTH
WINDOW = 128
ATT_SCALE = HEAD_DIM ** -0.5
N_RWKV_HEADS = 8
D_DECAY_LORA = 64
D_AAA_LORA = 64
D_GATE_LORA = 128
D_MV_LORA = 32
DECAY_SCALE = math.exp(-0.5)
GN_EPS = 64e-5
LN_EPS = 1e-5
N_EXPERTS = 16
N_EXPERT_GROUPS = 4
EXPERTS_PER_GROUP = 4
EXPERT_FF = 512
ALPHA = (2 * DEPTH) ** 0.25

LANES = 128
SUBLANES = 8
VMEM_LIMIT_BYTES = 56 * 1024 * 1024
ROW_TILE = 256
MOE_BLOCK = 256
MOE_POS_TILE = 512
MOE_DMA_TILE = 256
N_EXPERTS_PER_TOKEN = 2
DMA_ISSUE_UNROLL = 16
WKV_CHUNK = 128
SAMPLE_ATT_BATCH = 8
SAMPLE_WKV_GROUP = 8


def _dg(a, b, ca, cb):
    return lax.dot_general(a, b, (((ca,), (cb,)), ((), ())), preferred_element_type=F32)


def _bdot(a, b):
    return _dg(a.astype(BF16), b.astype(BF16), 1, 0)


def _split2(x):
    hi = x.astype(BF16)
    lo = (x - hi.astype(F32)).astype(BF16)
    return hi, lo


def _dot3(a, b, ca=1, cb=0):
    ah, al = _split2(a)
    bh, bl = _split2(b)
    return _dg(ah, bh, ca, cb) + (_dg(ah, bl, ca, cb) + _dg(al, bh, ca, cb))


def _dot_exact_lhs(m_bf16, x, parts=3):
    acc = None
    rem = x
    for _ in range(parts):
        p = rem.astype(BF16)
        t = _dg(m_bf16, p, 1, 0)
        acc = t if acc is None else acc + t
        rem = rem - p.astype(F32)
    return acc


def _dot_exact_rhs(x, m_bf16, parts=3):
    acc = None
    rem = x
    for _ in range(parts):
        p = rem.astype(BF16)
        t = _dg(p, m_bf16, 1, 0)
        acc = t if acc is None else acc + t
        rem = rem - p.astype(F32)
    return acc


def _layer_norm(z, g, b):
    mu = jnp.mean(z, axis=-1, keepdims=True)
    var = jnp.mean(jnp.square(z - mu), axis=-1, keepdims=True)
    return (z - mu) * lax.rsqrt(var + LN_EPS) * g + b


def _params(*sem):
    return pltpu.CompilerParams(dimension_semantics=sem, vmem_limit_bytes=VMEM_LIMIT_BYTES)


def _const_spec(shape):
    nd = len(shape)
    return pl.BlockSpec(shape, lambda *_: (0,) * nd)


def _inproj_kernel(*refs, first, seq):
    if first:
        (x_ref, st_ref, watt_ref, wrw_ref, mu_ref, wda_ref, g2_ref, vec_ref, bd_ref,
         q_ref, ka_ref, va_ref, r_ref, lw_ref, k_ref, v_ref, a_ref, b_ref, g_ref, bonus_ref, carry_ref) = refs
    else:
        (x_ref, st_ref, watt_ref, wrw_ref, mu_ref, wda_ref, g2_ref, vec_ref, bd_ref, vfirst_ref, v2_ref,
         q_ref, ka_ref, va_ref, r_ref, lw_ref, k_ref, v_ref, a_ref, b_ref, g_ref, bonus_ref, carry_ref) = refs
    i = pl.program_id(0)
    x = x_ref[...]
    tm = x.shape[0]
    qkv = _dg(x.astype(BF16), watt_ref[...], 1, 0)
    q_ref[...] = qkv[:, :ATT_WIDTH]
    ka_ref[...] = qkv[:, ATT_WIDTH:ATT_WIDTH + KV_WIDTH]
    va_ref[...] = qkv[:, ATT_WIDTH + KV_WIDTH:]

    whole_tiles = seq >= tm
    st = jnp.broadcast_to(st_ref[0], (2 * SUBLANES, D_MODEL)) if whole_tiles else st_ref[...]
    pe = _dg(jnp.concatenate([x, st], axis=0).astype(BF16), wrw_ref[...], 1, 0)
    pc = pe[:tm]
    pst = pe[tm:]
    rowid = lax.broadcasted_iota(jnp.int32, pc.shape, 0)
    pp = pltpu.roll(pc, 1, 0)
    if whole_tiles:
        @pl.when(i == 0)
        def _init():
            carry_ref[...] = jnp.zeros_like(carry_ref)

        prev0 = jnp.where(i % (seq // tm) == 0, pst[0:1], carry_ref[SUBLANES - 1:SUBLANES, :])
        pp = jnp.where(rowid == 0, prev0, pp)
        carry_ref[...] = pc[tm - SUBLANES:]
    else:
        n_st = tm // seq
        er = lax.broadcasted_iota(jnp.int32, (tm, n_st), 0)
        ec = lax.broadcasted_iota(jnp.int32, (tm, n_st), 1)
        expand = (er == ec * seq).astype(BF16)
        pp = jnp.where(rowid % seq == 0, _dot_exact_lhs(expand, pst), pp)
    rw = pc + (pp - pc) * mu_ref[...]
    W = RWKV_WIDTH
    r = rw[:, 0:W]
    k = rw[:, W:2 * W]
    v = rw[:, 2 * W:3 * W]
    gl = rw[:, 3 * W:3 * W + D_GATE_LORA]
    wa = rw[:, 3 * W + D_GATE_LORA:3 * W + 2 * LANES]
    w0 = vec_ref[0:1, :]
    a0 = vec_ref[1:2, :]
    k_k = vec_ref[2:3, :]
    k_a = vec_ref[3:4, :]
    r_k = vec_ref[4:5, :]

    lane = lax.broadcasted_iota(jnp.int32, wa.shape, 1)
    wa_t = jnp.where(lane < D_DECAY_LORA, jnp.tanh(wa), wa)
    da = _bdot(wa_t, wda_ref[...])
    lw = -DECAY_SCALE * jax.nn.sigmoid(w0 + da[:, :W])
    a = jax.nn.sigmoid(a0 + da[:, W:])
    g = _bdot(jax.nn.sigmoid(gl), g2_ref[...])
    if not first:
        mv = rw[:, 3 * W + 2 * LANES:]
        v0 = vec_ref[5:6, :]
        v = v + (vfirst_ref[...] - v) * jax.nn.sigmoid(v0 + _bdot(mv, v2_ref[...]))
    bd = bd_ref[...]
    kk = k * k_k
    ssq = _dot_exact_rhs(kk * kk, bd)
    kk = kk / jnp.maximum(jnp.sqrt(ssq), 1e-12)
    k = k * (1.0 + (a - 1.0) * k_a)
    bonus = _dot_exact_rhs(r * k * r_k, bd) * v

    r_ref[...] = r
    lw_ref[...] = lw
    k_ref[...] = k
    v_ref[...] = v
    a_ref[...] = -kk
    b_ref[...] = kk * a
    g_ref[...] = g
    bonus_ref[...] = bonus


def _inproj(x, shift_state, wts, v_first, seq):
    n = x.shape[0]
    first = v_first is None
    tm = ROW_TILE
    row = lambda w: pl.BlockSpec((tm, w), lambda i: (i, 0))
    if seq >= tm:
        assert seq % tm == 0
        st = shift_state.reshape(-1, 1, D_MODEL)
        st_spec = pl.BlockSpec((1, 1, D_MODEL), lambda i: (i // (seq // tm), 0, 0))
    else:
        assert tm % seq == 0
        st = shift_state
        st_spec = pl.BlockSpec((tm // seq, D_MODEL), lambda i: (i, 0))
    ins = [x, st, wts["watt"], wts["wrw"], wts["mu"], wts["wda"], wts["g2"], wts["vecA"], wts["bd"]]
    in_specs = [row(D_MODEL), st_spec] + [_const_spec(a.shape) for a in ins[2:]]
    if not first:
        ins += [v_first, wts["v2"]]
        in_specs += [row(RWKV_WIDTH), _const_spec(wts["v2"].shape)]
    widths = [ATT_WIDTH, KV_WIDTH, KV_WIDTH] + [RWKV_WIDTH] * 8
    return pl.pallas_call(
        functools.partial(_inproj_kernel, first=first, seq=seq),
        grid=(n // tm,),
        in_specs=in_specs,
        out_specs=[row(w) for w in widths],
        out_shape=[jax.ShapeDtypeStruct((n, w), F32) for w in widths],
        scratch_shapes=[pltpu.VMEM((SUBLANES, wts["wrw"].shape[1]), F32)],
        compiler_params=_params("arbitrary"),
        name="inproj",
    )(*ins)


def _dot3s(a_sp, b_sp, ca=1, cb=0):
    (ah, al), (bh, bl) = a_sp, b_sp
    return _dg(ah, bh, ca, cb) + (_dg(ah, bl, ca, cb) + _dg(al, bh, ca, cb))


def _dot1s(a_sp, b_sp, ca=1, cb=0):
    return _dg(a_sp[0], b_sp[0], ca, cb)


def _wkv_kernel(*refs, group, seq_chunk, has_state):
    if has_state:
        r_ref, lw_ref, k_ref, v_ref, a_ref, b_ref, s0_ref, y_ref, sout_ref, s_scr = refs
    else:
        r_ref, lw_ref, k_ref, v_ref, a_ref, b_ref, y_ref, sout_ref, s_scr = refs
    G, Ls = group, seq_chunk
    L = G * Ls
    H, N = N_RWKV_HEADS, HEAD_DIM
    c_idx = pl.program_id(1)

    @pl.when(c_idx == 0)
    def _init():
        if has_state:
            s_scr[...] = s0_ref[...]
        else:
            s_scr[...] = jnp.zeros_like(s_scr)

    row = lax.broadcasted_iota(jnp.int32, (L, L), 0)
    col = lax.broadcasted_iota(jnp.int32, (L, L), 1)
    same = (row // Ls) == (col // Ls)
    incl = same & (row >= col)
    strict = same & (row > col)
    eye = (row == col).astype(F32)
    row2 = lax.broadcasted_iota(jnp.int32, (2 * L, 2 * L), 0)
    col2 = lax.broadcasted_iota(jnp.int32, (2 * L, 2 * L), 1)
    t_q = row2 % L
    t_k = col2 % L
    mask2 = ((t_q // Ls) == (t_k // Ls)) & ((t_q > t_k) | ((row2 >= L) & (t_q == t_k)))

    lw_all = lw_ref[...]
    cum_all = _dot_exact_lhs(incl.astype(BF16), lw_all)
    if G == 1:
        tot_all = jnp.broadcast_to(cum_all[L - 1:L, :], cum_all.shape)
    else:
        tot_all = _dot_exact_lhs(same.astype(BF16), lw_all)
    e_end = jnp.exp(tot_all - cum_all)
    etot_all = jnp.exp(tot_all)
    a_all = a_ref[...]
    r_all = r_ref[...]
    b_all = b_ref[...]
    k_all = k_ref[...]
    v_all = v_ref[...]
    ar_all = jnp.concatenate([a_all * jnp.exp(cum_all - lw_all), r_all * jnp.exp(cum_all)], axis=0)
    ar_sp = _split2(ar_all)
    if G == 1:
        mid = cum_all[L // 2 - 1:L // 2, :]
        e_neg = jnp.exp(mid - cum_all)
        arc_sp = _split2(jnp.concatenate([a_all * jnp.exp(cum_all - lw_all - mid), r_all * jnp.exp(cum_all - mid)],
                                         axis=0))
    else:
        e_neg = jnp.exp(-cum_all)
        arc_sp = ar_sp
    bk_sp = _split2(jnp.concatenate([b_all * e_neg, k_all * e_neg], axis=0))
    bkh_sp = _split2(jnp.concatenate([b_all * e_end, k_all * e_end], axis=0))
    heads = range(H)
    hsl = lambda t, h: t[:, h * N:(h + 1) * N]
    hsp = lambda sp, h: (hsl(sp[0], h), hsl(sp[1], h))

    m = [jnp.where(mask2, _dg(hsl(arc_sp[0], h), hsl(bk_sp[0], h), 1, 1), 0.0) for h in heads]
    m_sp = [_split2(t) for t in m]
    a_ab = [t[:L, :L] for t in m]
    d = [eye + jnp.where((row // 2) == (col // 2), t, 0.0) for t in a_ab]
    n = 4
    while n <= Ls:
        off = ((row // n) == (col // n)) & ((row // (n // 2)) != (col // (n // 2)))
        d_b = [t.astype(BF16) for t in d]
        dn = [_dg(t, jnp.where(off, s, 0.0).astype(BF16), 1, 0) for t, s in zip(d_b, a_ab)]
        d = [t + _dg(p.astype(BF16), tb, 1, 0) for t, p, tb in zip(d, dn, d_b)]
        n *= 2
    resid = [eye - t + _dot3s((s[0][:L, :L], s[1][:L, :L]), _split2(t)) for t, s in zip(d, m_sp)]
    t_inv = [t + _dg(t.astype(BF16), r.astype(BF16), 1, 0) for t, r in zip(d, resid)]

    s0 = [[s_scr[g, h] for h in heads] for g in range(G)]
    s0_sp = [[_split2(s0[g][h]) for h in heads] for g in range(G)]

    def seq_rows(sp, g, h):
        if G == 1:
            return hsp(sp, h)
        return tuple(jnp.concatenate([hsl(p, h)[g * Ls:(g + 1) * Ls], hsl(p, h)[L + g * Ls:L + (g + 1) * Ls]], axis=0)
                     for p in sp)

    st = [[_dot1s(seq_rows(ar_sp, g, h), s0_sp[g][h], 1, 1) for h in heads] for g in range(G)]
    rhs_st = [jnp.concatenate([st[g][h][:Ls] for g in range(G)], axis=0) for h in heads]
    y_st = [jnp.concatenate([st[g][h][Ls:] for g in range(G)], axis=0) for h in heads]

    v_h = [hsl(v_all, h) for h in heads]
    zeros = jnp.zeros((L, N), F32)
    m_top = [(s[0][:L], s[1][:L]) for s in m_sp]
    m_bot = [(s[0][L:], s[1][L:]) for s in m_sp]
    rhs = [rhs_st[h] + _dot1s(m_top[h], _split2(jnp.concatenate([zeros, v_h[h]], axis=0))) for h in heads]
    u = [_dot1s(_split2(t_inv[h]), _split2(rhs[h])) for h in heads]
    uv_sp = [_split2(jnp.concatenate([u[h], v_h[h]], axis=0)) for h in heads]
    y = [y_st[h] + _dg(m_bot[h][0], uv_sp[h][0], 1, 0) for h in heads]
    for g in range(G):
        for h in heads:
            if G == 1:
                uv_g = uv_sp[h]
            else:
                uv_g = tuple(jnp.concatenate([p[g * Ls:(g + 1) * Ls], p[L + g * Ls:L + (g + 1) * Ls]], axis=0)
                             for p in uv_sp[h])
            upd = _dot1s(uv_g, seq_rows(bkh_sp, g, h), 0, 0)
            s_scr[g, h] = s0[g][h] * hsl(etot_all, h)[g * Ls:g * Ls + 1] + upd
    ys = []
    for h in heads:
        mu = jnp.mean(y[h], axis=-1, keepdims=True)
        var = jnp.mean(jnp.square(y[h] - mu), axis=-1, keepdims=True)
        ys.append((y[h] - mu) * lax.rsqrt(var + GN_EPS))
    y_ref[...] = jnp.concatenate(ys, axis=1)

    @pl.when(c_idx == pl.num_programs(1) - 1)
    def _fin():
        sout_ref[...] = s_scr[...]


def _wkv_scan(r, lw, k, v, a, b, state, n_seq, seq, seq_chunk, group):
    n_chunks = seq // seq_chunk
    assert group == 1 or n_chunks == 1
    has_state = state is not None
    rows = group * seq_chunk
    tok_spec = pl.BlockSpec((rows, RWKV_WIDTH), lambda i, c: (i * n_chunks + c, 0))
    st_spec = pl.BlockSpec((group, N_RWKV_HEADS, HEAD_DIM, HEAD_DIM), lambda i, c: (i, 0, 0, 0))
    in_specs = [tok_spec] * 6
    args = (r, lw, k, v, a, b)
    if has_state:
        states, layer = state
        in_specs.append(pl.BlockSpec((None, group, N_RWKV_HEADS, HEAD_DIM, HEAD_DIM),
                                     lambda i, c: (layer, i, 0, 0, 0)))
        args += (states,)
    return pl.pallas_call(
        functools.partial(_wkv_kernel, group=group, seq_chunk=seq_chunk, has_state=has_state),
        grid=(n_seq // group, n_chunks),
        in_specs=in_specs,
        out_specs=[tok_spec, st_spec],
        out_shape=[jax.ShapeDtypeStruct((n_seq * seq, RWKV_WIDTH), F32),
                   jax.ShapeDtypeStruct((n_seq, N_RWKV_HEADS, HEAD_DIM, HEAD_DIM), F32)],
        scratch_shapes=[pltpu.VMEM((group, N_RWKV_HEADS, HEAD_DIM, HEAD_DIM), F32)],
        compiler_params=_params("arbitrary", "arbitrary"),
        name="wkv_scan",
    )(*args)


def _sink_softmax(s, sink):
    m = sink
    for t in s:
        m = jnp.maximum(m, jnp.max(t, axis=-1, keepdims=True))
    es = [jnp.exp(t - m) for t in s]
    den = jnp.exp(sink - m)
    for e in es:
        den = den + jnp.sum(e, axis=-1, keepdims=True)
    return [e / den for e in es]


def _attn_prompt_kernel(q_ref, kc_ref, kp_ref, vc_ref, vp_ref, sink_ref, o_ref):
    L = WINDOW
    n = pl.program_id(1)
    q = q_ref[...].astype(BF16)
    kw = jnp.concatenate([kp_ref[...], kc_ref[...]], axis=0).astype(BF16)
    vw = jnp.concatenate([vp_ref[...], vc_ref[...]], axis=0).astype(BF16)
    qi = lax.broadcasted_iota(jnp.int32, (L, 2 * L), 0)
    kj = lax.broadcasted_iota(jnp.int32, (L, 2 * L), 1)
    diff = qi + L - kj
    mask = (diff >= 0) & (diff < WINDOW) & ((kj >= L) | (n > 0))
    lane = lax.broadcasted_iota(jnp.int32, vw.shape, 1)
    one = jnp.ones_like(vw)
    vext = [jnp.where(lane < HEAD_DIM, vw, one), jnp.where(lane >= HEAD_DIM, vw, one)]
    heads = range(N_Q_HEADS)
    hsl = lambda t, h: t[:, h * HEAD_DIM:(h + 1) * HEAD_DIM]
    ss = [jnp.where(mask, _dg(hsl(q, h), hsl(kw, h // Q_PER_KV), 1, 1) * ATT_SCALE, -jnp.inf) for h in heads]
    sinks = [sink_ref[h:h + 1, 0:1] for h in heads]
    ms = [jnp.maximum(jnp.max(ss[h], axis=-1, keepdims=True), sinks[h]) for h in heads]
    es = [jnp.exp(ss[h] - ms[h]).astype(BF16) for h in heads]
    oes = [_dg(es[h], vext[h // Q_PER_KV], 1, 0) for h in heads]
    outs = []
    for h in heads:
        hk = h // Q_PER_KV
        rs = oes[h][:, (1 - hk) * HEAD_DIM:(1 - hk) * HEAD_DIM + 1]
        outs.append(hsl(oes[h], hk) / (rs + jnp.exp(sinks[h] - ms[h])))
    o_ref[...] = jnp.concatenate(outs, axis=1)


def _attn_prompt(q, k, v, sink_rows, n_batch, seq):
    nb = seq // WINDOW
    cur = lambda w: pl.BlockSpec((WINDOW, w), lambda b, n: (b * nb + n, 0))
    prev = lambda w: pl.BlockSpec((WINDOW, w), lambda b, n: (b * nb + jnp.maximum(n - 1, 0), 0))
    return pl.pallas_call(
        _attn_prompt_kernel,
        grid=(n_batch, nb),
        in_specs=[cur(ATT_WIDTH), cur(KV_WIDTH), prev(KV_WIDTH), cur(KV_WIDTH), prev(KV_WIDTH),
                  _const_spec(sink_rows.shape)],
        out_specs=cur(ATT_WIDTH),
        out_shape=jax.ShapeDtypeStruct((n_batch * seq, ATT_WIDTH), F32),
        compiler_params=_params("arbitrary", "arbitrary"),
        name="attn_prompt",
    )(q, k, k, v, v, sink_rows)


def _attn_sample_kernel(q_ref, kn_ref, vn_ref, ck_ref, cv_ref, sink_ref, o_ref, *, seq):
    S = seq
    W = WINDOW
    B = SAMPLE_ATT_BATCH
    R = Q_PER_KV * S
    tq1 = lax.broadcasted_iota(jnp.int32, (R, W), 0) % S
    kj1 = lax.broadcasted_iota(jnp.int32, (R, W), 1)
    mask_cache = kj1 > tq1
    tq2 = lax.broadcasted_iota(jnp.int32, (R, S), 0) % S
    kj2 = lax.broadcasted_iota(jnp.int32, (R, S), 1)
    mask_new = kj2 <= tq2
    kvs = range(N_KV_HEADS)
    hsl = lambda t, h: t[:, h * HEAD_DIM:(h + 1) * HEAD_DIM]
    group = lambda hk: range(hk * Q_PER_KV, (hk + 1) * Q_PER_KV)
    seq_rows = lambda t, b: t[b * S:(b + 1) * S]
    q = q_ref[...]
    kn = kn_ref[...]
    vn = vn_ref[...]
    qh = [hsl(q, h) for h in range(N_Q_HEADS)]
    knh = [hsl(kn, hk) for hk in kvs]
    vnh = [hsl(vn, hk) for hk in kvs]
    sinks = [jnp.concatenate([jnp.broadcast_to(sink_ref[h:h + 1, 0:1], (S, 1)) for h in group(hk)], axis=0)
             for hk in kvs]
    items = [(b, hk) for b in range(B) for hk in kvs]
    qs = [jnp.concatenate([seq_rows(qh[h], b) for h in group(hk)], axis=0).astype(BF16) for b, hk in items]
    ck = [ck_ref[b].astype(BF16) for b in range(B)]
    cv = [cv_ref[b].astype(BF16) for b in range(B)]
    s1 = [jnp.where(mask_cache, _dg(qs[i], hsl(ck[b], hk), 1, 1) * ATT_SCALE, -jnp.inf)
          for i, (b, hk) in enumerate(items)]
    s2 = [jnp.where(mask_new, _dg(qs[i], seq_rows(knh[hk], b).astype(BF16), 1, 1) * ATT_SCALE, -jnp.inf)
          for i, (b, hk) in enumerate(items)]
    ps = [_sink_softmax([s1[i], s2[i]], sinks[hk]) for i, (b, hk) in enumerate(items)]
    outs = [_dg(ps[i][0].astype(BF16), hsl(cv[b], hk), 1, 0)
            + _dg(ps[i][1].astype(BF16), seq_rows(vnh[hk], b).astype(BF16), 1, 0)
            for i, (b, hk) in enumerate(items)]
    rows = [jnp.concatenate([outs[b * N_KV_HEADS + hk][g * S:(g + 1) * S] for hk in kvs for g in range(Q_PER_KV)],
                            axis=1) for b in range(B)]
    o_ref[...] = jnp.concatenate(rows, axis=0)


def _attn_sample(q, k, v, cache_k, cache_v, sink_rows, n_batch, seq):
    bb = SAMPLE_ATT_BATCH
    tok = lambda w: pl.BlockSpec((bb * seq, w), lambda i: (i, 0))
    cache = pl.BlockSpec((bb, WINDOW, KV_WIDTH), lambda i: (i, 0, 0))
    return pl.pallas_call(
        functools.partial(_attn_sample_kernel, seq=seq),
        grid=(n_batch // bb,),
        in_specs=[tok(ATT_WIDTH), tok(KV_WIDTH), tok(KV_WIDTH), cache, cache, _const_spec(sink_rows.shape)],
        out_specs=tok(ATT_WIDTH),
        out_shape=jax.ShapeDtypeStruct((n_batch * seq, ATT_WIDTH), F32),
        compiler_params=_params("arbitrary"),
        name="attn_sample",
    )(q, k, v, cache_k, cache_v, sink_rows)


def _second_max4(a, b, c, d):
    return jnp.maximum(jnp.maximum(jnp.minimum(a, b), jnp.minimum(c, d)),
                       jnp.minimum(jnp.maximum(a, b), jnp.maximum(c, d)))


def _route(logits_t, bias_col):
    G, E = N_EXPERT_GROUPS, EXPERTS_PER_GROUP
    m = jnp.max(logits_t, axis=0, keepdims=True)
    ex = jnp.exp(logits_t - m)
    probs = ex / jnp.sum(ex, axis=0, keepdims=True)
    sel = probs + bias_col
    p = [probs[e:e + 1, :] for e in range(N_EXPERTS)]
    s = [sel[e:e + 1, :] for e in range(N_EXPERTS)]
    gs = []
    for g in range(G):
        a, b, c, d = s[E * g:E * g + E]
        top1 = jnp.maximum(jnp.maximum(a, b), jnp.maximum(c, d))
        gs.append(top1 + _second_max4(a, b, c, d))
    best = jnp.zeros_like(gs[0], dtype=jnp.int32)
    best_s = gs[0]
    for g in range(1, G):
        upd = gs[g] > best_s
        best = jnp.where(upd, g, best)
        best_s = jnp.where(upd, gs[g], best_s)

    def pick(vals, j):
        out = vals[j]
        for g in range(1, G):
            out = jnp.where(best == g, vals[E * g + j], out)
        return out

    ig = [pick(s, j) for j in range(E)]
    pg = [pick(p, j) for j in range(E)]
    l1 = jnp.zeros_like(best)
    v1 = ig[0]
    for j in range(1, E):
        upd = ig[j] > v1
        l1 = jnp.where(upd, j, l1)
        v1 = jnp.where(upd, ig[j], v1)
    l2 = jnp.full_like(best, -1)
    v2 = jnp.full_like(v1, -jnp.inf)
    for j in range(E):
        upd = (l1 != j) & (ig[j] > v2)
        l2 = jnp.where(upd, j, l2)
        v2 = jnp.where(upd, ig[j], v2)
    zero = jnp.zeros_like(v1)
    w1 = zero
    w2 = zero
    for j in range(E):
        w1 = jnp.where(l1 == j, pg[j], w1)
        w2 = jnp.where(l2 == j, pg[j], w2)
    wsum = w1 + w2
    w1 = w1 / wsum
    w2 = w2 / wsum
    e1 = (best * E + l1).astype(F32)
    e2 = (best * E + l2).astype(F32)
    return jnp.concatenate([e1, e2, w1, w2, zero, zero, zero, zero], axis=0)


def _post_kernel(yn_ref, bonus_ref, g_ref, att_ref, x_ref, woa_ref, wor_ref, gn_ref, ln_ref,
                 wrt_ref, rb_ref, x1_ref, gates_ref):
    rw_out = (yn_ref[...] * gn_ref[0:1, :] + gn_ref[1:2, :] + bonus_ref[...]) * g_ref[...]
    mixed = _bdot(att_ref[...], woa_ref[...]) + _bdot(rw_out, wor_ref[...])
    x1 = _layer_norm(ALPHA * x_ref[...] + mixed, ln_ref[0:1, :], ln_ref[1:2, :])
    x1_ref[...] = x1
    logits_t = _dot3(wrt_ref[...], x1, 1, 1)
    route_t = _route(logits_t, rb_ref[:, 0:1])
    pad = jnp.zeros((LANES - route_t.shape[0], route_t.shape[1]), F32)
    gates_ref[...] = jnp.concatenate([route_t, pad], axis=0).T


def _post(yn, bonus, g, att, x, wts, glob):
    n = x.shape[0]
    tm = ROW_TILE
    row = lambda w: pl.BlockSpec((tm, w), lambda i: (i, 0))
    consts = [wts["woa"], wts["wor"], wts["gn"], wts["ln1"], glob["wrt"], glob["rb"]]
    return pl.pallas_call(
        _post_kernel,
        grid=(n // tm,),
        in_specs=[row(RWKV_WIDTH)] * 3 + [row(ATT_WIDTH), row(D_MODEL)] + [_const_spec(a.shape) for a in consts],
        out_specs=[row(D_MODEL), row(LANES)],
        out_shape=[jax.ShapeDtypeStruct((n, D_MODEL), F32), jax.ShapeDtypeStruct((n, LANES), F32)],
        compiler_params=_params("arbitrary"),
        name="post_mix",
    )(yn, bonus, g, att, x, *consts)


def _moe_positions_kernel(gates_ref, pos_ref, cnt_ref, cnt_scr, offs_scr, carry_scr):
    ph = pl.program_id(0)
    i = pl.program_id(1)
    g = gates_ref[...]
    tm = g.shape[0]
    lane = lax.broadcasted_iota(jnp.int32, g.shape, 1).astype(F32)
    oh1 = (lane == g[:, 0:1]).astype(F32)
    oh2 = (lane == g[:, 1:2]).astype(F32)
    oh = oh1 + oh2

    @pl.when((ph == 0) & (i == 0))
    def _zero():
        cnt_scr[...] = jnp.zeros_like(cnt_scr)

    @pl.when(ph == 0)
    def _count():
        cnt_scr[...] += jnp.sum(oh, axis=0, keepdims=True)
        pos_ref[...] = jnp.zeros_like(pos_ref)

    @pl.when((ph == 1) & (i == 0))
    def _offsets():
        cnt = cnt_scr[...]
        padded = jnp.floor((cnt + (MOE_BLOCK - 1)) * (1.0 / MOE_BLOCK)) * MOE_BLOCK
        r = lax.broadcasted_iota(jnp.int32, (LANES, LANES), 0)
        c = lax.broadcasted_iota(jnp.int32, (LANES, LANES), 1)
        offs_scr[...] = _dot_exact_rhs(padded, (r < c).astype(BF16))
        carry_scr[...] = jnp.zeros_like(carry_scr)
        cnt_ref[...] = cnt

    @pl.when(ph == 1)
    def _rank():
        r = lax.broadcasted_iota(jnp.int32, (tm, tm), 0)
        c = lax.broadcasted_iota(jnp.int32, (tm, tm), 1)
        before = _dg((r > c).astype(BF16), oh.astype(BF16), 1, 0)
        base = before + carry_scr[0:1, :] + offs_scr[0:1, :]
        p1 = jnp.sum(oh1 * base, axis=1, keepdims=True)
        p2 = jnp.sum(oh2 * base, axis=1, keepdims=True)
        pos_ref[...] = jnp.where(lane == 0.0, p1, jnp.where(lane == 1.0, p2, 0.0))
        carry_scr[...] += jnp.sum(oh, axis=0, keepdims=True)


def _moe_positions(gates):
    n = gates.shape[0]
    tm = min(MOE_POS_TILE, n)
    stat = pltpu.VMEM((SUBLANES, LANES), F32)
    return pl.pallas_call(
        _moe_positions_kernel,
        grid=(2, n // tm),
        in_specs=[pl.BlockSpec((tm, LANES), lambda ph, i: (i, 0))],
        out_specs=[pl.BlockSpec((tm, LANES), lambda ph, i: (i * ph, 0)), _const_spec((SUBLANES, LANES))],
        out_shape=[jax.ShapeDtypeStruct((n, LANES), F32), jax.ShapeDtypeStruct((SUBLANES, LANES), F32)],
        scratch_shapes=[stat, stat, stat],
        compiler_params=_params("arbitrary", "arbitrary"),
        name="moe_positions",
    )(gates)


def _moe_dispatch_kernel(ends_ref, pos_ref, x_ref, xs_ref, zero_scr, sem):
    tm = x_ref.shape[0]
    R = zero_scr.shape[0]

    @pl.when(pl.program_id(0) == 0)
    def _zero_tail_blocks():
        zero_scr[...] = jnp.zeros_like(zero_scr)

        def tail_copy(e):
            return pltpu.make_async_copy(zero_scr, xs_ref.at[pl.ds((ends_ref[e] - 1) * R, R), :], sem)

        def non_empty(e):
            return ends_ref[e] > (ends_ref[e - 1] if e else 0)

        n_blocks = xs_ref.shape[0] // R
        n_used = ends_ref[N_EXPERTS - 1]

        def spare_copy(j):
            return pltpu.make_async_copy(zero_scr, xs_ref.at[pl.ds((n_used + j) * R, R), :], sem)

        for e in range(N_EXPERTS):
            pl.when(non_empty(e))(lambda e=e: tail_copy(e).start())
            pl.when(n_used + e < n_blocks)(lambda e=e: spare_copy(e).start())
        for e in range(N_EXPERTS):
            pl.when(non_empty(e))(lambda e=e: tail_copy(e).wait())
            pl.when(n_used + e < n_blocks)(lambda e=e: spare_copy(e).wait())

    def row_copy(r, k):
        return pltpu.make_async_copy(x_ref.at[pl.ds(r, 1), :], xs_ref.at[pl.ds(pos_ref[k, r], 1), :], sem)

    def issue(r, carry):
        row_copy(r, 0).start(priority=0)
        row_copy(r, 1).start(priority=1)
        return carry

    lax.fori_loop(0, tm, issue, 0, unroll=DMA_ISSUE_UNROLL)
    for _ in range(N_EXPERTS_PER_TOKEN):
        pltpu.make_async_copy(x_ref, xs_ref.at[pl.ds(0, tm), :], sem).wait()


def _moe_dispatch(x1, pos, ends, n_slots):
    n = x1.shape[0]
    tm = MOE_DMA_TILE
    return pl.pallas_call(
        _moe_dispatch_kernel,
        grid=(n // tm,),
        in_specs=[pl.BlockSpec(memory_space=pltpu.SMEM),
                  pl.BlockSpec((2, tm), lambda i: (0, i), memory_space=pltpu.SMEM),
                  pl.BlockSpec((tm, D_MODEL), lambda i: (i, 0))],
        out_specs=pl.BlockSpec(memory_space=pl.ANY),
        out_shape=jax.ShapeDtypeStruct((n_slots, D_MODEL), F32),
        scratch_shapes=[pltpu.VMEM((MOE_BLOCK, D_MODEL), F32), pltpu.SemaphoreType.DMA],
        compiler_params=_params("arbitrary"),
        name="moe_dispatch",
    )(ends, pos, x1)


def _moe_experts_kernel(blk_expert_ref, n_used_ref, xs_ref, wg_ref, wu_ref, wd_ref, ys_ref, wg_b, wu_b, wd_b):
    b = pl.program_id(0)
    used = b < n_used_ref[0]
    new_expert = (b == 0) | (blk_expert_ref[b] != blk_expert_ref[jnp.maximum(b - 1, 0)])

    @pl.when(used & new_expert)
    def _cast_weights():
        wg_b[...] = wg_ref[0, 0].astype(BF16)
        wu_b[...] = wu_ref[0, 0].astype(BF16)
        wd_b[...] = wd_ref[0, 0].astype(BF16)

    @pl.when(used)
    def _compute():
        xb = xs_ref[...].astype(BF16)
        h = jax.nn.silu(_dg(xb, wg_b[...], 1, 0)) * _dg(xb, wu_b[...], 1, 0)
        ys_ref[...] = _dg(h.astype(BF16), wd_b[...], 1, 0)

    @pl.when(b >= n_used_ref[0])
    def _skip():
        ys_ref[...] = jnp.zeros_like(ys_ref)


def _moe_experts(xs, blk_expert, n_used, glob, l):
    n_slots = xs.shape[0]
    R = MOE_BLOCK
    last = lambda b, nu: jnp.minimum(b, nu[0] - 1)
    wspec = lambda s: pl.BlockSpec((1, 1) + s, lambda b, be, nu: (l, be[last(b, nu)], 0, 0))
    return pl.pallas_call(
        _moe_experts_kernel,
        grid_spec=pltpu.PrefetchScalarGridSpec(
            num_scalar_prefetch=2,
            grid=(n_slots // R,),
            in_specs=[pl.BlockSpec((R, D_MODEL), lambda b, be, nu: (last(b, nu), 0)),
                      wspec((D_MODEL, EXPERT_FF)), wspec((D_MODEL, EXPERT_FF)), wspec((EXPERT_FF, D_MODEL))],
            out_specs=pl.BlockSpec((R, D_MODEL), lambda b, be, nu: (b, 0)),
            scratch_shapes=[pltpu.VMEM((D_MODEL, EXPERT_FF), BF16), pltpu.VMEM((D_MODEL, EXPERT_FF), BF16),
                            pltpu.VMEM((EXPERT_FF, D_MODEL), BF16)],
        ),
        out_shape=jax.ShapeDtypeStruct((n_slots, D_MODEL), F32),
        compiler_params=_params("arbitrary"),
        name="moe_experts",
    )(blk_expert, n_used, xs, glob["wg"], glob["wu"], glob["wd"])


def _moe_combine_kernel(pos_ref, x_ref, gates_ref, ys_ref, ln_ref, o_ref, buf, sem):
    tm = x_ref.shape[0]

    def row_copy(r, k):
        return pltpu.make_async_copy(ys_ref.at[pl.ds(pos_ref[k, r], 1), :], buf.at[k, pl.ds(r, 1), :], sem)

    def issue(r, carry):
        row_copy(r, 0).start(priority=0)
        row_copy(r, 1).start(priority=1)
        return carry

    lax.fori_loop(0, tm, issue, 0, unroll=DMA_ISSUE_UNROLL)
    for k in range(N_EXPERTS_PER_TOKEN):
        pltpu.make_async_copy(ys_ref.at[pl.ds(0, tm), :], buf.at[k], sem).wait()
    g = gates_ref[...]
    ffn = g[:, 2:3] * buf[0] + g[:, 3:4] * buf[1]
    o_ref[...] = _layer_norm(ALPHA * x_ref[...] + ffn, ln_ref[0:1, :], ln_ref[1:2, :])


def _moe_combine(x1, gates, pos, ys, wts):
    n = x1.shape[0]
    tm = MOE_DMA_TILE
    return pl.pallas_call(
        _moe_combine_kernel,
        grid=(n // tm,),
        in_specs=[pl.BlockSpec((2, tm), lambda i: (0, i), memory_space=pltpu.SMEM),
                  pl.BlockSpec((tm, D_MODEL), lambda i: (i, 0)),
                  pl.BlockSpec((tm, LANES), lambda i: (i, 0)),
                  pl.BlockSpec(memory_space=pl.ANY),
                  _const_spec(wts["ln2"].shape)],
        out_specs=pl.BlockSpec((tm, D_MODEL), lambda i: (i, 0)),
        out_shape=jax.ShapeDtypeStruct((n, D_MODEL), F32),
        scratch_shapes=[pltpu.VMEM((2, tm, D_MODEL), F32), pltpu.SemaphoreType.DMA],
        compiler_params=_params("arbitrary"),
        name="moe_combine",
    )(pos, x1, gates, ys, wts["ln2"])


def _moe(x1, gates, wts, glob, l):
    n = x1.shape[0]
    R = MOE_BLOCK
    n_slots = N_EXPERTS_PER_TOKEN * n + N_EXPERTS * R
    pos_f, cnt = _moe_positions(gates)
    pos = pos_f[:, :N_EXPERTS_PER_TOKEN].astype(jnp.int32).T
    blocks = (cnt[0, :N_EXPERTS].astype(jnp.int32) + (R - 1)) // R
    ends = jnp.cumsum(blocks)
    n_used = ends[-1:]
    blk_expert = jnp.sum(jnp.arange(n_slots // R, dtype=jnp.int32)[:, None] >= ends[None, :], axis=1)
    blk_expert = jnp.minimum(blk_expert, N_EXPERTS - 1).astype(jnp.int32)
    xs = _moe_dispatch(x1, pos, ends.astype(jnp.int32), n_slots)
    ys = _moe_experts(xs, blk_expert, n_used, glob, l)
    return _moe_combine(x1, gates, pos, ys, wts)


def _rows8(vectors, width):
    rows = [v.reshape(1, width).astype(F32) for v in vectors]
    rows.append(jnp.zeros((8 - len(rows), width), F32))
    return jnp.concatenate(rows, axis=0)


def _prep_layer(l, p):
    W = RWKV_WIDTH
    w_l = p["w_in"][l]
    rwc = w_l[:, N_ATT_COLS:]
    mu = p["mu_rwkv"][l]
    o_w, o_k, o_v, o_a, o_g = W, W + D_DECAY_LORA, 2 * W + D_DECAY_LORA, 3 * W + D_DECAY_LORA, 3 * W + 2 * D_DECAY_LORA

    def reorder(t):
        parts = [t[..., 0:W], t[..., o_k:o_k + W], t[..., o_v:o_v + W], t[..., o_g:o_g + D_GATE_LORA],
                 t[..., o_w:o_w + D_DECAY_LORA], t[..., o_a:o_a + D_AAA_LORA]]
        return parts

    w_parts = reorder(rwc)
    mu_parts = reorder(mu)
    if l > 0:
        padw = LANES - D_MV_LORA
        w_parts += [p["w_vres_in"][l - 1], jnp.zeros((D_MODEL, padw), F32)]
        mu_parts += [p["mu_vres"][l - 1], jnp.zeros((padw,), F32)]
    wrw = jnp.concatenate(w_parts, axis=1).astype(BF16)
    mu_row = jnp.concatenate(mu_parts).reshape(1, -1)
    zero = jnp.zeros((D_DECAY_LORA, W), F32)
    wda = jnp.concatenate([jnp.concatenate([p["decay_w2"][l], zero], axis=1),
                           jnp.concatenate([zero, p["aaa_a2"][l]], axis=1)], axis=0).astype(BF16)
    vecs = [p["decay_w0"][l], p["aaa_a0"][l], p["k_k"][l], p["k_a"][l], p["r_k"][l].reshape(W)]
    out = {
        "watt": w_l[:, :N_ATT_COLS].astype(BF16),
        "wrw": wrw,
        "mu": mu_row,
        "wda": wda,
        "g2": p["gate_g2"][l].astype(BF16),
        "woa": p["w_o"][l][:ATT_WIDTH].astype(BF16),
        "wor": p["w_o"][l][ATT_WIDTH:].astype(BF16),
        "gn": _rows8([p["gn_g"][l], p["gn_b"][l]], W),
        "ln1": _rows8([p["ln1_g"][l], p["ln1_b"][l]], D_MODEL),
        "ln2": _rows8([p["ln2_g"][l], p["ln2_b"][l]], D_MODEL),
        "sink_rows": jnp.broadcast_to(p["sinks"][l].reshape(N_Q_HEADS, 1), (N_Q_HEADS, LANES)).astype(F32),
    }
    if l > 0:
        vecs.append(p["vres_v0"][l - 1])
        out["v2"] = jnp.concatenate([p["vres_v2"][l - 1], jnp.zeros((LANES - D_MV_LORA, W), F32)], axis=0).astype(BF16)
    out["vecA"] = _rows8(vecs, W)
    hid = jnp.arange(W) // HEAD_DIM
    out["bd"] = (hid[:, None] == hid[None, :]).astype(BF16)
    return out


def _trunk(x3, shift_prev, cache_k, cache_v, wkv_prev, layer_wts, glob):
    decode = cache_k is not None
    n_batch, seq, _ = x3.shape
    x = x3.reshape(n_batch * seq, D_MODEL)
    new_k, new_v, new_wkv, new_shift = [], [], [], []
    v_first = None
    for l in range(DEPTH):
        wts = layer_wts[l]
        new_shift.append(x.reshape(n_batch, seq, D_MODEL)[:, -1])
        q, ka, va, r, lw, k, v, a, b, g, bonus = _inproj(x, shift_prev[l], wts, v_first, seq)
        if l == 0:
            v_first = v
        if decode:
            ck = cache_k[l].reshape(n_batch, WINDOW, KV_WIDTH)
            cv = cache_v[l].reshape(n_batch, WINDOW, KV_WIDTH)
            att = _attn_sample(q, ka, va, ck, cv, wts["sink_rows"], n_batch, seq)
            new_k.append(ka.reshape(n_batch, seq, N_KV_HEADS, HEAD_DIM))
            new_v.append(va.reshape(n_batch, seq, N_KV_HEADS, HEAD_DIM))
            yn, s_out = _wkv_scan(r, lw, k, v, a, b, (wkv_prev[l][None], 0), n_batch, seq, seq, SAMPLE_WKV_GROUP)
        else:
            att = _attn_prompt(q, ka, va, wts["sink_rows"], n_batch, seq)
            last = lambda t: t.reshape(n_batch, seq, KV_WIDTH)[:, -WINDOW:].reshape(n_batch, WINDOW, N_KV_HEADS, HEAD_DIM)
            new_k.append(last(ka))
            new_v.append(last(va))
            yn, s_out = _wkv_scan(r, lw, k, v, a, b, None, n_batch, seq, WKV_CHUNK, 1)
        new_wkv.append(s_out)
        x1, gates = _post(yn, bonus, g, att, x, wts, glob)
        x = _moe(x1, gates, wts, glob, l)
    return (x.reshape(n_batch, seq, D_MODEL), jnp.stack(new_k), jnp.stack(new_v), jnp.stack(new_wkv),
            jnp.stack(new_shift))


def kernel(x_prompt, x_sample, cache_k, cache_v, state_wkv, state_shift, w_in, w_vres_in, mu_rwkv, mu_vres, sinks, decay_w0, decay_w2, aaa_a0, aaa_a2, vres_v0, vres_v2, gate_g2, k_k, k_a, r_k, gn_g, gn_b, w_o, ln1_g, ln1_b, w_router, router_bias, w_gate, w_up, w_down, ln2_g, ln2_b):
    p = dict(w_in=w_in, w_vres_in=w_vres_in, mu_rwkv=mu_rwkv, mu_vres=mu_vres, sinks=sinks,
             decay_w0=decay_w0, decay_w2=decay_w2, aaa_a0=aaa_a0, aaa_a2=aaa_a2,
             vres_v0=vres_v0, vres_v2=vres_v2, gate_g2=gate_g2, k_k=k_k, k_a=k_a, r_k=r_k,
             gn_g=gn_g, gn_b=gn_b, w_o=w_o, ln1_g=ln1_g, ln1_b=ln1_b,
             w_gate=w_gate, w_up=w_up, w_down=w_down, ln2_g=ln2_g, ln2_b=ln2_b)
    layer_wts = [_prep_layer(l, p) for l in range(DEPTH)]
    glob = {
        "wrt": w_router.T.astype(F32),
        "rb": jnp.broadcast_to(router_bias.reshape(N_EXPERTS, 1), (N_EXPERTS, LANES)).astype(F32),
        "wg": w_gate,
        "wu": w_up,
        "wd": w_down,
    }
    b_p = x_prompt.shape[0]
    zero_shift = jnp.zeros((DEPTH, b_p, D_MODEL), x_prompt.dtype)
    y_p, k_p, v_p, wkv_p, shift_p = _trunk(x_prompt, zero_shift, None, None, None, layer_wts, glob)
    y_s, k_s, v_s, wkv_s, shift_s = _trunk(x_sample, state_shift, cache_k, cache_v, state_wkv, layer_wts, glob)
    return (y_p, y_s, k_p, v_p, wkv_p, shift_p, k_s, v_s, wkv_s, shift_s)
```

```python
import functools
import math

import jax
import jax.numpy as jnp
from jax import lax
from jax.experimental import pallas as pl
from jax.experimental.pallas import tpu as pltpu

F32 = jnp.float32
BF16 = jnp.bfloat16

D_MODEL = 1024
DEPTH = 4
HEAD_DIM = 64
ATT_WIDTH = 512
RWKV_WIDTH = 512
N_Q_HEADS = 8
N_KV_HEADS = 2
Q_PER_KV = 4
KV_WIDTH = 128
N_ATT_COLS = ATT_WIDTH + 2 * KV_WIDTH
WINDOW = 128
ATT_SCALE = HEAD_DIM ** -0.5
N_RWKV_HEADS = 8
D_DECAY_LORA = 64
D_AAA_LORA = 64
D_GATE_LORA = 128
D_MV_LORA = 32
DECAY_SCALE = math.exp(-0.5)
GN_EPS = 64e-5
LN_EPS = 1e-5
N_EXPERTS = 16
N_EXPERT_GROUPS = 4
EXPERTS_PER_GROUP = 4
EXPERT_FF = 512
ALPHA = (2 * DEPTH) ** 0.25

LANES = 128
SUBLANES = 8
VMEM_LIMIT_BYTES = 56 * 1024 * 1024
ROW_TILE = 256
MOE_BLOCK = 512
MOE_POS_TILE = 512
MOE_DMA_TILE = 1024
N_EXPERTS_PER_TOKEN = 2
DMA_ISSUE_UNROLL = 16
WKV_CHUNK = 128
SAMPLE_ATT_BATCH = 8
SAMPLE_WKV_GROUP = 8


def _dg(a, b, ca, cb):
    return lax.dot_general(a, b, (((ca,), (cb,)), ((), ())), preferred_element_type=F32)


def _bdot(a, b):
    return _dg(a.astype(BF16), b.astype(BF16), 1, 0)


def _split2(x):
    hi = x.astype(BF16)
    lo = (x - hi.astype(F32)).astype(BF16)
    return hi, lo


def _dot3(a, b, ca=1, cb=0):
    ah, al = _split2(a)
    bh, bl = _split2(b)
    return _dg(ah, bh, ca, cb) + (_dg(ah, bl, ca, cb) + _dg(al, bh, ca, cb))


def _dot_exact_lhs(m_bf16, x, parts=3):
    acc = None
    rem = x
    for _ in range(parts):
        p = rem.astype(BF16)
        t = _dg(m_bf16, p, 1, 0)
        acc = t if acc is None else acc + t
        rem = rem - p.astype(F32)
    return acc


def _dot_exact_rhs(x, m_bf16, parts=3):
    acc = None
    rem = x
    for _ in range(parts):
        p = rem.astype(BF16)
        t = _dg(p, m_bf16, 1, 0)
        acc = t if acc is None else acc + t
        rem = rem - p.astype(F32)
    return acc


def _layer_norm(z, g, b):
    mu = jnp.mean(z, axis=-1, keepdims=True)
    var = jnp.mean(jnp.square(z - mu), axis=-1, keepdims=True)
    return (z - mu) * lax.rsqrt(var + LN_EPS) * g + b


def _params(*sem):
    return pltpu.CompilerParams(dimension_semantics=sem, vmem_limit_bytes=VMEM_LIMIT_BYTES)


def _const_spec(shape):
    nd = len(shape)
    return pl.BlockSpec(shape, lambda *_: (0,) * nd)


def _inproj_kernel(*refs, first, seq):
    if first:
        (x_ref, st_ref, watt_ref, wrw_ref, mu_ref, wda_ref, g2_ref, vec_ref, bd_ref,
         q_ref, ka_ref, va_ref, r_ref, lw_ref, k_ref, v_ref, a_ref, b_ref, g_ref, bonus_ref, carry_ref) = refs
    else:
        (x_ref, st_ref, watt_ref, wrw_ref, mu_ref, wda_ref, g2_ref, vec_ref, bd_ref, vfirst_ref, v2_ref,
         q_ref, ka_ref, va_ref, r_ref, lw_ref, k_ref, v_ref, a_ref, b_ref, g_ref, bonus_ref, carry_ref) = refs
    i = pl.program_id(0)
    x = x_ref[...]
    tm = x.shape[0]
    qkv = _dg(x.astype(BF16), watt_ref[...], 1, 0)
    q_ref[...] = qkv[:, :ATT_WIDTH]
    ka_ref[...] = qkv[:, ATT_WIDTH:ATT_WIDTH + KV_WIDTH]
    va_ref[...] = qkv[:, ATT_WIDTH + KV_WIDTH:]

    whole_tiles = seq >= tm
    st = jnp.broadcast_to(st_ref[0], (2 * SUBLANES, D_MODEL)) if whole_tiles else st_ref[...]
    pe = _dg(jnp.concatenate([x, st], axis=0).astype(BF16), wrw_ref[...], 1, 0)
    pc = pe[:tm]
    pst = pe[tm:]
    rowid = lax.broadcasted_iota(jnp.int32, pc.shape, 0)
    pp = pltpu.roll(pc, 1, 0)
    if whole_tiles:
        @pl.when(i == 0)
        def _init():
            carry_ref[...] = jnp.zeros_like(carry_ref)

        prev0 = jnp.where(i % (seq // tm) == 0, pst[0:1], carry_ref[SUBLANES - 1:SUBLANES, :])
        pp = jnp.where(rowid == 0, prev0, pp)
        carry_ref[...] = pc[tm - SUBLANES:]
    else:
        n_st = tm // seq
        er = lax.broadcasted_iota(jnp.int32, (tm, n_st), 0)
        ec = lax.broadcasted_iota(jnp.int32, (tm, n_st), 1)
        expand = (er == ec * seq).astype(BF16)
        pp = jnp.where(rowid % seq == 0, _dot_exact_lhs(expand, pst), pp)
    rw = pc + (pp - pc) * mu_ref[...]
    W = RWKV_WIDTH
    r = rw[:, 0:W]
    k = rw[:, W:2 * W]
    v = rw[:, 2 * W:3 * W]
    gl = rw[:, 3 * W:3 * W + D_GATE_LORA]
    wa = rw[:, 3 * W + D_GATE_LORA:3 * W + 2 * LANES]
    w0 = vec_ref[0:1, :]
    a0 = vec_ref[1:2, :]
    k_k = vec_ref[2:3, :]
    k_a = vec_ref[3:4, :]
    r_k = vec_ref[4:5, :]

    lane = lax.broadcasted_iota(jnp.int32, wa.shape, 1)
    wa_t = jnp.where(lane < D_DECAY_LORA, jnp.tanh(wa), wa)
    da = _bdot(wa_t, wda_ref[...])
    lw = -DECAY_SCALE * jax.nn.sigmoid(w0 + da[:, :W])
    a = jax.nn.sigmoid(a0 + da[:, W:])
    g = _bdot(jax.nn.sigmoid(gl), g2_ref[...])
    if not first:
        mv = rw[:, 3 * W + 2 * LANES:]
        v0 = vec_ref[5:6, :]
        v = v + (vfirst_ref[...] - v) * jax.nn.sigmoid(v0 + _bdot(mv, v2_ref[...]))
    bd = bd_ref[...]
    kk = k * k_k
    ssq = _dot_exact_rhs(kk * kk, bd)
    kk = kk / jnp.maximum(jnp.sqrt(ssq), 1e-12)
    k = k * (1.0 + (a - 1.0) * k_a)
    bonus = _dot_exact_rhs(r * k * r_k, bd) * v

    r_ref[...] = r
    lw_ref[...] = lw
    k_ref[...] = k
    v_ref[...] = v
    a_ref[...] = -kk
    b_ref[...] = kk * a
    g_ref[...] = g
    bonus_ref[...] = bonus


def _inproj(x, shift_state, wts, v_first, seq):
    n = x.shape[0]
    first = v_first is None
    tm = ROW_TILE
    row = lambda w: pl.BlockSpec((tm, w), lambda i: (i, 0))
    if seq >= tm:
        assert seq % tm == 0
        st = shift_state.reshape(-1, 1, D_MODEL)
        st_spec = pl.BlockSpec((1, 1, D_MODEL), lambda i: (i // (seq // tm), 0, 0))
    else:
        assert tm % seq == 0
        st = shift_state
        st_spec = pl.BlockSpec((tm // seq, D_MODEL), lambda i: (i, 0))
    ins = [x, st, wts["watt"], wts["wrw"], wts["mu"], wts["wda"], wts["g2"], wts["vecA"], wts["bd"]]
    in_specs = [row(D_MODEL), st_spec] + [_const_spec(a.shape) for a in ins[2:]]
    if not first:
        ins += [v_first, wts["v2"]]
        in_specs += [row(RWKV_WIDTH), _const_spec(wts["v2"].shape)]
    widths = [ATT_WIDTH, KV_WIDTH, KV_WIDTH] + [RWKV_WIDTH] * 8
    return pl.pallas_call(
        functools.partial(_inproj_kernel, first=first, seq=seq),
        grid=(n // tm,),
        in_specs=in_specs,
        out_specs=[row(w) for w in widths],
        out_shape=[jax.ShapeDtypeStruct((n, w), F32) for w in widths],
        scratch_shapes=[pltpu.VMEM((SUBLANES, wts["wrw"].shape[1]), F32)],
        compiler_params=_params("arbitrary"),
        name="inproj",
    )(*ins)


def _dot3s(a_sp, b_sp, ca=1, cb=0):
    (ah, al), (bh, bl) = a_sp, b_sp
    return _dg(ah, bh, ca, cb) + (_dg(ah, bl, ca, cb) + _dg(al, bh, ca, cb))


def _dot1s(a_sp, b_sp, ca=1, cb=0):
    return _dg(a_sp[0], b_sp[0], ca, cb)


def _wkv_kernel(*refs, group, seq_chunk, has_state):
    if has_state:
        r_ref, lw_ref, k_ref, v_ref, a_ref, b_ref, s0_ref, y_ref, sout_ref, s_scr = refs
    else:
        r_ref, lw_ref, k_ref, v_ref, a_ref, b_ref, y_ref, sout_ref, s_scr = refs
    G, Ls = group, seq_chunk
    L = G * Ls
    H, N = N_RWKV_HEADS, HEAD_DIM
    c_idx = pl.program_id(1)

    @pl.when(c_idx == 0)
    def _init():
        if has_state:
            s_scr[...] = s0_ref[...]
        else:
            s_scr[...] = jnp.zeros_like(s_scr)

    row = lax.broadcasted_iota(jnp.int32, (L, L), 0)
    col = lax.broadcasted_iota(jnp.int32, (L, L), 1)
    same = (row // Ls) == (col // Ls)
    incl = same & (row >= col)
    strict = same & (row > col)
    eye = (row == col).astype(F32)
    row2 = lax.broadcasted_iota(jnp.int32, (2 * L, 2 * L), 0)
    col2 = lax.broadcasted_iota(jnp.int32, (2 * L, 2 * L), 1)
    t_q = row2 % L
    t_k = col2 % L
    mask2 = ((t_q // Ls) == (t_k // Ls)) & ((t_q > t_k) | ((row2 >= L) & (t_q == t_k)))

    lw_all = lw_ref[...]
    cum_all = _dot_exact_lhs(incl.astype(BF16), lw_all)
    if G == 1:
        tot_all = jnp.broadcast_to(cum_all[L - 1:L, :], cum_all.shape)
    else:
        tot_all = _dot_exact_lhs(same.astype(BF16), lw_all)
    e_end = jnp.exp(tot_all - cum_all)
    etot_all = jnp.exp(tot_all)
    a_all = a_ref[...]
    r_all = r_ref[...]
    b_all = b_ref[...]
    k_all = k_ref[...]
    v_all = v_ref[...]
    ar_all = jnp.concatenate([a_all * jnp.exp(cum_all - lw_all), r_all * jnp.exp(cum_all)], axis=0)
    ar_sp = _split2(ar_all)
    if G == 1:
        mid = cum_all[L // 2 - 1:L // 2, :]
        e_neg = jnp.exp(mid - cum_all)
        arc_sp = _split2(jnp.concatenate([a_all * jnp.exp(cum_all - lw_all - mid), r_all * jnp.exp(cum_all - mid)],
                                         axis=0))
    else:
        e_neg = jnp.exp(-cum_all)
        arc_sp = ar_sp
    bk_sp = _split2(jnp.concatenate([b_all * e_neg, k_all * e_neg], axis=0))
    bkh_sp = _split2(jnp.concatenate([b_all * e_end, k_all * e_end], axis=0))
    heads = range(H)
    hsl = lambda t, h: t[:, h * N:(h + 1) * N]
    hsp = lambda sp, h: (hsl(sp[0], h), hsl(sp[1], h))

    m = [jnp.where(mask2, _dg(hsl(arc_sp[0], h), hsl(bk_sp[0], h), 1, 1), 0.0) for h in heads]
    m_sp = [_split2(t) for t in m]
    a_ab = [t[:L, :L] for t in m]
    d = [eye + jnp.where((row // 2) == (col // 2), t, 0.0) for t in a_ab]
    n = 4
    while n <= Ls:
        off = ((row // n) == (col // n)) & ((row // (n // 2)) != (col // (n // 2)))
        d_b = [t.astype(BF16) for t in d]
        dn = [_dg(t, jnp.where(off, s, 0.0).astype(BF16), 1, 0) for t, s in zip(d_b, a_ab)]
        d = [t + _dg(p.astype(BF16), tb, 1, 0) for t, p, tb in zip(d, dn, d_b)]
        n *= 2
    resid = [eye - t + _dot3s((s[0][:L, :L], s[1][:L, :L]), _split2(t)) for t, s in zip(d, m_sp)]
    t_inv = [t + _dg(t.astype(BF16), r.astype(BF16), 1, 0) for t, r in zip(d, resid)]

    s0 = [[s_scr[g, h] for h in heads] for g in range(G)]
    s0_sp = [[_split2(s0[g][h]) for h in heads] for g in range(G)]

    def seq_rows(sp, g, h):
        if G == 1:
            return hsp(sp, h)
        return tuple(jnp.concatenate([hsl(p, h)[g * Ls:(g + 1) * Ls], hsl(p, h)[L + g * Ls:L + (g + 1) * Ls]], axis=0)
                     for p in sp)

    st = [[_dot1s(seq_rows(ar_sp, g, h), s0_sp[g][h], 1, 1) for h in heads] for g in range(G)]
    rhs_st = [jnp.concatenate([st[g][h][:Ls] for g in range(G)], axis=0) for h in heads]
    y_st = [jnp.concatenate([st[g][h][Ls:] for g in range(G)], axis=0) for h in heads]

    v_h = [hsl(v_all, h) for h in heads]
    zeros = jnp.zeros((L, N), F32)
    m_top = [(s[0][:L], s[1][:L]) for s in m_sp]
    m_bot = [(s[0][L:], s[1][L:]) for s in m_sp]
    rhs = [rhs_st[h] + _dot1s(m_top[h], _split2(jnp.concatenate([zeros, v_h[h]], axis=0))) for h in heads]
    u = [_dot1s(_split2(t_inv[h]), _split2(rhs[h])) for h in heads]
    uv_sp = [_split2(jnp.concatenate([u[h], v_h[h]], axis=0)) for h in heads]
    y = [y_st[h] + _dg(m_bot[h][0], uv_sp[h][0], 1, 0) for h in heads]
    for g in range(G):
        for h in heads:
            if G == 1:
                uv_g = uv_sp[h]
            else:
                uv_g = tuple(jnp.concatenate([p[g * Ls:(g + 1) * Ls], p[L + g * Ls:L + (g + 1) * Ls]], axis=0)
                             for p in uv_sp[h])
            upd = _dot1s(uv_g, seq_rows(bkh_sp, g, h), 0, 0)
            s_scr[g, h] = s0[g][h] * hsl(etot_all, h)[g * Ls:g * Ls + 1] + upd
    ys = []
    for h in heads:
        mu = jnp.mean(y[h], axis=-1, keepdims=True)
        var = jnp.mean(jnp.square(y[h] - mu), axis=-1, keepdims=True)
        ys.append((y[h] - mu) * lax.rsqrt(var + GN_EPS))
    y_ref[...] = jnp.concatenate(ys, axis=1)

    @pl.when(c_idx == pl.num_programs(1) - 1)
    def _fin():
        sout_ref[...] = s_scr[...]


def _wkv_scan(r, lw, k, v, a, b, state, n_seq, seq, seq_chunk, group):
    n_chunks = seq // seq_chunk
    assert group == 1 or n_chunks == 1
    has_state = state is not None
    rows = group * seq_chunk
    tok_spec = pl.BlockSpec((rows, RWKV_WIDTH), lambda i, c: (i * n_chunks + c, 0))
    st_spec = pl.BlockSpec((group, N_RWKV_HEADS, HEAD_DIM, HEAD_DIM), lambda i, c: (i, 0, 0, 0))
    in_specs = [tok_spec] * 6
    args = (r, lw, k, v, a, b)
    if has_state:
        states, layer = state
        in_specs.append(pl.BlockSpec((None, group, N_RWKV_HEADS, HEAD_DIM, HEAD_DIM),
                                     lambda i, c: (layer, i, 0, 0, 0)))
        args += (states,)
    return pl.pallas_call(
        functools.partial(_wkv_kernel, group=group, seq_chunk=seq_chunk, has_state=has_state),
        grid=(n_seq // group, n_chunks),
        in_specs=in_specs,
        out_specs=[tok_spec, st_spec],
        out_shape=[jax.ShapeDtypeStruct((n_seq * seq, RWKV_WIDTH), F32),
                   jax.ShapeDtypeStruct((n_seq, N_RWKV_HEADS, HEAD_DIM, HEAD_DIM), F32)],
        scratch_shapes=[pltpu.VMEM((group, N_RWKV_HEADS, HEAD_DIM, HEAD_DIM), F32)],
        compiler_params=_params("arbitrary", "arbitrary"),
        name="wkv_scan",
    )(*args)


def _sink_softmax(s, sink):
    m = sink
    for t in s:
        m = jnp.maximum(m, jnp.max(t, axis=-1, keepdims=True))
    es = [jnp.exp(t - m) for t in s]
    den = jnp.exp(sink - m)
    for e in es:
        den = den + jnp.sum(e, axis=-1, keepdims=True)
    return [e / den for e in es]


def _attn_prompt_kernel(q_ref, kc_ref, kp_ref, vc_ref, vp_ref, sink_ref, o_ref):
    L = WINDOW
    n = pl.program_id(1)
    q = q_ref[...].astype(BF16)
    kw = jnp.concatenate([kp_ref[...], kc_ref[...]], axis=0).astype(BF16)
    vw = jnp.concatenate([vp_ref[...], vc_ref[...]], axis=0).astype(BF16)
    qi = lax.broadcasted_iota(jnp.int32, (L, 2 * L), 0)
    kj = lax.broadcasted_iota(jnp.int32, (L, 2 * L), 1)
    diff = qi + L - kj
    mask = (diff >= 0) & (diff < WINDOW) & ((kj >= L) | (n > 0))
    lane = lax.broadcasted_iota(jnp.int32, vw.shape, 1)
    one = jnp.ones_like(vw)
    vext = [jnp.where(lane < HEAD_DIM, vw, one), jnp.where(lane >= HEAD_DIM, vw, one)]
    heads = range(N_Q_HEADS)
    hsl = lambda t, h: t[:, h * HEAD_DIM:(h + 1) * HEAD_DIM]
    ss = [jnp.where(mask, _dg(hsl(q, h), hsl(kw, h // Q_PER_KV), 1, 1) * ATT_SCALE, -jnp.inf) for h in heads]
    sinks = [sink_ref[h:h + 1, 0:1] for h in heads]
    ms = [jnp.maximum(jnp.max(ss[h], axis=-1, keepdims=True), sinks[h]) for h in heads]
    es = [jnp.exp(ss[h] - ms[h]).astype(BF16) for h in heads]
    oes = [_dg(es[h], vext[h // Q_PER_KV], 1, 0) for h in heads]
    outs = []
    for h in heads:
        hk = h // Q_PER_KV
        rs = oes[h][:, (1 - hk) * HEAD_DIM:(1 - hk) * HEAD_DIM + 1]
        outs.append(hsl(oes[h], hk) / (rs + jnp.exp(sinks[h] - ms[h])))
    o_ref[...] = jnp.concatenate(outs, axis=1)


def _attn_prompt(q, k, v, sink_rows, n_batch, seq):
    nb = seq // WINDOW
    cur = lambda w: pl.BlockSpec((WINDOW, w), lambda b, n: (b * nb + n, 0))
    prev = lambda w: pl.BlockSpec((WINDOW, w), lambda b, n: (b * nb + jnp.maximum(n - 1, 0), 0))
    return pl.pallas_call(
        _attn_prompt_kernel,
        grid=(n_batch, nb),
        in_specs=[cur(ATT_WIDTH), cur(KV_WIDTH), prev(KV_WIDTH), cur(KV_WIDTH), prev(KV_WIDTH),
                  _const_spec(sink_rows.shape)],
        out_specs=cur(ATT_WIDTH),
        out_shape=jax.ShapeDtypeStruct((n_batch * seq, ATT_WIDTH), F32),
        compiler_params=_params("arbitrary", "arbitrary"),
        name="attn_prompt",
    )(q, k, k, v, v, sink_rows)


def _attn_sample_kernel(q_ref, kn_ref, vn_ref, ck_ref, cv_ref, sink_ref, o_ref, *, seq):
    S = seq
    W = WINDOW
    B = SAMPLE_ATT_BATCH
    R = Q_PER_KV * S
    tq1 = lax.broadcasted_iota(jnp.int32, (R, W), 0) % S
    kj1 = lax.broadcasted_iota(jnp.int32, (R, W), 1)
    mask_cache = kj1 > tq1
    tq2 = lax.broadcasted_iota(jnp.int32, (R, S), 0) % S
    kj2 = lax.broadcasted_iota(jnp.int32, (R, S), 1)
    mask_new = kj2 <= tq2
    kvs = range(N_KV_HEADS)
    hsl = lambda t, h: t[:, h * HEAD_DIM:(h + 1) * HEAD_DIM]
    group = lambda hk: range(hk * Q_PER_KV, (hk + 1) * Q_PER_KV)
    seq_rows = lambda t, b: t[b * S:(b + 1) * S]
    q = q_ref[...]
    kn = kn_ref[...]
    vn = vn_ref[...]
    qh = [hsl(q, h) for h in range(N_Q_HEADS)]
    knh = [hsl(kn, hk) for hk in kvs]
    vnh = [hsl(vn, hk) for hk in kvs]
    sinks = [jnp.concatenate([jnp.broadcast_to(sink_ref[h:h + 1, 0:1], (S, 1)) for h in group(hk)], axis=0)
             for hk in kvs]
    items = [(b, hk) for b in range(B) for hk in kvs]
    qs = [jnp.concatenate([seq_rows(qh[h], b) for h in group(hk)], axis=0).astype(BF16) for b, hk in items]
    ck = [ck_ref[b].astype(BF16) for b in range(B)]
    cv = [cv_ref[b].astype(BF16) for b in range(B)]
    s1 = [jnp.where(mask_cache, _dg(qs[i], hsl(ck[b], hk), 1, 1) * ATT_SCALE, -jnp.inf)
          for i, (b, hk) in enumerate(items)]
    s2 = [jnp.where(mask_new, _dg(qs[i], seq_rows(knh[hk], b).astype(BF16), 1, 1) * ATT_SCALE, -jnp.inf)
          for i, (b, hk) in enumerate(items)]
    ps = [_sink_softmax([s1[i], s2[i]], sinks[hk]) for i, (b, hk) in enumerate(items)]
    outs = [_dg(ps[i][0].astype(BF16), hsl(cv[b], hk), 1, 0)
            + _dg(ps[i][1].astype(BF16), seq_rows(vnh[hk], b).astype(BF16), 1, 0)
            for i, (b, hk) in enumerate(items)]
    rows = [jnp.concatenate([outs[b * N_KV_HEADS + hk][g * S:(g + 1) * S] for hk in kvs for g in range(Q_PER_KV)],
                            axis=1) for b in range(B)]
    o_ref[...] = jnp.concatenate(rows, axis=0)


def _attn_sample(q, k, v, cache_k, cache_v, sink_rows, n_batch, seq):
    bb = SAMPLE_ATT_BATCH
    tok = lambda w: pl.BlockSpec((bb * seq, w), lambda i: (i, 0))
    cache = pl.BlockSpec((bb, WINDOW, KV_WIDTH), lambda i: (i, 0, 0))
    return pl.pallas_call(
        functools.partial(_attn_sample_kernel, seq=seq),
        grid=(n_batch // bb,),
        in_specs=[tok(ATT_WIDTH), tok(KV_WIDTH), tok(KV_WIDTH), cache, cache, _const_spec(sink_rows.shape)],
        out_specs=tok(ATT_WIDTH),
        out_shape=jax.ShapeDtypeStruct((n_batch * seq, ATT_WIDTH), F32),
        compiler_params=_params("arbitrary"),
        name="attn_sample",
    )(q, k, v, cache_k, cache_v, sink_rows)


def _second_max4(a, b, c, d):
    return jnp.maximum(jnp.maximum(jnp.minimum(a, b), jnp.minimum(c, d)),
                       jnp.minimum(jnp.maximum(a, b), jnp.maximum(c, d)))


def _route(logits_t, bias_col):
    G, E = N_EXPERT_GROUPS, EXPERTS_PER_GROUP
    m = jnp.max(logits_t, axis=0, keepdims=True)
    ex = jnp.exp(logits_t - m)
    probs = ex / jnp.sum(ex, axis=0, keepdims=True)
    sel = probs + bias_col
    p = [probs[e:e + 1, :] for e in range(N_EXPERTS)]
    s = [sel[e:e + 1, :] for e in range(N_EXPERTS)]
    gs = []
    for g in range(G):
        a, b, c, d = s[E * g:E * g + E]
        top1 = jnp.maximum(jnp.maximum(a, b), jnp.maximum(c, d))
        gs.append(top1 + _second_max4(a, b, c, d))
    best = jnp.zeros_like(gs[0], dtype=jnp.int32)
    best_s = gs[0]
    for g in range(1, G):
        upd = gs[g] > best_s
        best = jnp.where(upd, g, best)
        best_s = jnp.where(upd, gs[g], best_s)

    def pick(vals, j):
        out = vals[j]
        for g in range(1, G):
            out = jnp.where(best == g, vals[E * g + j], out)
        return out

    ig = [pick(s, j) for j in range(E)]
    pg = [pick(p, j) for j in range(E)]
    l1 = jnp.zeros_like(best)
    v1 = ig[0]
    for j in range(1, E):
        upd = ig[j] > v1
        l1 = jnp.where(upd, j, l1)
        v1 = jnp.where(upd, ig[j], v1)
    l2 = jnp.full_like(best, -1)
    v2 = jnp.full_like(v1, -jnp.inf)
    for j in range(E):
        upd = (l1 != j) & (ig[j] > v2)
        l2 = jnp.where(upd, j, l2)
        v2 = jnp.where(upd, ig[j], v2)
    zero = jnp.zeros_like(v1)
    w1 = zero
    w2 = zero
    for j in range(E):
        w1 = jnp.where(l1 == j, pg[j], w1)
        w2 = jnp.where(l2 == j, pg[j], w2)
    wsum = w1 + w2
    w1 = w1 / wsum
    w2 = w2 / wsum
    e1 = (best * E + l1).astype(F32)
    e2 = (best * E + l2).astype(F32)
    return jnp.concatenate([e1, e2, w1, w2, zero, zero, zero, zero], axis=0)


def _post_kernel(yn_ref, bonus_ref, g_ref, att_ref, x_ref, woa_ref, wor_ref, gn_ref, ln_ref,
                 wrt_ref, rb_ref, x1_ref, gates_ref):
    rw_out = (yn_ref[...] * gn_ref[0:1, :] + gn_ref[1:2, :] + bonus_ref[...]) * g_ref[...]
    mixed = _bdot(att_ref[...], woa_ref[...]) + _bdot(rw_out, wor_ref[...])
    x1 = _layer_norm(ALPHA * x_ref[...] + mixed, ln_ref[0:1, :], ln_ref[1:2, :])
    x1_ref[...] = x1
    logits_t = _dot3(wrt_ref[...], x1, 1, 1)
    route_t = _route(logits_t, rb_ref[:, 0:1])
    pad = jnp.zeros((LANES - route_t.shape[0], route_t.shape[1]), F32)
    gates_ref[...] = jnp.concatenate([route_t, pad], axis=0).T


def _post(yn, bonus, g, att, x, wts, glob):
    n = x.shape[0]
    tm = ROW_TILE
    row = lambda w: pl.BlockSpec((tm, w), lambda i: (i, 0))
    consts = [wts["woa"], wts["wor"], wts["gn"], wts["ln1"], glob["wrt"], glob["rb"]]
    return pl.pallas_call(
        _post_kernel,
        grid=(n // tm,),
        in_specs=[row(RWKV_WIDTH)] * 3 + [row(ATT_WIDTH), row(D_MODEL)] + [_const_spec(a.shape) for a in consts],
        out_specs=[row(D_MODEL), row(LANES)],
        out_shape=[jax.ShapeDtypeStruct((n, D_MODEL), F32), jax.ShapeDtypeStruct((n, LANES), F32)],
        compiler_params=_params("arbitrary"),
        name="post_mix",
    )(yn, bonus, g, att, x, *consts)


def _moe_positions_kernel(gates_ref, pos_ref, cnt_ref, cnt_scr, offs_scr, carry_scr):
    ph = pl.program_id(0)
    i = pl.program_id(1)
    g = gates_ref[...]
    tm = g.shape[0]
    lane = lax.broadcasted_iota(jnp.int32, g.shape, 1).astype(F32)
    oh1 = (lane == g[:, 0:1]).astype(F32)
    oh2 = (lane == g[:, 1:2]).astype(F32)
    oh = oh1 + oh2

    @pl.when((ph == 0) & (i == 0))
    def _zero():
        cnt_scr[...] = jnp.zeros_like(cnt_scr)

    @pl.when(ph == 0)
    def _count():
        cnt_scr[...] += jnp.sum(oh, axis=0, keepdims=True)
        pos_ref[...] = jnp.zeros_like(pos_ref)

    @pl.when((ph == 1) & (i == 0))
    def _offsets():
        cnt = cnt_scr[...]
        padded = jnp.floor((cnt + (MOE_BLOCK - 1)) * (1.0 / MOE_BLOCK)) * MOE_BLOCK
        r = lax.broadcasted_iota(jnp.int32, (LANES, LANES), 0)
        c = lax.broadcasted_iota(jnp.int32, (LANES, LANES), 1)
        offs_scr[...] = _dot_exact_rhs(padded, (r < c).astype(BF16))
        carry_scr[...] = jnp.zeros_like(carry_scr)
        cnt_ref[...] = cnt

    @pl.when(ph == 1)
    def _rank():
        r = lax.broadcasted_iota(jnp.int32, (tm, tm), 0)
        c = lax.broadcasted_iota(jnp.int32, (tm, tm), 1)
        before = _dg((r > c).astype(BF16), oh.astype(BF16), 1, 0)
        base = before + carry_scr[0:1, :] + offs_scr[0:1, :]
        p1 = jnp.sum(oh1 * base, axis=1, keepdims=True)
        p2 = jnp.sum(oh2 * base, axis=1, keepdims=True)
        pos_ref[...] = jnp.where(lane == 0.0, p1, jnp.where(lane == 1.0, p2, 0.0))
        carry_scr[...] += jnp.sum(oh, axis=0, keepdims=True)


def _moe_positions(gates):
    n = gates.shape[0]
    tm = min(MOE_POS_TILE, n)
    stat = pltpu.VMEM((SUBLANES, LANES), F32)
    return pl.pallas_call(
        _moe_positions_kernel,
        grid=(2, n // tm),
        in_specs=[pl.BlockSpec((tm, LANES), lambda ph, i: (i, 0))],
        out_specs=[pl.BlockSpec((tm, LANES), lambda ph, i: (i * ph, 0)), _const_spec((SUBLANES, LANES))],
        out_shape=[jax.ShapeDtypeStruct((n, LANES), F32), jax.ShapeDtypeStruct((SUBLANES, LANES), F32)],
        scratch_shapes=[stat, stat, stat],
        compiler_params=_params("arbitrary", "arbitrary"),
        name="moe_positions",
    )(gates)


def _moe_dispatch_kernel(ends_ref, pos_ref, x_ref, xs_ref, zero_scr, sem):
    tm = x_ref.shape[0]
    R = zero_scr.shape[0]

    @pl.when(pl.program_id(0) == 0)
    def _zero_tail_blocks():
        zero_scr[...] = jnp.zeros_like(zero_scr)

        def tail_copy(e):
            return pltpu.make_async_copy(zero_scr, xs_ref.at[pl.ds((ends_ref[e] - 1) * R, R), :], sem)

        def non_empty(e):
            return ends_ref[e] > (ends_ref[e - 1] if e else 0)

        n_blocks = xs_ref.shape[0] // R
        n_used = ends_ref[N_EXPERTS - 1]

        def spare_copy(j):
            return pltpu.make_async_copy(zero_scr, xs_ref.at[pl.ds((n_used + j) * R, R), :], sem)

        for e in range(N_EXPERTS):
            pl.when(non_empty(e))(lambda e=e: tail_copy(e).start())
            pl.when(n_used + e < n_blocks)(lambda e=e: spare_copy(e).start())
        for e in range(N_EXPERTS):
            pl.when(non_empty(e))(lambda e=e: tail_copy(e).wait())
            pl.when(n_used + e < n_blocks)(lambda e=e: spare_copy(e).wait())

    def row_copy(r, k):
        return pltpu.make_async_copy(x_ref.at[pl.ds(r, 1), :], xs_ref.at[pl.ds(pos_ref[k, r], 1), :], sem)

    def issue(r, carry):
        row_copy(r, 0).start(priority=0)
        row_copy(r, 1).start(priority=1)
        return carry

    lax.fori_loop(0, tm, issue, 0, unroll=DMA_ISSUE_UNROLL)
    for _ in range(N_EXPERTS_PER_TOKEN):
        pltpu.make_async_copy(x_ref, xs_ref.at[pl.ds(0, tm), :], sem).wait()


def _moe_dispatch(x1, pos, ends, n_slots):
    n = x1.shape[0]
    tm = MOE_DMA_TILE
    return pl.pallas_call(
        _moe_dispatch_kernel,
        grid=(n // tm,),
        in_specs=[pl.BlockSpec(memory_space=pltpu.SMEM),
                  pl.BlockSpec((2, tm), lambda i: (0, i), memory_space=pltpu.SMEM),
                  pl.BlockSpec((tm, D_MODEL), lambda i: (i, 0))],
        out_specs=pl.BlockSpec(memory_space=pl.ANY),
        out_shape=jax.ShapeDtypeStruct((n_slots, D_MODEL), F32),
        scratch_shapes=[pltpu.VMEM((MOE_BLOCK, D_MODEL), F32), pltpu.SemaphoreType.DMA],
        compiler_params=_params("arbitrary"),
        name="moe_dispatch",
    )(ends, pos, x1)


def _moe_experts_kernel(blk_expert_ref, n_used_ref, xs_ref, wg_ref, wu_ref, wd_ref, ys_ref, wg_b, wu_b, wd_b):
    b = pl.program_id(0)
    used = b < n_used_ref[0]
    new_expert = (b == 0) | (blk_expert_ref[b] != blk_expert_ref[jnp.maximum(b - 1, 0)])

    @pl.when(used & new_expert)
    def _cast_weights():
        wg_b[...] = wg_ref[0, 0].astype(BF16)
        wu_b[...] = wu_ref[0, 0].astype(BF16)
        wd_b[...] = wd_ref[0, 0].astype(BF16)

    @pl.when(used)
    def _compute():
        xb = xs_ref[...].astype(BF16)
        h = jax.nn.silu(_dg(xb, wg_b[...], 1, 0)) * _dg(xb, wu_b[...], 1, 0)
        ys_ref[...] = _dg(h.astype(BF16), wd_b[...], 1, 0)

    @pl.when(b >= n_used_ref[0])
    def _skip():
        ys_ref[...] = jnp.zeros_like(ys_ref)


def _moe_experts(xs, blk_expert, n_used, glob, l):
    n_slots = xs.shape[0]
    R = MOE_BLOCK
    last = lambda b, nu: jnp.minimum(b, nu[0] - 1)
    wspec = lambda s: pl.BlockSpec((1, 1) + s, lambda b, be, nu: (l, be[last(b, nu)], 0, 0))
    return pl.pallas_call(
        _moe_experts_kernel,
        grid_spec=pltpu.PrefetchScalarGridSpec(
            num_scalar_prefetch=2,
            grid=(n_slots // R,),
            in_specs=[pl.BlockSpec((R, D_MODEL), lambda b, be, nu: (last(b, nu), 0)),
                      wspec((D_MODEL, EXPERT_FF)), wspec((D_MODEL, EXPERT_FF)), wspec((EXPERT_FF, D_MODEL))],
            out_specs=pl.BlockSpec((R, D_MODEL), lambda b, be, nu: (b, 0)),
            scratch_shapes=[pltpu.VMEM((D_MODEL, EXPERT_FF), BF16), pltpu.VMEM((D_MODEL, EXPERT_FF), BF16),
                            pltpu.VMEM((EXPERT_FF, D_MODEL), BF16)],
        ),
        out_shape=jax.ShapeDtypeStruct((n_slots, D_MODEL), F32),
        compiler_params=_params("arbitrary"),
        name="moe_experts",
    )(blk_expert, n_used, xs, glob["wg"], glob["wu"], glob["wd"])


def _moe_combine_kernel(pos_ref, x_ref, gates_ref, ys_ref, ln_ref, o_ref, buf, sem):
    tm = x_ref.shape[0]

    def row_copy(r, k):
        return pltpu.make_async_copy(ys_ref.at[pl.ds(pos_ref[k, r], 1), :], buf.at[k, pl.ds(r, 1), :], sem)

    def issue(r, carry):
        row_copy(r, 0).start(priority=0)
        row_copy(r, 1).start(priority=1)
        return carry

    lax.fori_loop(0, tm, issue, 0, unroll=DMA_ISSUE_UNROLL)
    for k in range(N_EXPERTS_PER_TOKEN):
        pltpu.make_async_copy(ys_ref.at[pl.ds(0, tm), :], buf.at[k], sem).wait()
    g = gates_ref[...]
    ffn = g[:, 2:3] * buf[0] + g[:, 3:4] * buf[1]
    o_ref[...] = _layer_norm(ALPHA * x_ref[...] + ffn, ln_ref[0:1, :], ln_ref[1:2, :])


def _moe_combine(x1, gates, pos, ys, wts):
    n = x1.shape[0]
    tm = MOE_DMA_TILE
    return pl.pallas_call(
        _moe_combine_kernel,
        grid=(n // tm,),
        in_specs=[pl.BlockSpec((2, tm), lambda i: (0, i), memory_space=pltpu.SMEM),
                  pl.BlockSpec((tm, D_MODEL), lambda i: (i, 0)),
                  pl.BlockSpec((tm, LANES), lambda i: (i, 0)),
                  pl.BlockSpec(memory_space=pl.ANY),
                  _const_spec(wts["ln2"].shape)],
        out_specs=pl.BlockSpec((tm, D_MODEL), lambda i: (i, 0)),
        out_shape=jax.ShapeDtypeStruct((n, D_MODEL), F32),
        scratch_shapes=[pltpu.VMEM((2, tm, D_MODEL), F32), pltpu.SemaphoreType.DMA],
        compiler_params=_params("arbitrary"),
        name="moe_combine",
    )(pos, x1, gates, ys, wts["ln2"])


def _moe(x1, gates, wts, glob, l):
    n = x1.shape[0]
    R = MOE_BLOCK
    n_slots = N_EXPERTS_PER_TOKEN * n + N_EXPERTS * R
    pos_f, cnt = _moe_positions(gates)
    pos = pos_f[:, :N_EXPERTS_PER_TOKEN].astype(jnp.int32).T
    blocks = (cnt[0, :N_EXPERTS].astype(jnp.int32) + (R - 1)) // R
    ends = jnp.cumsum(blocks)
    n_used = ends[-1:]
    blk_expert = jnp.sum(jnp.arange(n_slots // R, dtype=jnp.int32)[:, None] >= ends[None, :], axis=1)
    blk_expert = jnp.minimum(blk_expert, N_EXPERTS - 1).astype(jnp.int32)
    xs = _moe_dispatch(x1, pos, ends.astype(jnp.int32), n_slots)
    ys = _moe_experts(xs, blk_expert, n_used, glob, l)
    return _moe_combine(x1, gates, pos, ys, wts)


def _rows8(vectors, width):
    rows = [v.reshape(1, width).astype(F32) for v in vectors]
    rows.append(jnp.zeros((8 - len(rows), width), F32))
    return jnp.concatenate(rows, axis=0)


def _prep_layer(l, p):
    W = RWKV_WIDTH
    w_l = p["w_in"][l]
    rwc = w_l[:, N_ATT_COLS:]
    mu = p["mu_rwkv"][l]
    o_w, o_k, o_v, o_a, o_g = W, W + D_DECAY_LORA, 2 * W + D_DECAY_LORA, 3 * W + D_DECAY_LORA, 3 * W + 2 * D_DECAY_LORA

    def reorder(t):
        parts = [t[..., 0:W], t[..., o_k:o_k + W], t[..., o_v:o_v + W], t[..., o_g:o_g + D_GATE_LORA],
                 t[..., o_w:o_w + D_DECAY_LORA], t[..., o_a:o_a + D_AAA_LORA]]
        return parts

    w_parts = reorder(rwc)
    mu_parts = reorder(mu)
    if l > 0:
        padw = LANES - D_MV_LORA
        w_parts += [p["w_vres_in"][l - 1], jnp.zeros((D_MODEL, padw), F32)]
        mu_parts += [p["mu_vres"][l - 1], jnp.zeros((padw,), F32)]
    wrw = jnp.concatenate(w_parts, axis=1).astype(BF16)
    mu_row = jnp.concatenate(mu_parts).reshape(1, -1)
    zero = jnp.zeros((D_DECAY_LORA, W), F32)
    wda = jnp.concatenate([jnp.concatenate([p["decay_w2"][l], zero], axis=1),
                           jnp.concatenate([zero, p["aaa_a2"][l]], axis=1)], axis=0).astype(BF16)
    vecs = [p["decay_w0"][l], p["aaa_a0"][l], p["k_k"][l], p["k_a"][l], p["r_k"][l].reshape(W)]
    out = {
        "watt": w_l[:, :N_ATT_COLS].astype(BF16),
        "wrw": wrw,
        "mu": mu_row,
        "wda": wda,
        "g2": p["gate_g2"][l].astype(BF16),
        "woa": p["w_o"][l][:ATT_WIDTH].astype(BF16),
        "wor": p["w_o"][l][ATT_WIDTH:].astype(BF16),
        "gn": _rows8([p["gn_g"][l], p["gn_b"][l]], W),
        "ln1": _rows8([p["ln1_g"][l], p["ln1_b"][l]], D_MODEL),
        "ln2": _rows8([p["ln2_g"][l], p["ln2_b"][l]], D_MODEL),
        "sink_rows": jnp.broadcast_to(p["sinks"][l].reshape(N_Q_HEADS, 1), (N_Q_HEADS, LANES)).astype(F32),
    }
    if l > 0:
        vecs.append(p["vres_v0"][l - 1])
        out["v2"] = jnp.concatenate([p["vres_v2"][l - 1], jnp.zeros((LANES - D_MV_LORA, W), F32)], axis=0).astype(BF16)
    out["vecA"] = _rows8(vecs, W)
    hid = jnp.arange(W) // HEAD_DIM
    out["bd"] = (hid[:, None] == hid[None, :]).astype(BF16)
    return out


def _trunk(x3, shift_prev, cache_k, cache_v, wkv_prev, layer_wts, glob):
    decode = cache_k is not None
    n_batch, seq, _ = x3.shape
    x = x3.reshape(n_batch * seq, D_MODEL)
    new_k, new_v, new_wkv, new_shift = [], [], [], []
    v_first = None
    for l in range(DEPTH):
        wts = layer_wts[l]
        new_shift.append(x.reshape(n_batch, seq, D_MODEL)[:, -1])
        q, ka, va, r, lw, k, v, a, b, g, bonus = _inproj(x, shift_prev[l], wts, v_first, seq)
        if l == 0:
            v_first = v
        if decode:
            ck = cache_k[l].reshape(n_batch, WINDOW, KV_WIDTH)
            cv = cache_v[l].reshape(n_batch, WINDOW, KV_WIDTH)
            att = _attn_sample(q, ka, va, ck, cv, wts["sink_rows"], n_batch, seq)
            new_k.append(ka.reshape(n_batch, seq, N_KV_HEADS, HEAD_DIM))
            new_v.append(va.reshape(n_batch, seq, N_KV_HEADS, HEAD_DIM))
            yn, s_out = _wkv_scan(r, lw, k, v, a, b, (wkv_prev[l][None], 0), n_batch, seq, seq, SAMPLE_WKV_GROUP)
        else:
            att = _attn_prompt(q, ka, va, wts["sink_rows"], n_batch, seq)
            last = lambda t: t.reshape(n_batch, seq, KV_WIDTH)[:, -WINDOW:].reshape(n_batch, WINDOW, N_KV_HEADS, HEAD_DIM)
            new_k.append(last(ka))
            new_v.append(last(va))
            yn, s_out = _wkv_scan(r, lw, k, v, a, b, None, n_batch, seq, WKV_CHUNK, 1)
        new_wkv.append(s_out)
        x1, gates = _post(yn, bonus, g, att, x, wts, glob)
        x = _moe(x1, gates, wts, glob, l)
    return (x.reshape(n_batch, seq, D_MODEL), jnp.stack(new_k), jnp.stack(new_v), jnp.stack(new_wkv),
            jnp.stack(new_shift))


def kernel(x_prompt, x_sample, cache_k, cache_v, state_wkv, state_shift, w_in, w_vres_in, mu_rwkv, mu_vres, sinks, decay_w0, decay_w2, aaa_a0, aaa_a2, vres_v0, vres_v2, gate_g2, k_k, k_a, r_k, gn_g, gn_b, w_o, ln1_g, ln1_b, w_router, router_bias, w_gate, w_up, w_down, ln2_g, ln2_b):
    p = dict(w_in=w_in, w_vres_in=w_vres_in, mu_rwkv=mu_rwkv, mu_vres=mu_vres, sinks=sinks,
             decay_w0=decay_w0, decay_w2=decay_w2, aaa_a0=aaa_a0, aaa_a2=aaa_a2,
             vres_v0=vres_v0, vres_v2=vres_v2, gate_g2=gate_g2, k_k=k_k, k_a=k_a, r_k=r_k,
             gn_g=gn_g, gn_b=gn_b, w_o=w_o, ln1_g=ln1_g, ln1_b=ln1_b,
             w_gate=w_gate, w_up=w_up, w_down=w_down, ln2_g=ln2_g, ln2_b=ln2_b)
    layer_wts = [_prep_layer(l, p) for l in range(DEPTH)]
    glob = {
        "wrt": w_router.T.astype(F32),
        "rb": jnp.broadcast_to(router_bias.reshape(N_EXPERTS, 1), (N_EXPERTS, LANES)).astype(F32),
        "wg": w_gate,
        "wu": w_up,
        "wd": w_down,
    }
    b_p = x_prompt.shape[0]
    zero_shift = jnp.zeros((DEPTH, b_p, D_MODEL), x_prompt.dtype)
    y_p, k_p, v_p, wkv_p, shift_p = _trunk(x_prompt, zero_shift, None, None, None, layer_wts, glob)
    y_s, k_s, v_s, wkv_s, shift_s = _trunk(x_sample, state_shift, cache_k, cache_v, state_wkv, layer_wts, glob)
    return (y_p, y_s, k_p, v_p, wkv_p, shift_p, k_s, v_s, wkv_s, shift_s)
```

```python
import functools
import math

import jax
import jax.numpy as jnp
from jax import lax
from jax.experimental import pallas as pl
from jax.experimental.pallas import tpu as pltpu

F32 = jnp.float32
BF16 = jnp.bfloat16

D_MODEL = 1024
DEPTH = 4
HEAD_DIM = 64
ATT_WIDTH = 512
RWKV_WIDTH = 512
N_Q_HEADS = 8
N_KV_HEADS = 2
Q_PER_KV = 4
KV_WIDTH = 128
N_ATT_COLS = ATT_WIDTH + 2 * KV_WIDTH
WINDOW = 128
ATT_SCALE = HEAD_DIM ** -0.5
N_RWKV_HEADS = 8
D_DECAY_LORA = 64
D_AAA_LORA = 64
D_GATE_LORA = 128
D_MV_LORA = 32
DECAY_SCALE = math.exp(-0.5)
GN_EPS = 64e-5
LN_EPS = 1e-5
N_EXPERTS = 16
N_EXPERT_GROUPS = 4
EXPERTS_PER_GROUP = 4
EXPERT_FF = 512
ALPHA = (2 * DEPTH) ** 0.25

LANES = 128
SUBLANES = 8
VMEM_LIMIT_BYTES = 56 * 1024 * 1024
ROW_TILE = 256
MOE_BLOCK = 512
MOE_POS_TILE = 512
MOE_DMA_TILE = 1024
N_EXPERTS_PER_TOKEN = 2
DMA_ISSUE_UNROLL = 16
WKV_CHUNK = 128
SAMPLE_ATT_BATCH = 8
SAMPLE_WKV_GROUP = 8


def _dg(a, b, ca, cb):
    return lax.dot_general(a, b, (((ca,), (cb,)), ((), ())), preferred_element_type=F32)


def _bdot(a, b):
    return _dg(a.astype(BF16), b.astype(BF16), 1, 0)


def _split2(x):
    hi = x.astype(BF16)
    lo = (x - hi.astype(F32)).astype(BF16)
    return hi, lo


def _dot3(a, b, ca=1, cb=0):
    ah, al = _split2(a)
    bh, bl = _split2(b)
    return _dg(ah, bh, ca, cb) + (_dg(ah, bl, ca, cb) + _dg(al, bh, ca, cb))


def _dot_exact_lhs(m_bf16, x, parts=3):
    acc = None
    rem = x
    for _ in range(parts):
        p = rem.astype(BF16)
        t = _dg(m_bf16, p, 1, 0)
        acc = t if acc is None else acc + t
        rem = rem - p.astype(F32)
    return acc


def _dot_exact_rhs(x, m_bf16, parts=3):
    acc = None
    rem = x
    for _ in range(parts):
        p = rem.astype(BF16)
        t = _dg(p, m_bf16, 1, 0)
        acc = t if acc is None else acc + t
        rem = rem - p.astype(F32)
    return acc


def _layer_norm(z, g, b):
    mu = jnp.mean(z, axis=-1, keepdims=True)
    var = jnp.mean(jnp.square(z - mu), axis=-1, keepdims=True)
    return (z - mu) * lax.rsqrt(var + LN_EPS) * g + b


def _params(*sem):
    return pltpu.CompilerParams(dimension_semantics=sem, vmem_limit_bytes=VMEM_LIMIT_BYTES)


def _const_spec(shape):
    nd = len(shape)
    return pl.BlockSpec(shape, lambda *_: (0,) * nd)


def _inproj_kernel(*refs, first, seq):
    if first:
        (x_ref, st_ref, watt_ref, wrw_ref, mu_ref, wda_ref, g2_ref, vec_ref, bd_ref,
         q_ref, ka_ref, va_ref, r_ref, lw_ref, k_ref, v_ref, a_ref, b_ref, g_ref, bonus_ref, carry_ref) = refs
    else:
        (x_ref, st_ref, watt_ref, wrw_ref, mu_ref, wda_ref, g2_ref, vec_ref, bd_ref, vfirst_ref, v2_ref,
         q_ref, ka_ref, va_ref, r_ref, lw_ref, k_ref, v_ref, a_ref, b_ref, g_ref, bonus_ref, carry_ref) = refs
    i = pl.program_id(0)
    x = x_ref[...]
    tm = x.shape[0]
    qkv = _dg(x.astype(BF16), watt_ref[...], 1, 0)
    q_ref[...] = qkv[:, :ATT_WIDTH]
    ka_ref[...] = qkv[:, ATT_WIDTH:ATT_WIDTH + KV_WIDTH]
    va_ref[...] = qkv[:, ATT_WIDTH + KV_WIDTH:]

    whole_tiles = seq >= tm
    st = jnp.broadcast_to(st_ref[0], (2 * SUBLANES, D_MODEL)) if whole_tiles else st_ref[...]
    pe = _dg(jnp.concatenate([x, st], axis=0).astype(BF16), wrw_ref[...], 1, 0)
    pc = pe[:tm]
    pst = pe[tm:]
    rowid = lax.broadcasted_iota(jnp.int32, pc.shape, 0)
    pp = pltpu.roll(pc, 1, 0)
    if whole_tiles:
        @pl.when(i == 0)
        def _init():
            carry_ref[...] = jnp.zeros_like(carry_ref)

        prev0 = jnp.where(i % (seq // tm) == 0, pst[0:1], carry_ref[SUBLANES - 1:SUBLANES, :])
        pp = jnp.where(rowid == 0, prev0, pp)
        carry_ref[...] = pc[tm - SUBLANES:]
    else:
        n_st = tm // seq
        er = lax.broadcasted_iota(jnp.int32, (tm, n_st), 0)
        ec = lax.broadcasted_iota(jnp.int32, (tm, n_st), 1)
        expand = (er == ec * seq).astype(BF16)
        pp = jnp.where(rowid % seq == 0, _dot_exact_lhs(expand, pst), pp)
    rw = pc + (pp - pc) * mu_ref[...]
    W = RWKV_WIDTH
    r = rw[:, 0:W]
    k = rw[:, W:2 * W]
    v = rw[:, 2 * W:3 * W]
    gl = rw[:, 3 * W:3 * W + D_GATE_LORA]
    wa = rw[:, 3 * W + D_GATE_LORA:3 * W + 2 * LANES]
    w0 = vec_ref[0:1, :]
    a0 = vec_ref[1:2, :]
    k_k = vec_ref[2:3, :]
    k_a = vec_ref[3:4, :]
    r_k = vec_ref[4:5, :]

    lane = lax.broadcasted_iota(jnp.int32, wa.shape, 1)
    wa_t = jnp.where(lane < D_DECAY_LORA, jnp.tanh(wa), wa)
    da = _bdot(wa_t, wda_ref[...])
    lw = -DECAY_SCALE * jax.nn.sigmoid(w0 + da[:, :W])
    a = jax.nn.sigmoid(a0 + da[:, W:])
    g = _bdot(jax.nn.sigmoid(gl), g2_ref[...])
    if not first:
        mv = rw[:, 3 * W + 2 * LANES:]
        v0 = vec_ref[5:6, :]
        v = v + (vfirst_ref[...] - v) * jax.nn.sigmoid(v0 + _bdot(mv, v2_ref[...]))
    bd = bd_ref[...]
    kk = k * k_k
    ssq = _dot_exact_rhs(kk * kk, bd)
    kk = kk / jnp.maximum(jnp.sqrt(ssq), 1e-12)
    k = k * (1.0 + (a - 1.0) * k_a)
    bonus = _dot_exact_rhs(r * k * r_k, bd) * v

    r_ref[...] = r
    lw_ref[...] = lw
    k_ref[...] = k
    v_ref[...] = v
    a_ref[...] = -kk
    b_ref[...] = kk * a
    g_ref[...] = g
    bonus_ref[...] = bonus


def _inproj(x, shift_state, wts, v_first, seq):
    n = x.shape[0]
    first = v_first is None
    tm = ROW_TILE
    row = lambda w: pl.BlockSpec((tm, w), lambda i: (i, 0))
    if seq >= tm:
        assert seq % tm == 0
        st = shift_state.reshape(-1, 1, D_MODEL)
        st_spec = pl.BlockSpec((1, 1, D_MODEL), lambda i: (i // (seq // tm), 0, 0))
    else:
        assert tm % seq == 0
        st = shift_state
        st_spec = pl.BlockSpec((tm // seq, D_MODEL), lambda i: (i, 0))
    ins = [x, st, wts["watt"], wts["wrw"], wts["mu"], wts["wda"], wts["g2"], wts["vecA"], wts["bd"]]
    in_specs = [row(D_MODEL), st_spec] + [_const_spec(a.shape) for a in ins[2:]]
    if not first:
        ins += [v_first, wts["v2"]]
        in_specs += [row(RWKV_WIDTH), _const_spec(wts["v2"].shape)]
    widths = [ATT_WIDTH, KV_WIDTH, KV_WIDTH] + [RWKV_WIDTH] * 8
    return pl.pallas_call(
        functools.partial(_inproj_kernel, first=first, seq=seq),
        grid=(n // tm,),
        in_specs=in_specs,
        out_specs=[row(w) for w in widths],
        out_shape=[jax.ShapeDtypeStruct((n, w), F32) for w in widths],
        scratch_shapes=[pltpu.VMEM((SUBLANES, wts["wrw"].shape[1]), F32)],
        compiler_params=_params("arbitrary"),
        name="inproj",
    )(*ins)


def _dot3s(a_sp, b_sp, ca=1, cb=0):
    (ah, al), (bh, bl) = a_sp, b_sp
    return _dg(ah, bh, ca, cb) + (_dg(ah, bl, ca, cb) + _dg(al, bh, ca, cb))


def _dot1s(a_sp, b_sp, ca=1, cb=0):
    return _dg(a_sp[0], b_sp[0], ca, cb)


def _wkv_kernel(*refs, group, seq_chunk, has_state):
    if has_state:
        r_ref, lw_ref, k_ref, v_ref, a_ref, b_ref, s0_ref, y_ref, sout_ref, s_scr = refs
    else:
        r_ref, lw_ref, k_ref, v_ref, a_ref, b_ref, y_ref, sout_ref, s_scr = refs
    G, Ls = group, seq_chunk
    L = G * Ls
    H, N = N_RWKV_HEADS, HEAD_DIM
    c_idx = pl.program_id(1)

    @pl.when(c_idx == 0)
    def _init():
        if has_state:
            s_scr[...] = s0_ref[...]
        else:
            s_scr[...] = jnp.zeros_like(s_scr)

    row = lax.broadcasted_iota(jnp.int32, (L, L), 0)
    col = lax.broadcasted_iota(jnp.int32, (L, L), 1)
    same = (row // Ls) == (col // Ls)
    incl = same & (row >= col)
    strict = same & (row > col)
    eye = (row == col).astype(F32)
    row2 = lax.broadcasted_iota(jnp.int32, (2 * L, 2 * L), 0)
    col2 = lax.broadcasted_iota(jnp.int32, (2 * L, 2 * L), 1)
    t_q = row2 % L
    t_k = col2 % L
    mask2 = ((t_q // Ls) == (t_k // Ls)) & ((t_q > t_k) | ((row2 >= L) & (t_q == t_k)))

    lw_all = lw_ref[...]
    cum_all = _dot_exact_lhs(incl.astype(BF16), lw_all)
    if G == 1:
        tot_all = jnp.broadcast_to(cum_all[L - 1:L, :], cum_all.shape)
    else:
        tot_all = _dot_exact_lhs(same.astype(BF16), lw_all)
    e_end = jnp.exp(tot_all - cum_all)
    etot_all = jnp.exp(tot_all)
    a_all = a_ref[...]
    r_all = r_ref[...]
    b_all = b_ref[...]
    k_all = k_ref[...]
    v_all = v_ref[...]
    ar_all = jnp.concatenate([a_all * jnp.exp(cum_all - lw_all), r_all * jnp.exp(cum_all)], axis=0)
    ar_sp = _split2(ar_all)
    if G == 1:
        mid = cum_all[L // 2 - 1:L // 2, :]
        e_neg = jnp.exp(mid - cum_all)
        arc_sp = _split2(jnp.concatenate([a_all * jnp.exp(cum_all - lw_all - mid), r_all * jnp.exp(cum_all - mid)],
                                         axis=0))
    else:
        e_neg = jnp.exp(-cum_all)
        arc_sp = ar_sp
    bk_sp = _split2(jnp.concatenate([b_all * e_neg, k_all * e_neg], axis=0))
    bkh_sp = _split2(jnp.concatenate([b_all * e_end, k_all * e_end], axis=0))
    heads = range(H)
    hsl = lambda t, h: t[:, h * N:(h + 1) * N]
    hsp = lambda sp, h: (hsl(sp[0], h), hsl(sp[1], h))

    m = [jnp.where(mask2, _dg(hsl(arc_sp[0], h), hsl(bk_sp[0], h), 1, 1), 0.0) for h in heads]
    m_sp = [_split2(t) for t in m]
    a_ab = [t[:L, :L] for t in m]
    d = [eye + jnp.where((row // 2) == (col // 2), t, 0.0) for t in a_ab]
    n = 4
    while n <= Ls:
        off = ((row // n) == (col // n)) & ((row // (n // 2)) != (col // (n // 2)))
        d_b = [t.astype(BF16) for t in d]
        dn = [_dg(t, jnp.where(off, s, 0.0).astype(BF16), 1, 0) for t, s in zip(d_b, a_ab)]
        d = [t + _dg(p.astype(BF16), tb, 1, 0) for t, p, tb in zip(d, dn, d_b)]
        n *= 2
    resid = [eye - t + _dot3s((s[0][:L, :L], s[1][:L, :L]), _split2(t)) for t, s in zip(d, m_sp)]
    t_inv = [t + _dg(t.astype(BF16), r.astype(BF16), 1, 0) for t, r in zip(d, resid)]

    s0 = [[s_scr[g, h] for h in heads] for g in range(G)]
    s0_sp = [[_split2(s0[g][h]) for h in heads] for g in range(G)]

    def seq_rows(sp, g, h):
        if G == 1:
            return hsp(sp, h)
        return tuple(jnp.concatenate([hsl(p, h)[g * Ls:(g + 1) * Ls], hsl(p, h)[L + g * Ls:L + (g + 1) * Ls]], axis=0)
                     for p in sp)

    st = [[_dot1s(seq_rows(ar_sp, g, h), s0_sp[g][h], 1, 1) for h in heads] for g in range(G)]
    rhs_st = [jnp.concatenate([st[g][h][:Ls] for g in range(G)], axis=0) for h in heads]
    y_st = [jnp.concatenate([st[g][h][Ls:] for g in range(G)], axis=0) for h in heads]

    v_h = [hsl(v_all, h) for h in heads]
    zeros = jnp.zeros((L, N), F32)
    m_top = [(s[0][:L], s[1][:L]) for s in m_sp]
    m_bot = [(s[0][L:], s[1][L:]) for s in m_sp]
    rhs = [rhs_st[h] + _dot1s(m_top[h], _split2(jnp.concatenate([zeros, v_h[h]], axis=0))) for h in heads]
    u = [_dot1s(_split2(t_inv[h]), _split2(rhs[h])) for h in heads]
    uv_sp = [_split2(jnp.concatenate([u[h], v_h[h]], axis=0)) for h in heads]
    y = [y_st[h] + _dg(m_bot[h][0], uv_sp[h][0], 1, 0) for h in heads]
    for g in range(G):
        for h in heads:
            if G == 1:
                uv_g = uv_sp[h]
            else:
                uv_g = tuple(jnp.concatenate([p[g * Ls:(g + 1) * Ls], p[L + g * Ls:L + (g + 1) * Ls]], axis=0)
                             for p in uv_sp[h])
            upd = _dot1s(uv_g, seq_rows(bkh_sp, g, h), 0, 0)
            s_scr[g, h] = s0[g][h] * hsl(etot_all, h)[g * Ls:g * Ls + 1] + upd
    ys = []
    for h in heads:
        mu = jnp.mean(y[h], axis=-1, keepdims=True)
        var = jnp.mean(jnp.square(y[h] - mu), axis=-1, keepdims=True)
        ys.append((y[h] - mu) * lax.rsqrt(var + GN_EPS))
    y_ref[...] = jnp.concatenate(ys, axis=1)

    @pl.when(c_idx == pl.num_programs(1) - 1)
    def _fin():
        sout_ref[...] = s_scr[...]


def _wkv_scan(r, lw, k, v, a, b, state, n_seq, seq, seq_chunk, group):
    n_chunks = seq // seq_chunk
    assert group == 1 or n_chunks == 1
    has_state = state is not None
    rows = group * seq_chunk
    tok_spec = pl.BlockSpec((rows, RWKV_WIDTH), lambda i, c: (i * n_chunks + c, 0))
    st_spec = pl.BlockSpec((group, N_RWKV_HEADS, HEAD_DIM, HEAD_DIM), lambda i, c: (i, 0, 0, 0))
    in_specs = [tok_spec] * 6
    args = (r, lw, k, v, a, b)
    if has_state:
        states, layer = state
        in_specs.append(pl.BlockSpec((None, group, N_RWKV_HEADS, HEAD_DIM, HEAD_DIM),
                                     lambda i, c: (layer, i, 0, 0, 0)))
        args += (states,)
    return pl.pallas_call(
        functools.partial(_wkv_kernel, group=group, seq_chunk=seq_chunk, has_state=has_state),
        grid=(n_seq // group, n_chunks),
        in_specs=in_specs,
        out_specs=[tok_spec, st_spec],
        out_shape=[jax.ShapeDtypeStruct((n_seq * seq, RWKV_WIDTH), F32),
                   jax.ShapeDtypeStruct((n_seq, N_RWKV_HEADS, HEAD_DIM, HEAD_DIM), F32)],
        scratch_shapes=[pltpu.VMEM((group, N_RWKV_HEADS, HEAD_DIM, HEAD_DIM), F32)],
        compiler_params=_params("arbitrary", "arbitrary"),
        name="wkv_scan",
    )(*args)


def _sink_softmax(s, sink):
    m = sink
    for t in s:
        m = jnp.maximum(m, jnp.max(t, axis=-1, keepdims=True))
    es = [jnp.exp(t - m) for t in s]
    den = jnp.exp(sink - m)
    for e in es:
        den = den + jnp.sum(e, axis=-1, keepdims=True)
    return [e / den for e in es]


def _attn_prompt_kernel(q_ref, kc_ref, kp_ref, vc_ref, vp_ref, sink_ref, o_ref):
    L = WINDOW
    n = pl.program_id(1)
    q = q_ref[...].astype(BF16)
    kc = kc_ref[...].astype(BF16)
    vc = vc_ref[...].astype(BF16)
    kws = [jnp.concatenate([kp_ref[...].astype(BF16), kc[:L]], axis=0), kc]
    vws = [jnp.concatenate([vp_ref[...].astype(BF16), vc[:L]], axis=0), vc]
    qi = lax.broadcasted_iota(jnp.int32, (L, 2 * L), 0)
    kj = lax.broadcasted_iota(jnp.int32, (L, 2 * L), 1)
    diff = qi + L - kj
    band = (diff >= 0) & (diff < WINDOW)
    masks = [band & ((kj >= L) | (n > 0)), band]
    lane = lax.broadcasted_iota(jnp.int32, vc.shape, 1)
    one = jnp.ones_like(vc)
    vexts = [[jnp.where(lane < HEAD_DIM, vw, one), jnp.where(lane >= HEAD_DIM, vw, one)] for vw in vws]
    hsl = lambda t, h: t[:, h * HEAD_DIM:(h + 1) * HEAD_DIM]
    items = [(j, h) for j in range(2) for h in range(N_Q_HEADS)]
    sinks = [sink_ref[h:h + 1, 0:1] for h in range(N_Q_HEADS)]
    ss = [jnp.where(masks[j], _dg(hsl(q[j * L:(j + 1) * L], h), hsl(kws[j], h // Q_PER_KV), 1, 1) * ATT_SCALE, -jnp.inf)
          for j, h in items]
    ms = [jnp.maximum(jnp.max(ss[i], axis=-1, keepdims=True), sinks[h]) for i, (j, h) in enumerate(items)]
    es = [jnp.exp(ss[i] - ms[i]).astype(BF16) for i in range(len(items))]
    oes = [_dg(es[i], vexts[j][h // Q_PER_KV], 1, 0) for i, (j, h) in enumerate(items)]
    outs = []
    for i, (j, h) in enumerate(items):
        hk = h // Q_PER_KV
        rs = oes[i][:, (1 - hk) * HEAD_DIM:(1 - hk) * HEAD_DIM + 1]
        outs.append(hsl(oes[i], hk) / (rs + jnp.exp(sinks[h] - ms[i])))
    o_ref[...] = jnp.concatenate([jnp.concatenate(outs[j * N_Q_HEADS:(j + 1) * N_Q_HEADS], axis=1) for j in range(2)],
                                 axis=0)


def _attn_prompt(q, k, v, sink_rows, n_batch, seq):
    nb = seq // WINDOW
    assert nb % 2 == 0
    npair = nb // 2
    cur = lambda w: pl.BlockSpec((2 * WINDOW, w), lambda b, n: (b * npair + n, 0))
    prev = lambda w: pl.BlockSpec((WINDOW, w), lambda b, n: (b * nb + jnp.maximum(2 * n - 1, 0), 0))
    return pl.pallas_call(
        _attn_prompt_kernel,
        grid=(n_batch, npair),
        in_specs=[cur(ATT_WIDTH), cur(KV_WIDTH), prev(KV_WIDTH), cur(KV_WIDTH), prev(KV_WIDTH),
                  _const_spec(sink_rows.shape)],
        out_specs=cur(ATT_WIDTH),
        out_shape=jax.ShapeDtypeStruct((n_batch * seq, ATT_WIDTH), F32),
        compiler_params=_params("arbitrary", "arbitrary"),
        name="attn_prompt",
    )(q, k, k, v, v, sink_rows)


def _attn_sample_kernel(q_ref, kn_ref, vn_ref, ck_ref, cv_ref, sink_ref, o_ref, *, seq):
    S = seq
    W = WINDOW
    B = SAMPLE_ATT_BATCH
    R = Q_PER_KV * S
    tq1 = lax.broadcasted_iota(jnp.int32, (R, W), 0) % S
    kj1 = lax.broadcasted_iota(jnp.int32, (R, W), 1)
    mask_cache = kj1 > tq1
    tq2 = lax.broadcasted_iota(jnp.int32, (R, S), 0) % S
    kj2 = lax.broadcasted_iota(jnp.int32, (R, S), 1)
    mask_new = kj2 <= tq2
    kvs = range(N_KV_HEADS)
    hsl = lambda t, h: t[:, h * HEAD_DIM:(h + 1) * HEAD_DIM]
    group = lambda hk: range(hk * Q_PER_KV, (hk + 1) * Q_PER_KV)
    seq_rows = lambda t, b: t[b * S:(b + 1) * S]
    q = q_ref[...]
    kn = kn_ref[...]
    vn = vn_ref[...]
    qh = [hsl(q, h) for h in range(N_Q_HEADS)]
    knh = [hsl(kn, hk) for hk in kvs]
    vnh = [hsl(vn, hk) for hk in kvs]
    sinks = [jnp.concatenate([jnp.broadcast_to(sink_ref[h:h + 1, 0:1], (S, 1)) for h in group(hk)], axis=0)
             for hk in kvs]
    items = [(b, hk) for b in range(B) for hk in kvs]
    qs = [jnp.concatenate([seq_rows(qh[h], b) for h in group(hk)], axis=0).astype(BF16) for b, hk in items]
    ck = [ck_ref[b].astype(BF16) for b in range(B)]
    cv = [cv_ref[b].astype(BF16) for b in range(B)]
    s1 = [jnp.where(mask_cache, _dg(qs[i], hsl(ck[b], hk), 1, 1) * ATT_SCALE, -jnp.inf)
          for i, (b, hk) in enumerate(items)]
    s2 = [jnp.where(mask_new, _dg(qs[i], seq_rows(knh[hk], b).astype(BF16), 1, 1) * ATT_SCALE, -jnp.inf)
          for i, (b, hk) in enumerate(items)]
    ps = [_sink_softmax([s1[i], s2[i]], sinks[hk]) for i, (b, hk) in enumerate(items)]
    outs = [_dg(ps[i][0].astype(BF16), hsl(cv[b], hk), 1, 0)
            + _dg(ps[i][1].astype(BF16), seq_rows(vnh[hk], b).astype(BF16), 1, 0)
            for i, (b, hk) in enumerate(items)]
    rows = [jnp.concatenate([outs[b * N_KV_HEADS + hk][g * S:(g + 1) * S] for hk in kvs for g in range(Q_PER_KV)],
                            axis=1) for b in range(B)]
    o_ref[...] = jnp.concatenate(rows, axis=0)


def _attn_sample(q, k, v, cache_k, cache_v, sink_rows, n_batch, seq):
    bb = SAMPLE_ATT_BATCH
    tok = lambda w: pl.BlockSpec((bb * seq, w), lambda i: (i, 0))
    cache = pl.BlockSpec((bb, WINDOW, KV_WIDTH), lambda i: (i, 0, 0))
    return pl.pallas_call(
        functools.partial(_attn_sample_kernel, seq=seq),
        grid=(n_batch // bb,),
        in_specs=[tok(ATT_WIDTH), tok(KV_WIDTH), tok(KV_WIDTH), cache, cache, _const_spec(sink_rows.shape)],
        out_specs=tok(ATT_WIDTH),
        out_shape=jax.ShapeDtypeStruct((n_batch * seq, ATT_WIDTH), F32),
        compiler_params=_params("arbitrary"),
        name="attn_sample",
    )(q, k, v, cache_k, cache_v, sink_rows)


def _second_max4(a, b, c, d):
    return jnp.maximum(jnp.maximum(jnp.minimum(a, b), jnp.minimum(c, d)),
                       jnp.minimum(jnp.maximum(a, b), jnp.maximum(c, d)))


def _route(logits_t, bias_col):
    G, E = N_EXPERT_GROUPS, EXPERTS_PER_GROUP
    m = jnp.max(logits_t, axis=0, keepdims=True)
    ex = jnp.exp(logits_t - m)
    probs = ex / jnp.sum(ex, axis=0, keepdims=True)
    sel = probs + bias_col
    p = [probs[e:e + 1, :] for e in range(N_EXPERTS)]
    s = [sel[e:e + 1, :] for e in range(N_EXPERTS)]
    gs = []
    for g in range(G):
        a, b, c, d = s[E * g:E * g + E]
        top1 = jnp.maximum(jnp.maximum(a, b), jnp.maximum(c, d))
        gs.append(top1 + _second_max4(a, b, c, d))
    best = jnp.zeros_like(gs[0], dtype=jnp.int32)
    best_s = gs[0]
    for g in range(1, G):
        upd = gs[g] > best_s
        best = jnp.where(upd, g, best)
        best_s = jnp.where(upd, gs[g], best_s)

    def pick(vals, j):
        out = vals[j]
        for g in range(1, G):
            out = jnp.where(best == g, vals[E * g + j], out)
        return out

    ig = [pick(s, j) for j in range(E)]
    pg = [pick(p, j) for j in range(E)]
    l1 = jnp.zeros_like(best)
    v1 = ig[0]
    for j in range(1, E):
        upd = ig[j] > v1
        l1 = jnp.where(upd, j, l1)
        v1 = jnp.where(upd, ig[j], v1)
    l2 = jnp.full_like(best, -1)
    v2 = jnp.full_like(v1, -jnp.inf)
    for j in range(E):
        upd = (l1 != j) & (ig[j] > v2)
        l2 = jnp.where(upd, j, l2)
        v2 = jnp.where(upd, ig[j], v2)
    zero = jnp.zeros_like(v1)
    w1 = zero
    w2 = zero
    for j in range(E):
        w1 = jnp.where(l1 == j, pg[j], w1)
        w2 = jnp.where(l2 == j, pg[j], w2)
    wsum = w1 + w2
    w1 = w1 / wsum
    w2 = w2 / wsum
    e1 = (best * E + l1).astype(F32)
    e2 = (best * E + l2).astype(F32)
    return jnp.concatenate([e1, e2, w1, w2, zero, zero, zero, zero], axis=0)


def _post_kernel(yn_ref, bonus_ref, g_ref, att_ref, x_ref, woa_ref, wor_ref, gn_ref, ln_ref,
                 wrt_ref, rb_ref, x1_ref, gates_ref):
    rw_out = (yn_ref[...] * gn_ref[0:1, :] + gn_ref[1:2, :] + bonus_ref[...]) * g_ref[...]
    mixed = _bdot(att_ref[...], woa_ref[...]) + _bdot(rw_out, wor_ref[...])
    x1 = _layer_norm(ALPHA * x_ref[...] + mixed, ln_ref[0:1, :], ln_ref[1:2, :])
    x1_ref[...] = x1
    logits_t = _dot3(wrt_ref[...], x1, 1, 1)
    route_t = _route(logits_t, rb_ref[:, 0:1])
    pad = jnp.zeros((LANES - route_t.shape[0], route_t.shape[1]), F32)
    gates_ref[...] = jnp.concatenate([route_t, pad], axis=0).T


def _post(yn, bonus, g, att, x, wts, glob):
    n = x.shape[0]
    tm = ROW_TILE
    row = lambda w: pl.BlockSpec((tm, w), lambda i: (i, 0))
    consts = [wts["woa"], wts["wor"], wts["gn"], wts["ln1"], glob["wrt"], glob["rb"]]
    return pl.pallas_call(
        _post_kernel,
        grid=(n // tm,),
        in_specs=[row(RWKV_WIDTH)] * 3 + [row(ATT_WIDTH), row(D_MODEL)] + [_const_spec(a.shape) for a in consts],
        out_specs=[row(D_MODEL), row(LANES)],
        out_shape=[jax.ShapeDtypeStruct((n, D_MODEL), F32), jax.ShapeDtypeStruct((n, LANES), F32)],
        compiler_params=_params("arbitrary"),
        name="post_mix",
    )(yn, bonus, g, att, x, *consts)


def _moe_positions_kernel(gates_ref, pos_ref, cnt_ref, cnt_scr, offs_scr, carry_scr):
    ph = pl.program_id(0)
    i = pl.program_id(1)
    g = gates_ref[...]
    tm = g.shape[0]
    lane = lax.broadcasted_iota(jnp.int32, g.shape, 1).astype(F32)
    oh1 = (lane == g[:, 0:1]).astype(F32)
    oh2 = (lane == g[:, 1:2]).astype(F32)
    oh = oh1 + oh2

    @pl.when((ph == 0) & (i == 0))
    def _zero():
        cnt_scr[...] = jnp.zeros_like(cnt_scr)

    @pl.when(ph == 0)
    def _count():
        cnt_scr[...] += jnp.sum(oh, axis=0, keepdims=True)
        pos_ref[...] = jnp.zeros_like(pos_ref)

    @pl.when((ph == 1) & (i == 0))
    def _offsets():
        cnt = cnt_scr[...]
        padded = jnp.floor((cnt + (MOE_BLOCK - 1)) * (1.0 / MOE_BLOCK)) * MOE_BLOCK
        r = lax.broadcasted_iota(jnp.int32, (LANES, LANES), 0)
        c = lax.broadcasted_iota(jnp.int32, (LANES, LANES), 1)
        offs_scr[...] = _dot_exact_rhs(padded, (r < c).astype(BF16))
        carry_scr[...] = jnp.zeros_like(carry_scr)
        cnt_ref[...] = cnt

    @pl.when(ph == 1)
    def _rank():
        r = lax.broadcasted_iota(jnp.int32, (tm, tm), 0)
        c = lax.broadcasted_iota(jnp.int32, (tm, tm), 1)
        before = _dg((r > c).astype(BF16), oh.astype(BF16), 1, 0)
        base = before + carry_scr[0:1, :] + offs_scr[0:1, :]
        p1 = jnp.sum(oh1 * base, axis=1, keepdims=True)
        p2 = jnp.sum(oh2 * base, axis=1, keepdims=True)
        pos_ref[...] = jnp.where(lane == 0.0, p1, jnp.where(lane == 1.0, p2, 0.0))
        carry_scr[...] += jnp.sum(oh, axis=0, keepdims=True)


def _moe_positions(gates):
    n = gates.shape[0]
    tm = min(MOE_POS_TILE, n)
    stat = pltpu.VMEM((SUBLANES, LANES), F32)
    return pl.pallas_call(
        _moe_positions_kernel,
        grid=(2, n // tm),
        in_specs=[pl.BlockSpec((tm, LANES), lambda ph, i: (i, 0))],
        out_specs=[pl.BlockSpec((tm, LANES), lambda ph, i: (i * ph, 0)), _const_spec((SUBLANES, LANES))],
        out_shape=[jax.ShapeDtypeStruct((n, LANES), F32), jax.ShapeDtypeStruct((SUBLANES, LANES), F32)],
        scratch_shapes=[stat, stat, stat],
        compiler_params=_params("arbitrary", "arbitrary"),
        name="moe_positions",
    )(gates)


def _moe_dispatch_kernel(ends_ref, pos_ref, x_ref, xs_ref, zero_scr, sem):
    tm = x_ref.shape[0]
    R = zero_scr.shape[0]

    @pl.when(pl.program_id(0) == 0)
    def _zero_tail_blocks():
        zero_scr[...] = jnp.zeros_like(zero_scr)

        def tail_copy(e):
            return pltpu.make_async_copy(zero_scr, xs_ref.at[pl.ds((ends_ref[e] - 1) * R, R), :], sem)

        def non_empty(e):
            return ends_ref[e] > (ends_ref[e - 1] if e else 0)

        n_blocks = xs_ref.shape[0] // R
        n_used = ends_ref[N_EXPERTS - 1]

        def spare_copy(j):
            return pltpu.make_async_copy(zero_scr, xs_ref.at[pl.ds((n_used + j) * R, R), :], sem)

        for e in range(N_EXPERTS):
            pl.when(non_empty(e))(lambda e=e: tail_copy(e).start())
            pl.when(n_used + e < n_blocks)(lambda e=e: spare_copy(e).start())
        for e in range(N_EXPERTS):
            pl.when(non_empty(e))(lambda e=e: tail_copy(e).wait())
            pl.when(n_used + e < n_blocks)(lambda e=e: spare_copy(e).wait())

    def row_copy(r, k):
        return pltpu.make_async_copy(x_ref.at[pl.ds(r, 1), :], xs_ref.at[pl.ds(pos_ref[k, r], 1), :], sem)

    def issue(r, carry):
        row_copy(r, 0).start(priority=0)
        row_copy(r, 1).start(priority=1)
        return carry

    lax.fori_loop(0, tm, issue, 0, unroll=DMA_ISSUE_UNROLL)
    for _ in range(N_EXPERTS_PER_TOKEN):
        pltpu.make_async_copy(x_ref, xs_ref.at[pl.ds(0, tm), :], sem).wait()


def _moe_dispatch(x1, pos, ends, n_slots):
    n = x1.shape[0]
    tm = MOE_DMA_TILE
    return pl.pallas_call(
        _moe_dispatch_kernel,
        grid=(n // tm,),
        in_specs=[pl.BlockSpec(memory_space=pltpu.SMEM),
                  pl.BlockSpec((2, tm), lambda i: (0, i), memory_space=pltpu.SMEM),
                  pl.BlockSpec((tm, D_MODEL), lambda i: (i, 0))],
        out_specs=pl.BlockSpec(memory_space=pl.ANY),
        out_shape=jax.ShapeDtypeStruct((n_slots, D_MODEL), F32),
        scratch_shapes=[pltpu.VMEM((MOE_BLOCK, D_MODEL), F32), pltpu.SemaphoreType.DMA],
        compiler_params=_params("arbitrary"),
        name="moe_dispatch",
    )(ends, pos, x1)


def _moe_experts_kernel(blk_expert_ref, n_used_ref, xs_ref, wg_ref, wu_ref, wd_ref, ys_ref, wg_b, wu_b, wd_b):
    b = pl.program_id(0)
    used = b < n_used_ref[0]
    new_expert = (b == 0) | (blk_expert_ref[b] != blk_expert_ref[jnp.maximum(b - 1, 0)])

    @pl.when(used & new_expert)
    def _cast_weights():
        wg_b[...] = wg_ref[0, 0].astype(BF16)
        wu_b[...] = wu_ref[0, 0].astype(BF16)
        wd_b[...] = wd_ref[0, 0].astype(BF16)

    @pl.when(used)
    def _compute():
        xb = xs_ref[...].astype(BF16)
        h = jax.nn.silu(_dg(xb, wg_b[...], 1, 0)) * _dg(xb, wu_b[...], 1, 0)
        ys_ref[...] = _dg(h.astype(BF16), wd_b[...], 1, 0)

    @pl.when(b >= n_used_ref[0])
    def _skip():
        ys_ref[...] = jnp.zeros_like(ys_ref)


def _moe_experts(xs, blk_expert, n_used, glob, l):
    n_slots = xs.shape[0]
    R = MOE_BLOCK
    last = lambda b, nu: jnp.minimum(b, nu[0] - 1)
    wspec = lambda s: pl.BlockSpec((1, 1) + s, lambda b, be, nu: (l, be[last(b, nu)], 0, 0))
    return pl.pallas_call(
        _moe_experts_kernel,
        grid_spec=pltpu.PrefetchScalarGridSpec(
            num_scalar_prefetch=2,
            grid=(n_slots // R,),
            in_specs=[pl.BlockSpec((R, D_MODEL), lambda b, be, nu: (last(b, nu), 0)),
                      wspec((D_MODEL, EXPERT_FF)), wspec((D_MODEL, EXPERT_FF)), wspec((EXPERT_FF, D_MODEL))],
            out_specs=pl.BlockSpec((R, D_MODEL), lambda b, be, nu: (b, 0)),
            scratch_shapes=[pltpu.VMEM((D_MODEL, EXPERT_FF), BF16), pltpu.VMEM((D_MODEL, EXPERT_FF), BF16),
                            pltpu.VMEM((EXPERT_FF, D_MODEL), BF16)],
        ),
        out_shape=jax.ShapeDtypeStruct((n_slots, D_MODEL), F32),
        compiler_params=_params("arbitrary"),
        name="moe_experts",
    )(blk_expert, n_used, xs, glob["wg"], glob["wu"], glob["wd"])


def _moe_combine_kernel(pos_ref, x_ref, gates_ref, ys_ref, ln_ref, o_ref, buf, sem):
    tm = x_ref.shape[0]

    def row_copy(r, k):
        return pltpu.make_async_copy(ys_ref.at[pl.ds(pos_ref[k, r], 1), :], buf.at[k, pl.ds(r, 1), :], sem)

    def issue(r, carry):
        row_copy(r, 0).start(priority=0)
        row_copy(r, 1).start(priority=1)
        return carry

    lax.fori_loop(0, tm, issue, 0, unroll=DMA_ISSUE_UNROLL)
    for k in range(N_EXPERTS_PER_TOKEN):
        pltpu.make_async_copy(ys_ref.at[pl.ds(0, tm), :], buf.at[k], sem).wait()
    g = gates_ref[...]
    ffn = g[:, 2:3] * buf[0] + g[:, 3:4] * buf[1]
    o_ref[...] = _layer_norm(ALPHA * x_ref[...] + ffn, ln_ref[0:1, :], ln_ref[1:2, :])


def _moe_combine(x1, gates, pos, ys, wts):
    n = x1.shape[0]
    tm = MOE_DMA_TILE
    return pl.pallas_call(
        _moe_combine_kernel,
        grid=(n // tm,),
        in_specs=[pl.BlockSpec((2, tm), lambda i: (0, i), memory_space=pltpu.SMEM),
                  pl.BlockSpec((tm, D_MODEL), lambda i: (i, 0)),
                  pl.BlockSpec((tm, LANES), lambda i: (i, 0)),
                  pl.BlockSpec(memory_space=pl.ANY),
                  _const_spec(wts["ln2"].shape)],
        out_specs=pl.BlockSpec((tm, D_MODEL), lambda i: (i, 0)),
        out_shape=jax.ShapeDtypeStruct((n, D_MODEL), F32),
        scratch_shapes=[pltpu.VMEM((2, tm, D_MODEL), F32), pltpu.SemaphoreType.DMA],
        compiler_params=_params("arbitrary"),
        name="moe_combine",
    )(pos, x1, gates, ys, wts["ln2"])


def _moe(x1, gates, wts, glob, l):
    n = x1.shape[0]
    R = MOE_BLOCK
    n_slots = N_EXPERTS_PER_TOKEN * n + N_EXPERTS * R
    pos_f, cnt = _moe_positions(gates)
    pos = pos_f[:, :N_EXPERTS_PER_TOKEN].astype(jnp.int32).T
    blocks = (cnt[0, :N_EXPERTS].astype(jnp.int32) + (R - 1)) // R
    ends = jnp.cumsum(blocks)
    n_used = ends[-1:]
    blk_expert = jnp.sum(jnp.arange(n_slots // R, dtype=jnp.int32)[:, None] >= ends[None, :], axis=1)
    blk_expert = jnp.minimum(blk_expert, N_EXPERTS - 1).astype(jnp.int32)
    xs = _moe_dispatch(x1, pos, ends.astype(jnp.int32), n_slots)
    ys = _moe_experts(xs, blk_expert, n_used, glob, l)
    return _moe_combine(x1, gates, pos, ys, wts)


def _rows8(vectors, width):
    rows = [v.reshape(1, width).astype(F32) for v in vectors]
    rows.append(jnp.zeros((8 - len(rows), width), F32))
    return jnp.concatenate(rows, axis=0)


def _prep_layer(l, p):
    W = RWKV_WIDTH
    w_l = p["w_in"][l]
    rwc = w_l[:, N_ATT_COLS:]
    mu = p["mu_rwkv"][l]
    o_w, o_k, o_v, o_a, o_g = W, W + D_DECAY_LORA, 2 * W + D_DECAY_LORA, 3 * W + D_DECAY_LORA, 3 * W + 2 * D_DECAY_LORA

    def reorder(t):
        parts = [t[..., 0:W], t[..., o_k:o_k + W], t[..., o_v:o_v + W], t[..., o_g:o_g + D_GATE_LORA],
                 t[..., o_w:o_w + D_DECAY_LORA], t[..., o_a:o_a + D_AAA_LORA]]
        return parts

    w_parts = reorder(rwc)
    mu_parts = reorder(mu)
    if l > 0:
        padw = LANES - D_MV_LORA
        w_parts += [p["w_vres_in"][l - 1], jnp.zeros((D_MODEL, padw), F32)]
        mu_parts += [p["mu_vres"][l - 1], jnp.zeros((padw,), F32)]
    wrw = jnp.concatenate(w_parts, axis=1).astype(BF16)
    mu_row = jnp.concatenate(mu_parts).reshape(1, -1)
    zero = jnp.zeros((D_DECAY_LORA, W), F32)
    wda = jnp.concatenate([jnp.concatenate([p["decay_w2"][l], zero], axis=1),
                           jnp.concatenate([zero, p["aaa_a2"][l]], axis=1)], axis=0).astype(BF16)
    vecs = [p["decay_w0"][l], p["aaa_a0"][l], p["k_k"][l], p["k_a"][l], p["r_k"][l].reshape(W)]
    out = {
        "watt": w_l[:, :N_ATT_COLS].astype(BF16),
        "wrw": wrw,
        "mu": mu_row,
        "wda": wda,
        "g2": p["gate_g2"][l].astype(BF16),
        "woa": p["w_o"][l][:ATT_WIDTH].astype(BF16),
        "wor": p["w_o"][l][ATT_WIDTH:].astype(BF16),
        "gn": _rows8([p["gn_g"][l], p["gn_b"][l]], W),
        "ln1": _rows8([p["ln1_g"][l], p["ln1_b"][l]], D_MODEL),
        "ln2": _rows8([p["ln2_g"][l], p["ln2_b"][l]], D_MODEL),
        "sink_rows": jnp.broadcast_to(p["sinks"][l].reshape(N_Q_HEADS, 1), (N_Q_HEADS, LANES)).astype(F32),
    }
    if l > 0:
        vecs.append(p["vres_v0"][l - 1])
        out["v2"] = jnp.concatenate([p["vres_v2"][l - 1], jnp.zeros((LANES - D_MV_LORA, W), F32)], axis=0).astype(BF16)
    out["vecA"] = _rows8(vecs, W)
    hid = jnp.arange(W) // HEAD_DIM
    out["bd"] = (hid[:, None] == hid[None, :]).astype(BF16)
    return out


def _trunk(x3, shift_prev, cache_k, cache_v, wkv_prev, layer_wts, glob):
    decode = cache_k is not None
    n_batch, seq, _ = x3.shape
    x = x3.reshape(n_batch * seq, D_MODEL)
    new_k, new_v, new_wkv, new_shift = [], [], [], []
    v_first = None
    for l in range(DEPTH):
        wts = layer_wts[l]
        new_shift.append(x.reshape(n_batch, seq, D_MODEL)[:, -1])
        q, ka, va, r, lw, k, v, a, b, g, bonus = _inproj(x, shift_prev[l], wts, v_first, seq)
        if l == 0:
            v_first = v
        if decode:
            ck = cache_k[l].reshape(n_batch, WINDOW, KV_WIDTH)
            cv = cache_v[l].reshape(n_batch, WINDOW, KV_WIDTH)
            att = _attn_sample(q, ka, va, ck, cv, wts["sink_rows"], n_batch, seq)
            new_k.append(ka.reshape(n_batch, seq, N_KV_HEADS, HEAD_DIM))
            new_v.append(va.reshape(n_batch, seq, N_KV_HEADS, HEAD_DIM))
            yn, s_out = _wkv_scan(r, lw, k, v, a, b, (wkv_prev[l][None], 0), n_batch, seq, seq, SAMPLE_WKV_GROUP)
        else:
            att = _attn_prompt(q, ka, va, wts["sink_rows"], n_batch, seq)
            last = lambda t: t.reshape(n_batch, seq, KV_WIDTH)[:, -WINDOW:].reshape(n_batch, WINDOW, N_KV_HEADS, HEAD_DIM)
            new_k.append(last(ka))
            new_v.append(last(va))
            yn, s_out = _wkv_scan(r, lw, k, v, a, b, None, n_batch, seq, WKV_CHUNK, 1)
        new_wkv.append(s_out)
        x1, gates = _post(yn, bonus, g, att, x, wts, glob)
        x = _moe(x1, gates, wts, glob, l)
    return (x.reshape(n_batch, seq, D_MODEL), jnp.stack(new_k), jnp.stack(new_v), jnp.stack(new_wkv),
            jnp.stack(new_shift))


def kernel(x_prompt, x_sample, cache_k, cache_v, state_wkv, state_shift, w_in, w_vres_in, mu_rwkv, mu_vres, sinks, decay_w0, decay_w2, aaa_a0, aaa_a2, vres_v0, vres_v2, gate_g2, k_k, k_a, r_k, gn_g, gn_b, w_o, ln1_g, ln1_b, w_router, router_bias, w_gate, w_up, w_down, ln2_g, ln2_b):
    p = dict(w_in=w_in, w_vres_in=w_vres_in, mu_rwkv=mu_rwkv, mu_vres=mu_vres, sinks=sinks,
             decay_w0=decay_w0, decay_w2=decay_w2, aaa_a0=aaa_a0, aaa_a2=aaa_a2,
             vres_v0=vres_v0, vres_v2=vres_v2, gate_g2=gate_g2, k_k=k_k, k_a=k_a, r_k=r_k,
             gn_g=gn_g, gn_b=gn_b, w_o=w_o, ln1_g=ln1_g, ln1_b=ln1_b,
             w_gate=w_gate, w_up=w_up, w_down=w_down, ln2_g=ln2_g, ln2_b=ln2_b)
    layer_wts = [_prep_layer(l, p) for l in range(DEPTH)]
    glob = {
        "wrt": w_router.T.astype(F32),
        "rb": jnp.broadcast_to(router_bias.reshape(N_EXPERTS, 1), (N_EXPERTS, LANES)).astype(F32),
        "wg": w_gate,
        "wu": w_up,
        "wd": w_down,
    }
    b_p = x_prompt.shape[0]
    zero_shift = jnp.zeros((DEPTH, b_p, D_MODEL), x_prompt.dtype)
    y_p, k_p, v_p, wkv_p, shift_p = _trunk(x_prompt, zero_shift, None, None, None, layer_wts, glob)
    y_s, k_s, v_s, wkv_s, shift_s = _trunk(x_sample, state_shift, cache_k, cache_v, state_wkv, layer_wts, glob)
    return (y_p, y_s, k_p, v_p, wkv_p, shift_p, k_s, v_s, wkv_s, shift_s)
```

```python
import functools
import math

import jax
import jax.numpy as jnp
from jax import lax
from jax.experimental import pallas as pl
from jax.experimental.pallas import tpu as pltpu

F32 = jnp.float32
BF16 = jnp.bfloat16

D_MODEL = 1024
DEPTH = 4
HEAD_DIM = 64
ATT_WIDTH = 512
RWKV_WIDTH = 512
N_Q_HEADS = 8
N_KV_HEADS = 2
Q_PER_KV = 4
KV_WIDTH = 128
N_ATT_COLS = ATT_WIDTH + 2 * KV_WIDTH
WINDOW = 128
ATT_SCALE = HEAD_DIM ** -0.5
N_RWKV_HEADS = 8
D_DECAY_LORA = 64
D_AAA_LORA = 64
D_GATE_LORA = 128
D_MV_LORA = 32
DECAY_SCALE = math.exp(-0.5)
GN_EPS = 64e-5
LN_EPS = 1e-5
N_EXPERTS = 16
N_EXPERT_GROUPS = 4
EXPERTS_PER_GROUP = 4
EXPERT_FF = 512
ALPHA = (2 * DEPTH) ** 0.25

LANES = 128
SUBLANES = 8
VMEM_LIMIT_BYTES = 56 * 1024 * 1024
ROW_TILE = 256
MOE_BLOCK = 512
MOE_POS_TILE = 512
MOE_DMA_TILE = 1024
N_EXPERTS_PER_TOKEN = 2
DMA_ISSUE_UNROLL = 16
WKV_CHUNK = 128
ATT_Q_BLOCKS = 4
SAMPLE_ATT_BATCH = 8
SAMPLE_WKV_GROUP = 8


def _dg(a, b, ca, cb):
    return lax.dot_general(a, b, (((ca,), (cb,)), ((), ())), preferred_element_type=F32)


def _bdot(a, b):
    return _dg(a.astype(BF16), b.astype(BF16), 1, 0)


def _split2(x):
    hi = x.astype(BF16)
    lo = (x - hi.astype(F32)).astype(BF16)
    return hi, lo


def _dot3(a, b, ca=1, cb=0):
    ah, al = _split2(a)
    bh, bl = _split2(b)
    return _dg(ah, bh, ca, cb) + (_dg(ah, bl, ca, cb) + _dg(al, bh, ca, cb))


def _dot_exact_lhs(m_bf16, x, parts=3):
    acc = None
    rem = x
    for _ in range(parts):
        p = rem.astype(BF16)
        t = _dg(m_bf16, p, 1, 0)
        acc = t if acc is None else acc + t
        rem = rem - p.astype(F32)
    return acc


def _dot_exact_rhs(x, m_bf16, parts=3):
    acc = None
    rem = x
    for _ in range(parts):
        p = rem.astype(BF16)
        t = _dg(p, m_bf16, 1, 0)
        acc = t if acc is None else acc + t
        rem = rem - p.astype(F32)
    return acc


def _layer_norm(z, g, b):
    mu = jnp.mean(z, axis=-1, keepdims=True)
    var = jnp.mean(jnp.square(z - mu), axis=-1, keepdims=True)
    return (z - mu) * lax.rsqrt(var + LN_EPS) * g + b


def _params(*sem):
    return pltpu.CompilerParams(dimension_semantics=sem, vmem_limit_bytes=VMEM_LIMIT_BYTES)


def _const_spec(shape):
    nd = len(shape)
    return pl.BlockSpec(shape, lambda *_: (0,) * nd)


def _inproj_kernel(*refs, first, seq):
    if first:
        (x_ref, st_ref, watt_ref, wrw_ref, mu_ref, wda_ref, g2_ref, vec_ref, bd_ref,
         q_ref, ka_ref, va_ref, r_ref, lw_ref, k_ref, v_ref, a_ref, b_ref, g_ref, bonus_ref, carry_ref) = refs
    else:
        (x_ref, st_ref, watt_ref, wrw_ref, mu_ref, wda_ref, g2_ref, vec_ref, bd_ref, vfirst_ref, v2_ref,
         q_ref, ka_ref, va_ref, r_ref, lw_ref, k_ref, v_ref, a_ref, b_ref, g_ref, bonus_ref, carry_ref) = refs
    i = pl.program_id(0)
    x = x_ref[...]
    tm = x.shape[0]
    qkv = _dg(x.astype(BF16), watt_ref[...], 1, 0)
    q_ref[...] = qkv[:, :ATT_WIDTH]
    ka_ref[...] = qkv[:, ATT_WIDTH:ATT_WIDTH + KV_WIDTH]
    va_ref[...] = qkv[:, ATT_WIDTH + KV_WIDTH:]

    whole_tiles = seq >= tm
    st = jnp.broadcast_to(st_ref[0], (2 * SUBLANES, D_MODEL)) if whole_tiles else st_ref[...]
    pe = _dg(jnp.concatenate([x, st], axis=0).astype(BF16), wrw_ref[...], 1, 0)
    pc = pe[:tm]
    pst = pe[tm:]
    rowid = lax.broadcasted_iota(jnp.int32, pc.shape, 0)
    pp = pltpu.roll(pc, 1, 0)
    if whole_tiles:
        @pl.when(i == 0)
        def _init():
            carry_ref[...] = jnp.zeros_like(carry_ref)

        prev0 = jnp.where(i % (seq // tm) == 0, pst[0:1], carry_ref[SUBLANES - 1:SUBLANES, :])
        pp = jnp.where(rowid == 0, prev0, pp)
        carry_ref[...] = pc[tm - SUBLANES:]
    else:
        n_st = tm // seq
        er = lax.broadcasted_iota(jnp.int32, (tm, n_st), 0)
        ec = lax.broadcasted_iota(jnp.int32, (tm, n_st), 1)
        expand = (er == ec * seq).astype(BF16)
        pp = jnp.where(rowid % seq == 0, _dot_exact_lhs(expand, pst), pp)
    rw = pc + (pp - pc) * mu_ref[...]
    W = RWKV_WIDTH
    r = rw[:, 0:W]
    k = rw[:, W:2 * W]
    v = rw[:, 2 * W:3 * W]
    gl = rw[:, 3 * W:3 * W + D_GATE_LORA]
    wa = rw[:, 3 * W + D_GATE_LORA:3 * W + 2 * LANES]
    w0 = vec_ref[0:1, :]
    a0 = vec_ref[1:2, :]
    k_k = vec_ref[2:3, :]
    k_a = vec_ref[3:4, :]
    r_k = vec_ref[4:5, :]

    lane = lax.broadcasted_iota(jnp.int32, wa.shape, 1)
    wa_t = jnp.where(lane < D_DECAY_LORA, jnp.tanh(wa), wa)
    da = _bdot(wa_t, wda_ref[...])
    lw = -DECAY_SCALE * jax.nn.sigmoid(w0 + da[:, :W])
    a = jax.nn.sigmoid(a0 + da[:, W:])
    g = _bdot(jax.nn.sigmoid(gl), g2_ref[...])
    if not first:
        mv = rw[:, 3 * W + 2 * LANES:]
        v0 = vec_ref[5:6, :]
        v = v + (vfirst_ref[...] - v) * jax.nn.sigmoid(v0 + _bdot(mv, v2_ref[...]))
    bd = bd_ref[...]
    kk = k * k_k
    ssq = _dot_exact_rhs(kk * kk, bd)
    kk = kk / jnp.maximum(jnp.sqrt(ssq), 1e-12)
    k = k * (1.0 + (a - 1.0) * k_a)
    bonus = _dot_exact_rhs(r * k * r_k, bd) * v

    r_ref[...] = r
    lw_ref[...] = lw
    k_ref[...] = k
    v_ref[...] = v
    a_ref[...] = -kk
    b_ref[...] = kk * a
    g_ref[...] = g
    bonus_ref[...] = bonus


def _inproj(x, shift_state, wts, v_first, seq):
    n = x.shape[0]
    first = v_first is None
    tm = ROW_TILE
    row = lambda w: pl.BlockSpec((tm, w), lambda i: (i, 0))
    if seq >= tm:
        assert seq % tm == 0
        st = shift_state.reshape(-1, 1, D_MODEL)
        st_spec = pl.BlockSpec((1, 1, D_MODEL), lambda i: (i // (seq // tm), 0, 0))
    else:
        assert tm % seq == 0
        st = shift_state
        st_spec = pl.BlockSpec((tm // seq, D_MODEL), lambda i: (i, 0))
    ins = [x, st, wts["watt"], wts["wrw"], wts["mu"], wts["wda"], wts["g2"], wts["vecA"], wts["bd"]]
    in_specs = [row(D_MODEL), st_spec] + [_const_spec(a.shape) for a in ins[2:]]
    if not first:
        ins += [v_first, wts["v2"]]
        in_specs += [row(RWKV_WIDTH), _const_spec(wts["v2"].shape)]
    widths = [ATT_WIDTH, KV_WIDTH, KV_WIDTH] + [RWKV_WIDTH] * 8
    return pl.pallas_call(
        functools.partial(_inproj_kernel, first=first, seq=seq),
        grid=(n // tm,),
        in_specs=in_specs,
        out_specs=[row(w) for w in widths],
        out_shape=[jax.ShapeDtypeStruct((n, w), F32) for w in widths],
        scratch_shapes=[pltpu.VMEM((SUBLANES, wts["wrw"].shape[1]), F32)],
        compiler_params=_params("arbitrary"),
        name="inproj",
    )(*ins)


def _dot3s(a_sp, b_sp, ca=1, cb=0):
    (ah, al), (bh, bl) = a_sp, b_sp
    return _dg(ah, bh, ca, cb) + (_dg(ah, bl, ca, cb) + _dg(al, bh, ca, cb))


def _dot1s(a_sp, b_sp, ca=1, cb=0):
    return _dg(a_sp[0], b_sp[0], ca, cb)


def _wkv_kernel(*refs, group, seq_chunk, has_state):
    if has_state:
        r_ref, lw_ref, k_ref, v_ref, a_ref, b_ref, s0_ref, y_ref, sout_ref, s_scr = refs
    else:
        r_ref, lw_ref, k_ref, v_ref, a_ref, b_ref, y_ref, sout_ref, s_scr = refs
    G, Ls = group, seq_chunk
    L = G * Ls
    H, N = N_RWKV_HEADS, HEAD_DIM
    c_idx = pl.program_id(1)

    @pl.when(c_idx == 0)
    def _init():
        if has_state:
            s_scr[...] = s0_ref[...]
        else:
            s_scr[...] = jnp.zeros_like(s_scr)

    row = lax.broadcasted_iota(jnp.int32, (L, L), 0)
    col = lax.broadcasted_iota(jnp.int32, (L, L), 1)
    same = (row // Ls) == (col // Ls)
    incl = same & (row >= col)
    strict = same & (row > col)
    eye = (row == col).astype(F32)
    row2 = lax.broadcasted_iota(jnp.int32, (2 * L, 2 * L), 0)
    col2 = lax.broadcasted_iota(jnp.int32, (2 * L, 2 * L), 1)
    t_q = row2 % L
    t_k = col2 % L
    mask2 = ((t_q // Ls) == (t_k // Ls)) & ((t_q > t_k) | ((row2 >= L) & (t_q == t_k)))

    lw_all = lw_ref[...]
    cum_all = _dot_exact_lhs(incl.astype(BF16), lw_all)
    if G == 1:
        tot_all = jnp.broadcast_to(cum_all[L - 1:L, :], cum_all.shape)
    else:
        tot_all = _dot_exact_lhs(same.astype(BF16), lw_all)
    e_end = jnp.exp(tot_all - cum_all)
    etot_all = jnp.exp(tot_all)
    a_all = a_ref[...]
    r_all = r_ref[...]
    b_all = b_ref[...]
    k_all = k_ref[...]
    v_all = v_ref[...]
    ar_all = jnp.concatenate([a_all * jnp.exp(cum_all - lw_all), r_all * jnp.exp(cum_all)], axis=0)
    ar_sp = _split2(ar_all)
    if G == 1:
        mid = cum_all[L // 2 - 1:L // 2, :]
        e_neg = jnp.exp(mid - cum_all)
        arc_sp = _split2(jnp.concatenate([a_all * jnp.exp(cum_all - lw_all - mid), r_all * jnp.exp(cum_all - mid)],
                                         axis=0))
    else:
        e_neg = jnp.exp(-cum_all)
        arc_sp = ar_sp
    bk_sp = _split2(jnp.concatenate([b_all * e_neg, k_all * e_neg], axis=0))
    bkh_sp = _split2(jnp.concatenate([b_all * e_end, k_all * e_end], axis=0))
    heads = range(H)
    hsl = lambda t, h: t[:, h * N:(h + 1) * N]
    hsp = lambda sp, h: (hsl(sp[0], h), hsl(sp[1], h))

    m = [jnp.where(mask2, _dg(hsl(arc_sp[0], h), hsl(bk_sp[0], h), 1, 1), 0.0) for h in heads]
    m_sp = [_split2(t) for t in m]
    a_ab = [t[:L, :L] for t in m]
    d = [eye + jnp.where((row // 2) == (col // 2), t, 0.0) for t in a_ab]
    n = 4
    while n <= Ls:
        off = ((row // n) == (col // n)) & ((row // (n // 2)) != (col // (n // 2)))
        d_b = [t.astype(BF16) for t in d]
        dn = [_dg(t, jnp.where(off, s, 0.0).astype(BF16), 1, 0) for t, s in zip(d_b, a_ab)]
        d = [t + _dg(p.astype(BF16), tb, 1, 0) for t, p, tb in zip(d, dn, d_b)]
        n *= 2
    resid = [eye - t + _dot3s((s[0][:L, :L], s[1][:L, :L]), _split2(t)) for t, s in zip(d, m_sp)]
    t_inv = [t + _dg(t.astype(BF16), r.astype(BF16), 1, 0) for t, r in zip(d, resid)]

    s0 = [[s_scr[g, h] for h in heads] for g in range(G)]
    s0_sp = [[_split2(s0[g][h]) for h in heads] for g in range(G)]

    def seq_rows(sp, g, h):
        if G == 1:
            return hsp(sp, h)
        return tuple(jnp.concatenate([hsl(p, h)[g * Ls:(g + 1) * Ls], hsl(p, h)[L + g * Ls:L + (g + 1) * Ls]], axis=0)
                     for p in sp)

    st = [[_dot1s(seq_rows(ar_sp, g, h), s0_sp[g][h], 1, 1) for h in heads] for g in range(G)]
    rhs_st = [jnp.concatenate([st[g][h][:Ls] for g in range(G)], axis=0) for h in heads]
    y_st = [jnp.concatenate([st[g][h][Ls:] for g in range(G)], axis=0) for h in heads]

    v_h = [hsl(v_all, h) for h in heads]
    zeros = jnp.zeros((L, N), F32)
    m_top = [(s[0][:L], s[1][:L]) for s in m_sp]
    m_bot = [(s[0][L:], s[1][L:]) for s in m_sp]
    rhs = [rhs_st[h] + _dot1s(m_top[h], _split2(jnp.concatenate([zeros, v_h[h]], axis=0))) for h in heads]
    u = [_dot1s(_split2(t_inv[h]), _split2(rhs[h])) for h in heads]
    uv_sp = [_split2(jnp.concatenate([u[h], v_h[h]], axis=0)) for h in heads]
    y = [y_st[h] + _dg(m_bot[h][0], uv_sp[h][0], 1, 0) for h in heads]
    for g in range(G):
        for h in heads:
            if G == 1:
                uv_g = uv_sp[h]
            else:
                uv_g = tuple(jnp.concatenate([p[g * Ls:(g + 1) * Ls], p[L + g * Ls:L + (g + 1) * Ls]], axis=0)
                             for p in uv_sp[h])
            upd = _dot1s(uv_g, seq_rows(bkh_sp, g, h), 0, 0)
            s_scr[g, h] = s0[g][h] * hsl(etot_all, h)[g * Ls:g * Ls + 1] + upd
    ys = []
    for h in heads:
        mu = jnp.mean(y[h], axis=-1, keepdims=True)
        var = jnp.mean(jnp.square(y[h] - mu), axis=-1, keepdims=True)
        ys.append((y[h] - mu) * lax.rsqrt(var + GN_EPS))
    y_ref[...] = jnp.concatenate(ys, axis=1)

    @pl.when(c_idx == pl.num_programs(1) - 1)
    def _fin():
        sout_ref[...] = s_scr[...]


def _wkv_scan(r, lw, k, v, a, b, state, n_seq, seq, seq_chunk, group):
    n_chunks = seq // seq_chunk
    assert group == 1 or n_chunks == 1
    has_state = state is not None
    rows = group * seq_chunk
    tok_spec = pl.BlockSpec((rows, RWKV_WIDTH), lambda i, c: (i * n_chunks + c, 0))
    st_spec = pl.BlockSpec((group, N_RWKV_HEADS, HEAD_DIM, HEAD_DIM), lambda i, c: (i, 0, 0, 0))
    in_specs = [tok_spec] * 6
    args = (r, lw, k, v, a, b)
    if has_state:
        states, layer = state
        in_specs.append(pl.BlockSpec((None, group, N_RWKV_HEADS, HEAD_DIM, HEAD_DIM),
                                     lambda i, c: (layer, i, 0, 0, 0)))
        args += (states,)
    return pl.pallas_call(
        functools.partial(_wkv_kernel, group=group, seq_chunk=seq_chunk, has_state=has_state),
        grid=(n_seq // group, n_chunks),
        in_specs=in_specs,
        out_specs=[tok_spec, st_spec],
        out_shape=[jax.ShapeDtypeStruct((n_seq * seq, RWKV_WIDTH), F32),
                   jax.ShapeDtypeStruct((n_seq, N_RWKV_HEADS, HEAD_DIM, HEAD_DIM), F32)],
        scratch_shapes=[pltpu.VMEM((group, N_RWKV_HEADS, HEAD_DIM, HEAD_DIM), F32)],
        compiler_params=_params("arbitrary", "arbitrary"),
        name="wkv_scan",
    )(*args)


def _sink_softmax(s, sink):
    m = sink
    for t in s:
        m = jnp.maximum(m, jnp.max(t, axis=-1, keepdims=True))
    es = [jnp.exp(t - m) for t in s]
    den = jnp.exp(sink - m)
    for e in es:
        den = den + jnp.sum(e, axis=-1, keepdims=True)
    return [e / den for e in es]


def _attn_prompt_kernel(q_ref, kc_ref, kp_ref, vc_ref, vp_ref, sink_ref, o_ref):
    L = WINDOW
    QB = ATT_Q_BLOCKS
    n = pl.program_id(1)
    q = q_ref[...].astype(BF16)
    kc = kc_ref[...].astype(BF16)
    vc = vc_ref[...].astype(BF16)
    kws = [jnp.concatenate([kp_ref[...].astype(BF16), kc[:L]], axis=0)] + [kc[(j - 1) * L:(j + 1) * L] for j in range(1, QB)]
    vws = [jnp.concatenate([vp_ref[...].astype(BF16), vc[:L]], axis=0)] + [vc[(j - 1) * L:(j + 1) * L] for j in range(1, QB)]
    qi = lax.broadcasted_iota(jnp.int32, (L, 2 * L), 0)
    kj = lax.broadcasted_iota(jnp.int32, (L, 2 * L), 1)
    diff = qi + L - kj
    band = (diff >= 0) & (diff < WINDOW)
    masks = [band & ((kj >= L) | (n > 0))] + [band] * (QB - 1)
    lane = lax.broadcasted_iota(jnp.int32, vws[0].shape, 1)
    one = jnp.ones_like(vws[0])
    vexts = [[jnp.where(lane < HEAD_DIM, vw, one), jnp.where(lane >= HEAD_DIM, vw, one)] for vw in vws]
    hsl = lambda t, h: t[:, h * HEAD_DIM:(h + 1) * HEAD_DIM]
    items = [(j, h) for j in range(QB) for h in range(N_Q_HEADS)]
    sinks = [sink_ref[h:h + 1, 0:1] for h in range(N_Q_HEADS)]
    ss = [jnp.where(masks[j], _dg(hsl(q[j * L:(j + 1) * L], h), hsl(kws[j], h // Q_PER_KV), 1, 1) * ATT_SCALE, -jnp.inf)
          for j, h in items]
    ms = [jnp.maximum(jnp.max(ss[i], axis=-1, keepdims=True), sinks[h]) for i, (j, h) in enumerate(items)]
    es = [jnp.exp(ss[i] - ms[i]).astype(BF16) for i in range(len(items))]
    oes = [_dg(es[i], vexts[j][h // Q_PER_KV], 1, 0) for i, (j, h) in enumerate(items)]
    outs = []
    for i, (j, h) in enumerate(items):
        hk = h // Q_PER_KV
        rs = oes[i][:, (1 - hk) * HEAD_DIM:(1 - hk) * HEAD_DIM + 1]
        outs.append(hsl(oes[i], hk) / (rs + jnp.exp(sinks[h] - ms[i])))
    o_ref[...] = jnp.concatenate([jnp.concatenate(outs[j * N_Q_HEADS:(j + 1) * N_Q_HEADS], axis=1) for j in range(QB)],
                                 axis=0)


def _attn_prompt(q, k, v, sink_rows, n_batch, seq):
    nb = seq // WINDOW
    QB = ATT_Q_BLOCKS
    assert nb % QB == 0
    npair = nb // QB
    cur = lambda w: pl.BlockSpec((QB * WINDOW, w), lambda b, n: (b * npair + n, 0))
    prev = lambda w: pl.BlockSpec((WINDOW, w), lambda b, n: (b * nb + jnp.maximum(QB * n - 1, 0), 0))
    return pl.pallas_call(
        _attn_prompt_kernel,
        grid=(n_batch, npair),
        in_specs=[cur(ATT_WIDTH), cur(KV_WIDTH), prev(KV_WIDTH), cur(KV_WIDTH), prev(KV_WIDTH),
                  _const_spec(sink_rows.shape)],
        out_specs=cur(ATT_WIDTH),
        out_shape=jax.ShapeDtypeStruct((n_batch * seq, ATT_WIDTH), F32),
        compiler_params=_params("arbitrary", "arbitrary"),
        name="attn_prompt",
    )(q, k, k, v, v, sink_rows)


def _attn_sample_kernel(q_ref, kn_ref, vn_ref, ck_ref, cv_ref, sink_ref, o_ref, *, seq):
    S = seq
    W = WINDOW
    B = SAMPLE_ATT_BATCH
    R = Q_PER_KV * S
    tq1 = lax.broadcasted_iota(jnp.int32, (R, W), 0) % S
    kj1 = lax.broadcasted_iota(jnp.int32, (R, W), 1)
    mask_cache = kj1 > tq1
    tq2 = lax.broadcasted_iota(jnp.int32, (R, S), 0) % S
    kj2 = lax.broadcasted_iota(jnp.int32, (R, S), 1)
    mask_new = kj2 <= tq2
    kvs = range(N_KV_HEADS)
    hsl = lambda t, h: t[:, h * HEAD_DIM:(h + 1) * HEAD_DIM]
    group = lambda hk: range(hk * Q_PER_KV, (hk + 1) * Q_PER_KV)
    seq_rows = lambda t, b: t[b * S:(b + 1) * S]
    q = q_ref[...]
    kn = kn_ref[...]
    vn = vn_ref[...]
    qh = [hsl(q, h) for h in range(N_Q_HEADS)]
    knh = [hsl(kn, hk) for hk in kvs]
    vnh = [hsl(vn, hk) for hk in kvs]
    sinks = [jnp.concatenate([jnp.broadcast_to(sink_ref[h:h + 1, 0:1], (S, 1)) for h in group(hk)], axis=0)
             for hk in kvs]
    items = [(b, hk) for b in range(B) for hk in kvs]
    qs = [jnp.concatenate([seq_rows(qh[h], b) for h in group(hk)], axis=0).astype(BF16) for b, hk in items]
    ck = [ck_ref[b].astype(BF16) for b in range(B)]
    cv = [cv_ref[b].astype(BF16) for b in range(B)]
    s1 = [jnp.where(mask_cache, _dg(qs[i], hsl(ck[b], hk), 1, 1) * ATT_SCALE, -jnp.inf)
          for i, (b, hk) in enumerate(items)]
    s2 = [jnp.where(mask_new, _dg(qs[i], seq_rows(knh[hk], b).astype(BF16), 1, 1) * ATT_SCALE, -jnp.inf)
          for i, (b, hk) in enumerate(items)]
    ps = [_sink_softmax([s1[i], s2[i]], sinks[hk]) for i, (b, hk) in enumerate(items)]
    outs = [_dg(ps[i][0].astype(BF16), hsl(cv[b], hk), 1, 0)
            + _dg(ps[i][1].astype(BF16), seq_rows(vnh[hk], b).astype(BF16), 1, 0)
            for i, (b, hk) in enumerate(items)]
    rows = [jnp.concatenate([outs[b * N_KV_HEADS + hk][g * S:(g + 1) * S] for hk in kvs for g in range(Q_PER_KV)],
                            axis=1) for b in range(B)]
    o_ref[...] = jnp.concatenate(rows, axis=0)


def _attn_sample(q, k, v, cache_k, cache_v, sink_rows, n_batch, seq):
    bb = SAMPLE_ATT_BATCH
    tok = lambda w: pl.BlockSpec((bb * seq, w), lambda i: (i, 0))
    cache = pl.BlockSpec((bb, WINDOW, KV_WIDTH), lambda i: (i, 0, 0))
    return pl.pallas_call(
        functools.partial(_attn_sample_kernel, seq=seq),
        grid=(n_batch // bb,),
        in_specs=[tok(ATT_WIDTH), tok(KV_WIDTH), tok(KV_WIDTH), cache, cache, _const_spec(sink_rows.shape)],
        out_specs=tok(ATT_WIDTH),
        out_shape=jax.ShapeDtypeStruct((n_batch * seq, ATT_WIDTH), F32),
        compiler_params=_params("arbitrary"),
        name="attn_sample",
    )(q, k, v, cache_k, cache_v, sink_rows)


def _second_max4(a, b, c, d):
    return jnp.maximum(jnp.maximum(jnp.minimum(a, b), jnp.minimum(c, d)),
                       jnp.minimum(jnp.maximum(a, b), jnp.maximum(c, d)))


def _route(logits_t, bias_col):
    G, E = N_EXPERT_GROUPS, EXPERTS_PER_GROUP
    m = jnp.max(logits_t, axis=0, keepdims=True)
    ex = jnp.exp(logits_t - m)
    probs = ex / jnp.sum(ex, axis=0, keepdims=True)
    sel = probs + bias_col
    p = [probs[e:e + 1, :] for e in range(N_EXPERTS)]
    s = [sel[e:e + 1, :] for e in range(N_EXPERTS)]
    gs = []
    for g in range(G):
        a, b, c, d = s[E * g:E * g + E]
        top1 = jnp.maximum(jnp.maximum(a, b), jnp.maximum(c, d))
        gs.append(top1 + _second_max4(a, b, c, d))
    best = jnp.zeros_like(gs[0], dtype=jnp.int32)
    best_s = gs[0]
    for g in range(1, G):
        upd = gs[g] > best_s
        best = jnp.where(upd, g, best)
        best_s = jnp.where(upd, gs[g], best_s)

    def pick(vals, j):
        out = vals[j]
        for g in range(1, G):
            out = jnp.where(best == g, vals[E * g + j], out)
        return out

    ig = [pick(s, j) for j in range(E)]
    pg = [pick(p, j) for j in range(E)]
    l1 = jnp.zeros_like(best)
    v1 = ig[0]
    for j in range(1, E):
        upd = ig[j] > v1
        l1 = jnp.where(upd, j, l1)
        v1 = jnp.where(upd, ig[j], v1)
    l2 = jnp.full_like(best, -1)
    v2 = jnp.full_like(v1, -jnp.inf)
    for j in range(E):
        upd = (l1 != j) & (ig[j] > v2)
        l2 = jnp.where(upd, j, l2)
        v2 = jnp.where(upd, ig[j], v2)
    zero = jnp.zeros_like(v1)
    w1 = zero
    w2 = zero
    for j in range(E):
        w1 = jnp.where(l1 == j, pg[j], w1)
        w2 = jnp.where(l2 == j, pg[j], w2)
    wsum = w1 + w2
    w1 = w1 / wsum
    w2 = w2 / wsum
    e1 = (best * E + l1).astype(F32)
    e2 = (best * E + l2).astype(F32)
    return jnp.concatenate([e1, e2, w1, w2, zero, zero, zero, zero], axis=0)


def _post_kernel(yn_ref, bonus_ref, g_ref, att_ref, x_ref, woa_ref, wor_ref, gn_ref, ln_ref,
                 wrt_ref, rb_ref, x1_ref, gates_ref):
    rw_out = (yn_ref[...] * gn_ref[0:1, :] + gn_ref[1:2, :] + bonus_ref[...]) * g_ref[...]
    mixed = _bdot(att_ref[...], woa_ref[...]) + _bdot(rw_out, wor_ref[...])
    x1 = _layer_norm(ALPHA * x_ref[...] + mixed, ln_ref[0:1, :], ln_ref[1:2, :])
    x1_ref[...] = x1
    logits_t = _dot3(wrt_ref[...], x1, 1, 1)
    route_t = _route(logits_t, rb_ref[:, 0:1])
    pad = jnp.zeros((LANES - route_t.shape[0], route_t.shape[1]), F32)
    gates_ref[...] = jnp.concatenate([route_t, pad], axis=0).T


def _post(yn, bonus, g, att, x, wts, glob):
    n = x.shape[0]
    tm = ROW_TILE
    row = lambda w: pl.BlockSpec((tm, w), lambda i: (i, 0))
    consts = [wts["woa"], wts["wor"], wts["gn"], wts["ln1"], glob["wrt"], glob["rb"]]
    return pl.pallas_call(
        _post_kernel,
        grid=(n // tm,),
        in_specs=[row(RWKV_WIDTH)] * 3 + [row(ATT_WIDTH), row(D_MODEL)] + [_const_spec(a.shape) for a in consts],
        out_specs=[row(D_MODEL), row(LANES)],
        out_shape=[jax.ShapeDtypeStruct((n, D_MODEL), F32), jax.ShapeDtypeStruct((n, LANES), F32)],
        compiler_params=_params("arbitrary"),
        name="post_mix",
    )(yn, bonus, g, att, x, *consts)


def _moe_positions_kernel(gates_ref, pos_ref, cnt_ref, cnt_scr, offs_scr, carry_scr):
    ph = pl.program_id(0)
    i = pl.program_id(1)
    g = gates_ref[...]
    tm = g.shape[0]
    lane = lax.broadcasted_iota(jnp.int32, g.shape, 1).astype(F32)
    oh1 = (lane == g[:, 0:1]).astype(F32)
    oh2 = (lane == g[:, 1:2]).astype(F32)
    oh = oh1 + oh2

    @pl.when((ph == 0) & (i == 0))
    def _zero():
        cnt_scr[...] = jnp.zeros_like(cnt_scr)

    @pl.when(ph == 0)
    def _count():
        cnt_scr[...] += jnp.sum(oh, axis=0, keepdims=True)
        pos_ref[...] = jnp.zeros_like(pos_ref)

    @pl.when((ph == 1) & (i == 0))
    def _offsets():
        cnt = cnt_scr[...]
        padded = jnp.floor((cnt + (MOE_BLOCK - 1)) * (1.0 / MOE_BLOCK)) * MOE_BLOCK
        r = lax.broadcasted_iota(jnp.int32, (LANES, LANES), 0)
        c = lax.broadcasted_iota(jnp.int32, (LANES, LANES), 1)
        offs_scr[...] = _dot_exact_rhs(padded, (r < c).astype(BF16))
        carry_scr[...] = jnp.zeros_like(carry_scr)
        cnt_ref[...] = cnt

    @pl.when(ph == 1)
    def _rank():
        r = lax.broadcasted_iota(jnp.int32, (tm, tm), 0)
        c = lax.broadcasted_iota(jnp.int32, (tm, tm), 1)
        before = _dg((r > c).astype(BF16), oh.astype(BF16), 1, 0)
        base = before + carry_scr[0:1, :] + offs_scr[0:1, :]
        p1 = jnp.sum(oh1 * base, axis=1, keepdims=True)
        p2 = jnp.sum(oh2 * base, axis=1, keepdims=True)
        pos_ref[...] = jnp.where(lane == 0.0, p1, jnp.where(lane == 1.0, p2, 0.0))
        carry_scr[...] += jnp.sum(oh, axis=0, keepdims=True)


def _moe_positions(gates):
    n = gates.shape[0]
    tm = min(MOE_POS_TILE, n)
    stat = pltpu.VMEM((SUBLANES, LANES), F32)
    return pl.pallas_call(
        _moe_positions_kernel,
        grid=(2, n // tm),
        in_specs=[pl.BlockSpec((tm, LANES), lambda ph, i: (i, 0))],
        out_specs=[pl.BlockSpec((tm, LANES), lambda ph, i: (i * ph, 0)), _const_spec((SUBLANES, LANES))],
        out_shape=[jax.ShapeDtypeStruct((n, LANES), F32), jax.ShapeDtypeStruct((SUBLANES, LANES), F32)],
        scratch_shapes=[stat, stat, stat],
        compiler_params=_params("arbitrary", "arbitrary"),
        name="moe_positions",
    )(gates)


def _moe_dispatch_kernel(ends_ref, pos_ref, x_ref, xs_ref, zero_scr, sem):
    tm = x_ref.shape[0]
    R = zero_scr.shape[0]

    @pl.when(pl.program_id(0) == 0)
    def _zero_tail_blocks():
        zero_scr[...] = jnp.zeros_like(zero_scr)

        def tail_copy(e):
            return pltpu.make_async_copy(zero_scr, xs_ref.at[pl.ds((ends_ref[e] - 1) * R, R), :], sem)

        def non_empty(e):
            return ends_ref[e] > (ends_ref[e - 1] if e else 0)

        n_blocks = xs_ref.shape[0] // R
        n_used = ends_ref[N_EXPERTS - 1]

        def spare_copy(j):
            return pltpu.make_async_copy(zero_scr, xs_ref.at[pl.ds((n_used + j) * R, R), :], sem)

        for e in range(N_EXPERTS):
            pl.when(non_empty(e))(lambda e=e: tail_copy(e).start())
            pl.when(n_used + e < n_blocks)(lambda e=e: spare_copy(e).start())
        for e in range(N_EXPERTS):
            pl.when(non_empty(e))(lambda e=e: tail_copy(e).wait())
            pl.when(n_used + e < n_blocks)(lambda e=e: spare_copy(e).wait())

    def row_copy(r, k):
        return pltpu.make_async_copy(x_ref.at[pl.ds(r, 1), :], xs_ref.at[pl.ds(pos_ref[k, r], 1), :], sem)

    def issue(r, carry):
        row_copy(r, 0).start(priority=0)
        row_copy(r, 1).start(priority=1)
        return carry

    lax.fori_loop(0, tm, issue, 0, unroll=DMA_ISSUE_UNROLL)
    for _ in range(N_EXPERTS_PER_TOKEN):
        pltpu.make_async_copy(x_ref, xs_ref.at[pl.ds(0, tm), :], sem).wait()


def _moe_dispatch(x1, pos, ends, n_slots):
    n = x1.shape[0]
    tm = MOE_DMA_TILE
    return pl.pallas_call(
        _moe_dispatch_kernel,
        grid=(n // tm,),
        in_specs=[pl.BlockSpec(memory_space=pltpu.SMEM),
                  pl.BlockSpec((2, tm), lambda i: (0, i), memory_space=pltpu.SMEM),
                  pl.BlockSpec((tm, D_MODEL), lambda i: (i, 0))],
        out_specs=pl.BlockSpec(memory_space=pl.ANY),
        out_shape=jax.ShapeDtypeStruct((n_slots, D_MODEL), F32),
        scratch_shapes=[pltpu.VMEM((MOE_BLOCK, D_MODEL), F32), pltpu.SemaphoreType.DMA],
        compiler_params=_params("arbitrary"),
        name="moe_dispatch",
    )(ends, pos, x1)


def _moe_experts_kernel(blk_expert_ref, n_used_ref, xs_ref, wg_ref, wu_ref, wd_ref, ys_ref, wg_b, wu_b, wd_b):
    b = pl.program_id(0)
    used = b < n_used_ref[0]
    new_expert = (b == 0) | (blk_expert_ref[b] != blk_expert_ref[jnp.maximum(b - 1, 0)])

    @pl.when(used & new_expert)
    def _cast_weights():
        wg_b[...] = wg_ref[0, 0].astype(BF16)
        wu_b[...] = wu_ref[0, 0].astype(BF16)
        wd_b[...] = wd_ref[0, 0].astype(BF16)

    @pl.when(used)
    def _compute():
        xb = xs_ref[...].astype(BF16)
        h = jax.nn.silu(_dg(xb, wg_b[...], 1, 0)) * _dg(xb, wu_b[...], 1, 0)
        ys_ref[...] = _dg(h.astype(BF16), wd_b[...], 1, 0)

    @pl.when(b >= n_used_ref[0])
    def _skip():
        ys_ref[...] = jnp.zeros_like(ys_ref)


def _moe_experts(xs, blk_expert, n_used, glob, l):
    n_slots = xs.shape[0]
    R = MOE_BLOCK
    last = lambda b, nu: jnp.minimum(b, nu[0] - 1)
    wspec = lambda s: pl.BlockSpec((1, 1) + s, lambda b, be, nu: (l, be[last(b, nu)], 0, 0))
    return pl.pallas_call(
        _moe_experts_kernel,
        grid_spec=pltpu.PrefetchScalarGridSpec(
            num_scalar_prefetch=2,
            grid=(n_slots // R,),
            in_specs=[pl.BlockSpec((R, D_MODEL), lambda b, be, nu: (last(b, nu), 0)),
                      wspec((D_MODEL, EXPERT_FF)), wspec((D_MODEL, EXPERT_FF)), wspec((EXPERT_FF, D_MODEL))],
            out_specs=pl.BlockSpec((R, D_MODEL), lambda b, be, nu: (b, 0)),
            scratch_shapes=[pltpu.VMEM((D_MODEL, EXPERT_FF), BF16), pltpu.VMEM((D_MODEL, EXPERT_FF), BF16),
                            pltpu.VMEM((EXPERT_FF, D_MODEL), BF16)],
        ),
        out_shape=jax.ShapeDtypeStruct((n_slots, D_MODEL), F32),
        compiler_params=_params("arbitrary"),
        name="moe_experts",
    )(blk_expert, n_used, xs, glob["wg"], glob["wu"], glob["wd"])


def _moe_combine_kernel(pos_ref, x_ref, gates_ref, ys_ref, ln_ref, o_ref, buf, sem):
    tm = x_ref.shape[0]

    def row_copy(r, k):
        return pltpu.make_async_copy(ys_ref.at[pl.ds(pos_ref[k, r], 1), :], buf.at[k, pl.ds(r, 1), :], sem)

    def issue(r, carry):
        row_copy(r, 0).start(priority=0)
        row_copy(r, 1).start(priority=1)
        return carry

    lax.fori_loop(0, tm, issue, 0, unroll=DMA_ISSUE_UNROLL)
    for k in range(N_EXPERTS_PER_TOKEN):
        pltpu.make_async_copy(ys_ref.at[pl.ds(0, tm), :], buf.at[k], sem).wait()
    g = gates_ref[...]
    ffn = g[:, 2:3] * buf[0] + g[:, 3:4] * buf[1]
    o_ref[...] = _layer_norm(ALPHA * x_ref[...] + ffn, ln_ref[0:1, :], ln_ref[1:2, :])


def _moe_combine(x1, gates, pos, ys, wts):
    n = x1.shape[0]
    tm = MOE_DMA_TILE
    return pl.pallas_call(
        _moe_combine_kernel,
        grid=(n // tm,),
        in_specs=[pl.BlockSpec((2, tm), lambda i: (0, i), memory_space=pltpu.SMEM),
                  pl.BlockSpec((tm, D_MODEL), lambda i: (i, 0)),
                  pl.BlockSpec((tm, LANES), lambda i: (i, 0)),
                  pl.BlockSpec(memory_space=pl.ANY),
                  _const_spec(wts["ln2"].shape)],
        out_specs=pl.BlockSpec((tm, D_MODEL), lambda i: (i, 0)),
        out_shape=jax.ShapeDtypeStruct((n, D_MODEL), F32),
        scratch_shapes=[pltpu.VMEM((2, tm, D_MODEL), F32), pltpu.SemaphoreType.DMA],
        compiler_params=_params("arbitrary"),
        name="moe_combine",
    )(pos, x1, gates, ys, wts["ln2"])


def _moe(x1, gates, wts, glob, l):
    n = x1.shape[0]
    R = MOE_BLOCK
    n_slots = N_EXPERTS_PER_TOKEN * n + N_EXPERTS * R
    pos_f, cnt = _moe_positions(gates)
    pos = pos_f[:, :N_EXPERTS_PER_TOKEN].astype(jnp.int32).T
    blocks = (cnt[0, :N_EXPERTS].astype(jnp.int32) + (R - 1)) // R
    ends = jnp.cumsum(blocks)
    n_used = ends[-1:]
    blk_expert = jnp.sum(jnp.arange(n_slots // R, dtype=jnp.int32)[:, None] >= ends[None, :], axis=1)
    blk_expert = jnp.minimum(blk_expert, N_EXPERTS - 1).astype(jnp.int32)
    xs = _moe_dispatch(x1, pos, ends.astype(jnp.int32), n_slots)
    ys = _moe_experts(xs, blk_expert, n_used, glob, l)
    return _moe_combine(x1, gates, pos, ys, wts)


def _rows8(vectors, width):
    rows = [v.reshape(1, width).astype(F32) for v in vectors]
    rows.append(jnp.zeros((8 - len(rows), width), F32))
    return jnp.concatenate(rows, axis=0)


def _prep_layer(l, p):
    W = RWKV_WIDTH
    w_l = p["w_in"][l]
    rwc = w_l[:, N_ATT_COLS:]
    mu = p["mu_rwkv"][l]
    o_w, o_k, o_v, o_a, o_g = W, W + D_DECAY_LORA, 2 * W + D_DECAY_LORA, 3 * W + D_DECAY_LORA, 3 * W + 2 * D_DECAY_LORA

    def reorder(t):
        parts = [t[..., 0:W], t[..., o_k:o_k + W], t[..., o_v:o_v + W], t[..., o_g:o_g + D_GATE_LORA],
                 t[..., o_w:o_w + D_DECAY_LORA], t[..., o_a:o_a + D_AAA_LORA]]
        return parts

    w_parts = reorder(rwc)
    mu_parts = reorder(mu)
    if l > 0:
        padw = LANES - D_MV_LORA
        w_parts += [p["w_vres_in"][l - 1], jnp.zeros((D_MODEL, padw), F32)]
        mu_parts += [p["mu_vres"][l - 1], jnp.zeros((padw,), F32)]
    wrw = jnp.concatenate(w_parts, axis=1).astype(BF16)
    mu_row = jnp.concatenate(mu_parts).reshape(1, -1)
    zero = jnp.zeros((D_DECAY_LORA, W), F32)
    wda = jnp.concatenate([jnp.concatenate([p["decay_w2"][l], zero], axis=1),
                           jnp.concatenate([zero, p["aaa_a2"][l]], axis=1)], axis=0).astype(BF16)
    vecs = [p["decay_w0"][l], p["aaa_a0"][l], p["k_k"][l], p["k_a"][l], p["r_k"][l].reshape(W)]
    out = {
        "watt": w_l[:, :N_ATT_COLS].astype(BF16),
        "wrw": wrw,
        "mu": mu_row,
        "wda": wda,
        "g2": p["gate_g2"][l].astype(BF16),
        "woa": p["w_o"][l][:ATT_WIDTH].astype(BF16),
        "wor": p["w_o"][l][ATT_WIDTH:].astype(BF16),
        "gn": _rows8([p["gn_g"][l], p["gn_b"][l]], W),
        "ln1": _rows8([p["ln1_g"][l], p["ln1_b"][l]], D_MODEL),
        "ln2": _rows8([p["ln2_g"][l], p["ln2_b"][l]], D_MODEL),
        "sink_rows": jnp.broadcast_to(p["sinks"][l].reshape(N_Q_HEADS, 1), (N_Q_HEADS, LANES)).astype(F32),
    }
    if l > 0:
        vecs.append(p["vres_v0"][l - 1])
        out["v2"] = jnp.concatenate([p["vres_v2"][l - 1], jnp.zeros((LANES - D_MV_LORA, W), F32)], axis=0).astype(BF16)
    out["vecA"] = _rows8(vecs, W)
    hid = jnp.arange(W) // HEAD_DIM
    out["bd"] = (hid[:, None] == hid[None, :]).astype(BF16)
    return out


def _trunk(x3, shift_prev, cache_k, cache_v, wkv_prev, layer_wts, glob):
    decode = cache_k is not None
    n_batch, seq, _ = x3.shape
    x = x3.reshape(n_batch * seq, D_MODEL)
    new_k, new_v, new_wkv, new_shift = [], [], [], []
    v_first = None
    for l in range(DEPTH):
        wts = layer_wts[l]
        new_shift.append(x.reshape(n_batch, seq, D_MODEL)[:, -1])
        q, ka, va, r, lw, k, v, a, b, g, bonus = _inproj(x, shift_prev[l], wts, v_first, seq)
        if l == 0:
            v_first = v
        if decode:
            ck = cache_k[l].reshape(n_batch, WINDOW, KV_WIDTH)
            cv = cache_v[l].reshape(n_batch, WINDOW, KV_WIDTH)
            att = _attn_sample(q, ka, va, ck, cv, wts["sink_rows"], n_batch, seq)
            new_k.append(ka.reshape(n_batch, seq, N_KV_HEADS, HEAD_DIM))
            new_v.append(va.reshape(n_batch, seq, N_KV_HEADS, HEAD_DIM))
            yn, s_out = _wkv_scan(r, lw, k, v, a, b, (wkv_prev[l][None], 0), n_batch, seq, seq, SAMPLE_WKV_GROUP)
        else:
            att = _attn_prompt(q, ka, va, wts["sink_rows"], n_batch, seq)
            last = lambda t: t.reshape(n_batch, seq, KV_WIDTH)[:, -WINDOW:].reshape(n_batch, WINDOW, N_KV_HEADS, HEAD_DIM)
            new_k.append(last(ka))
            new_v.append(last(va))
            yn, s_out = _wkv_scan(r, lw, k, v, a, b, None, n_batch, seq, WKV_CHUNK, 1)
        new_wkv.append(s_out)
        x1, gates = _post(yn, bonus, g, att, x, wts, glob)
        x = _moe(x1, gates, wts, glob, l)
    return (x.reshape(n_batch, seq, D_MODEL), jnp.stack(new_k), jnp.stack(new_v), jnp.stack(new_wkv),
            jnp.stack(new_shift))


def kernel(x_prompt, x_sample, cache_k, cache_v, state_wkv, state_shift, w_in, w_vres_in, mu_rwkv, mu_vres, sinks, decay_w0, decay_w2, aaa_a0, aaa_a2, vres_v0, vres_v2, gate_g2, k_k, k_a, r_k, gn_g, gn_b, w_o, ln1_g, ln1_b, w_router, router_bias, w_gate, w_up, w_down, ln2_g, ln2_b):
    p = dict(w_in=w_in, w_vres_in=w_vres_in, mu_rwkv=mu_rwkv, mu_vres=mu_vres, sinks=sinks,
             decay_w0=decay_w0, decay_w2=decay_w2, aaa_a0=aaa_a0, aaa_a2=aaa_a2,
             vres_v0=vres_v0, vres_v2=vres_v2, gate_g2=gate_g2, k_k=k_k, k_a=k_a, r_k=r_k,
             gn_g=gn_g, gn_b=gn_b, w_o=w_o, ln1_g=ln1_g, ln1_b=ln1_b,
             w_gate=w_gate, w_up=w_up, w_down=w_down, ln2_g=ln2_g, ln2_b=ln2_b)
    layer_wts = [_prep_layer(l, p) for l in range(DEPTH)]
    glob = {
        "wrt": w_router.T.astype(F32),
        "rb": jnp.broadcast_to(router_bias.reshape(N_EXPERTS, 1), (N_EXPERTS, LANES)).astype(F32),
        "wg": w_gate,
        "wu": w_up,
        "wd": w_down,
    }
    b_p = x_prompt.shape[0]
    zero_shift = jnp.zeros((DEPTH, b_p, D_MODEL), x_prompt.dtype)
    y_p, k_p, v_p, wkv_p, shift_p = _trunk(x_prompt, zero_shift, None, None, None, layer_wts, glob)
    y_s, k_s, v_s, wkv_s, shift_s = _trunk(x_sample, state_shift, cache_k, cache_v, state_wkv, layer_wts, glob)
    return (y_p, y_s, k_p, v_p, wkv_p, shift_p, k_s, v_s, wkv_s, shift_s)
```
